```python
import math
import jax
import jax.numpy as jnp
from jax import lax
import numpy as np

D_MODEL = 2048
BATCH = 4
SEQ = 4096
DEPTH = 2

GRID_W = 64
CTX_LEN = 256

DN_ALPHA = (2 * DEPTH) ** 0.25
DN_BETA = (8 * DEPTH) ** -0.25
N_SUB = 3
N_MOD = 3 * N_SUB
FFN_HALF = 0.5
D_FF = 5632
LN_EPS = 1e-5
RMS_EPS = 1e-6

BRANCH_W = D_MODEL // 2
N_BRANCH = 3

ATT_DH = 64
ATT_DV = 2 * ATT_DH
ATT_HEADS = BRANCH_W // ATT_DV
ATT_QK_W = ATT_HEADS * 2 * ATT_DH
ATT_BLOCK = 128
ROPE_BASE = 10000.0
ROPE_FREQS = ATT_DH // 4
LAMBDA_INIT_BASE = 0.8
LAMBDA_INIT_SPAN = 0.6
LAMBDA_INIT_RATE = 0.3

SSD_P = 64
SSD_HEADS = BRANCH_W // SSD_P
SSD_GROUPS = 4
SSD_HPG = SSD_HEADS // SSD_GROUPS
SSD_N = 128
SSD_CONV = 5
SSD_CHUNK = 128
SSD_INNER = SSD_HEADS * SSD_P
SSD_BC_W = SSD_GROUPS * SSD_N
SSD_XBC_W = SSD_INNER + 2 * SSD_BC_W
SSD_NORM_GROUP = SSD_INNER // SSD_GROUPS

S5_CH = BRANCH_W
S5_GROUP_CH = 16
S5_GROUPS = S5_CH // S5_GROUP_CH
S5_N = 64

IN_SPLITS = (ATT_QK_W, ATT_QK_W, ATT_HEADS * ATT_DV, SSD_INNER, SSD_XBC_W,
             2 * SSD_HEADS, S5_CH, N_BRANCH * D_MODEL)
IN_COLS = sum(IN_SPLITS)

kernel_name = "hybrid_diffattn_ssd_s5_dit_trunk"


def split_cols(t, sizes):
    cuts = [int(v) for v in np.cumsum(sizes)[:-1]]
    return jnp.split(t, cuts, axis=-1)


def flip_time(t, direction):
    return jnp.flip(t, axis=1) if direction else t


def layer_norm(t, g, b):
    tf = t.astype(jnp.float32)
    mu = jnp.mean(tf, axis=-1, keepdims=True)
    var = jnp.mean(jnp.square(tf - mu), axis=-1, keepdims=True)
    return ((tf - mu) * lax.rsqrt(var + LN_EPS) * g + b).astype(t.dtype)


def rms_norm(t, w):
    tf = t.astype(jnp.float32)
    return (tf * lax.rsqrt(jnp.mean(jnp.square(tf), axis=-1, keepdims=True) + RMS_EPS) * w).astype(t.dtype)


def swiglu(h, w1, w3, w2):
    return (jax.nn.silu(h @ w1) * (h @ w3)) @ w2


def adaln(m, j):
    return m[..., 3 * j, :], m[..., 3 * j + 1, :], m[..., 3 * j + 2, :]


def modulate(t, shift, scale):
    return t * (1.0 + scale) + shift


def post_norm(t, update, g, b):
    return layer_norm(DN_ALPHA * t + update, g, b)


def half_ffn(t, m, j, w1, w3, w2, g, b):
    shift, scale, gate = adaln(m, j)
    return post_norm(t, FFN_HALF * gate * swiglu(modulate(t, shift, scale), w1, w3, w2), g, b)


def axial_rope_tables(seq_len):
    rows = seq_len // GRID_W
    row = jnp.repeat(jnp.arange(rows), GRID_W)
    col = jnp.tile(jnp.arange(GRID_W), rows)
    inv = ROPE_BASE ** (-jnp.arange(ROPE_FREQS, dtype=jnp.float32) / ROPE_FREQS)
    ang = jnp.stack([row[:, None] * inv, col[:, None] * inv], axis=1)
    return jnp.cos(ang), jnp.sin(ang)


def apply_axial_rope(t, cos, sin):
    tr = t.astype(jnp.float32).reshape(*t.shape[:-1], 2, 2, ROPE_FREQS)
    t1, t2 = tr[..., 0, :], tr[..., 1, :]
    cs, sn = cos[:, None, None], sin[:, None, None]
    out = jnp.stack([t1 * cs - t2 * sn, t2 * cs + t1 * sn], axis=-2)
    return out.reshape(t.shape).astype(t.dtype)


def diff_attention(q, k, v, qc, kc, vc, lam_vec, subln_w, lam_init):
    bsz, seq = q.shape[:2]
    cos, sin = axial_rope_tables(seq)
    q = apply_axial_rope(q, cos, sin)
    k = apply_axial_rope(k, cos, sin)
    lv = lam_vec.astype(jnp.float32)
    lam = jnp.exp(jnp.sum(lv[0] * lv[1])) - jnp.exp(jnp.sum(lv[2] * lv[3])) + lam_init
    k_all = jnp.concatenate([kc, k], axis=1)
    v_all = jnp.concatenate([vc, v], axis=1)

    def attend(qb, kk, vv):
        s = jnp.einsum('bqhjd,bkhjd->bhjqk', qb, kk, preferred_element_type=jnp.float32) * (ATT_DH ** -0.5)
        p = jax.nn.softmax(s, axis=-1)
        a = p[:, :, 0] - lam * p[:, :, 1]
        return jnp.einsum('bhqk,bkhe->bqhe', a.astype(vv.dtype), vv)

    n_blk = seq // ATT_BLOCK
    q_blocks = q.reshape(bsz, n_blk, ATT_BLOCK, ATT_HEADS, 2, ATT_DH).swapaxes(0, 1)
    o = lax.map(lambda qb: attend(qb, k_all, v_all), q_blocks)
    o = o.swapaxes(0, 1).reshape(bsz, seq, ATT_HEADS, ATT_DV)
    oc = attend(qc, kc, vc)

    def heads_out(t):
        return (rms_norm(t, subln_w) * (1.0 - lam_init)).reshape(*t.shape[:2], ATT_HEADS * ATT_DV)

    return heads_out(o), heads_out(oc)


def dwconv_centred(u, w, b):
    out = lax.conv_general_dilated(
        u, w[:, None, :].astype(u.dtype), window_strides=(1,),
        padding=((SSD_CONV // 2, SSD_CONV // 2),),
        dimension_numbers=('NWC', 'WIO', 'NWC'), feature_group_count=u.shape[-1])
    return out + b.astype(u.dtype)


def ssd_chunked(xs, dt, a, bm, cm, h0):
    bsz, T, G, E, P = xs.shape
    N = bm.shape[-1]
    L = SSD_CHUNK
    nc = T // L
    f32 = jnp.float32
    xdt = (xs.astype(f32) * dt[..., None]).reshape(bsz, nc, L, G, E, P)
    bc = bm.astype(f32).reshape(bsz, nc, L, G, N)
    cc = cm.astype(f32).reshape(bsz, nc, L, G, N)
    da = (dt * a).reshape(bsz, nc, L, G, E).transpose(0, 1, 3, 4, 2)
    da_cs = jnp.cumsum(da, axis=-1)
    seg = da_cs[..., :, None] - da_cs[..., None, :]
    lower = jnp.tril(jnp.ones((L, L), dtype=bool))
    decay = jnp.where(lower, jnp.exp(jnp.where(lower, seg, 0.0)), 0.0)
    cb = jnp.einsum('bclgn,bcsgn->bcgls', cc, bc)
    y_diag = jnp.einsum('bcgels,bcsgep->bclgep', cb[:, :, :, None] * decay, xdt)
    to_end = jnp.exp(da_cs[..., -1:] - da_cs).transpose(0, 1, 4, 2, 3)
    states = jnp.einsum('bclgn,bclgep->bcgepn', bc, xdt * to_end[..., None])
    chunk_decay = jnp.exp(da_cs[..., -1])

    def carry_step(h, inp):
        dec, st = inp
        return h * dec[..., None, None] + st, h

    h_last, h_prev = lax.scan(carry_step, h0.astype(f32),
                              (chunk_decay.swapaxes(0, 1), states.swapaxes(0, 1)))
    h_prev = h_prev.swapaxes(0, 1)
    from_start = jnp.exp(da_cs).transpose(0, 1, 4, 2, 3)[..., None]
    y_off = jnp.einsum('bclgn,bcgepn->bclgep', cc, h_prev) * from_start
    return (y_diag + y_off).reshape(bsz, T, G, E, P), h_last


def mamba2_mixer(z, xbc, dt, zc, xbcc, dtc, conv_w, conv_b, a_log, dt_bias, d_skip, norm_w):
    f32 = jnp.float32
    a = -jnp.exp(a_log.astype(f32)).reshape(2, SSD_GROUPS, SSD_HPG)
    dsk = d_skip.astype(f32).reshape(SSD_GROUPS, SSD_HPG, 1)

    def prep(xbc_t, dt_t):
        bsz, T = xbc_t.shape[:2]
        xbc_t = jax.nn.silu(dwconv_centred(xbc_t, conv_w, conv_b))
        xs, bm, cm = split_cols(xbc_t, (SSD_INNER, SSD_BC_W, SSD_BC_W))
        xs = xs.reshape(bsz, T, SSD_GROUPS, SSD_HPG, SSD_P)
        bm = bm.reshape(bsz, T, SSD_GROUPS, SSD_N)
        cm = cm.reshape(bsz, T, SSD_GROUPS, SSD_N)
        dts = jax.nn.softplus(dt_t.astype(f32).reshape(bsz, T, 2, SSD_GROUPS, SSD_HPG)
                              + dt_bias.astype(f32).reshape(2, SSD_GROUPS, SSD_HPG))
        return xs, bm, cm, dts

    xs, bm, cm, dts = prep(xbc, dt)
    xsc, bmc, cmc, dtsc = prep(xbcc, dtc)
    y = xs.astype(f32) * dsk
    yc = xsc.astype(f32) * dsk
    h0 = jnp.zeros((xs.shape[0], SSD_GROUPS, SSD_HPG, SSD_P, SSD_N), f32)
    for direction in range(2):
        fl = lambda t: flip_time(t, direction)
        y_c, h_c = ssd_chunked(fl(xsc), fl(dtsc[:, :, direction]), a[direction], fl(bmc), fl(cmc), h0)
        y_l, _ = ssd_chunked(fl(xs), fl(dts[:, :, direction]), a[direction], fl(bm), fl(cm), h_c)
        yc = yc + fl(y_c)
        y = y + fl(y_l)

    def gate_norm(y_t, z_t):
        bsz, T = z_t.shape[:2]
        gated = y_t.reshape(bsz, T, SSD_INNER) * jax.nn.silu(z_t.astype(f32))
        normed = rms_norm(gated.reshape(bsz, T, SSD_GROUPS, SSD_NORM_GROUP),
                          norm_w.reshape(SSD_GROUPS, SSD_NORM_GROUP))
        return normed.reshape(bsz, T, SSD_INNER).astype(z_t.dtype)

    return gate_norm(y, z), gate_norm(yc, zc)


def complex_affine_combine(e1, e2):
    a1r, a1i, b1r, b1i = e1
    a2r, a2i, b2r, b2i = e2
    return (a1r * a2r - a1i * a2i, a1r * a2i + a1i * a2r,
            a2r * b1r - a2i * b1i + b2r, a2r * b1i + a2i * b1r + b2i)


def s5_scan(u, lam_re, lam_im, log_step, b_re, b_im, c_re, c_im, h0_re, h0_im):
    T = u.shape[1]
    step = jnp.exp(log_step)[:, None]
    mag = jnp.exp(lam_re * step)
    ang = lam_im * step
    ab_re, ab_im = mag * jnp.cos(ang), mag * jnp.sin(ang)
    den = lam_re * lam_re + lam_im * lam_im
    k_re = ((ab_re - 1.0) * lam_re + ab_im * lam_im) / den
    k_im = (ab_im * lam_re - (ab_re - 1.0) * lam_im) / den
    bb_re = k_re[..., None] * b_re - k_im[..., None] * b_im
    bb_im = k_re[..., None] * b_im + k_im[..., None] * b_re
    bu_re = jnp.einsum('btgh,gnh->btgn', u, bb_re)
    bu_im = jnp.einsum('btgh,gnh->btgn', u, bb_im)
    a_re = jnp.broadcast_to(ab_re, (1, T) + ab_re.shape)
    a_im = jnp.broadcast_to(ab_im, (1, T) + ab_im.shape)
    p_re, p_im, h_re, h_im = lax.associative_scan(
        complex_affine_combine, (a_re, a_im, bu_re, bu_im), axis=1)
    h_re, h_im = (h_re + p_re * h0_re[:, None] - p_im * h0_im[:, None],
                  h_im + p_re * h0_im[:, None] + p_im * h0_re[:, None])
    y = jnp.einsum('btgn,ghn->btgh', h_re, c_re) - jnp.einsum('btgn,ghn->btgh', h_im, c_im)
    return y, h_re[:, -1], h_im[:, -1]


def s5_mixer(u, uc, lam_re, lam_im, log_step, b_re, b_im, c_re, c_im, d_skip, glu_w, glu_b):
    f32 = jnp.float32
    grp = lambda t: t.astype(f32).reshape(*t.shape[:2], S5_GROUPS, S5_GROUP_CH)
    ug, ucg = grp(u), grp(uc)
    dsk = d_skip.astype(f32).reshape(S5_GROUPS, S5_GROUP_CH)
    y, yc = ug * dsk, ucg * dsk
    zeros = jnp.zeros((u.shape[0], S5_GROUPS, S5_N), f32)
    for direction in range(2):
        fl = lambda t: flip_time(t, direction)
        prm = [t[direction].astype(f32) for t in (lam_re, lam_im, log_step, b_re, b_im, c_re, c_im)]
        y_c, hc_re, hc_im = s5_scan(fl(ucg), *prm, zeros, zeros)
        y_l, _, _ = s5_scan(fl(ug), *prm, hc_re, hc_im)
        yc = yc + fl(y_c)
        y = y + fl(y_l)

    def glu(t):
        t = jax.nn.gelu(t.reshape(*t.shape[:2], S5_CH))
        return (t * jax.nn.sigmoid(t @ glu_w + glu_b)).astype(u.dtype)

    return glu(y), glu(yc)


def token_mixer(h, hc, w_in, att_lam, att_subln, lam_init, conv_w, conv_b, a_log, dt_bias,
                ssd_d, ssd_norm, lam_re, lam_im, log_step, b_re, b_im, c_re, c_im,
                s5_d, glu_w, glu_b, w_branch, w_out):
    q, k, v, z, xbc, dt, u, g = split_cols(h @ w_in, IN_SPLITS)
    qc, kc, vc, zc, xbcc, dtc, uc, gc = split_cols(hc @ w_in, IN_SPLITS)
    qk_heads = lambda t: t.reshape(*t.shape[:2], ATT_HEADS, 2, ATT_DH)
    v_heads = lambda t: t.reshape(*t.shape[:2], ATT_HEADS, ATT_DV)
    o_att, oc_att = diff_attention(qk_heads(q), qk_heads(k), v_heads(v),
                                   qk_heads(qc), qk_heads(kc), v_heads(vc),
                                   att_lam, att_subln, lam_init)
    o_ssd, oc_ssd = mamba2_mixer(z, xbc, dt, zc, xbcc, dtc, conv_w, conv_b, a_log, dt_bias, ssd_d, ssd_norm)
    o_s5, oc_s5 = s5_mixer(u, uc, lam_re, lam_im, log_step, b_re, b_im, c_re, c_im, s5_d, glu_w, glu_b)

    def merge(branches, gate_logits):
        br = jnp.stack([t.astype(gate_logits.dtype) for t in branches], axis=-2)
        gates = jax.nn.sigmoid(gate_logits.reshape(*gate_logits.shape[:-1], N_BRANCH, D_MODEL))
        mixed = jnp.sum(gates * jnp.einsum('btjc,jcd->btjd', br, w_branch), axis=-2)
        return mixed @ w_out

    return merge((o_att, o_ssd, o_s5), g), merge((oc_att, oc_ssd, oc_s5), gc)


def setup_inputs(seed: int = 0) -> dict:
    key = jax.random.key(seed)
    keys = jax.random.split(key, 40)
    kit = iter(range(40))
    f32 = jnp.float32
    L, D = DEPTH, D_MODEL

    def nrm(shape, scale):
        return jax.random.normal(keys[next(kit)], shape, f32) * scale

    def unif(shape, lo, hi):
        return jax.random.uniform(keys[next(kit)], shape, f32, lo, hi)

    x = nrm((BATCH, SEQ, D), 1.0)
    c = nrm((BATCH, D), 1.0)
    ctx = nrm((BATCH, CTX_LEN, D), 1.0)
    c_ctx = nrm((D,), 1.0)
    w_mod = nrm((L, D, N_MOD * D), D ** -0.5)
    b_mod = nrm((L, N_MOD * D), 0.01)
    ln_g = 1.0 + nrm((L, N_SUB, D), 0.02)
    ln_b = nrm((L, N_SUB, D), 0.02)
    ffn_w1 = nrm((L, 2, D, D_FF), D ** -0.5)
    ffn_w3 = nrm((L, 2, D, D_FF), D ** -0.5)
    ffn_w2 = nrm((L, 2, D_FF, D), DN_BETA * D_FF ** -0.5)
    w_in = nrm((L, D, IN_COLS), D ** -0.5)
    att_lam = nrm((L, 4, ATT_DH), 0.1)
    att_subln = 1.0 + nrm((L, ATT_DV), 0.02)
    ssd_conv_w = nrm((L, SSD_CONV, SSD_XBC_W), SSD_CONV ** -0.5)
    ssd_conv_b = nrm((L, SSD_XBC_W), 0.01)
    ssd_a_log = jnp.log(unif((L, 2, SSD_HEADS), 1.0, 16.0))
    dt0 = jnp.exp(unif((L, 2, SSD_HEADS), math.log(1e-3), math.log(1e-1)))
    ssd_dt_bias = dt0 + jnp.log(-jnp.expm1(-dt0))
    ssd_d = 1.0 + nrm((L, SSD_HEADS), 0.02)
    ssd_norm = 1.0 + nrm((L, SSD_INNER), 0.02)
    s5_lam_re = -0.5 * jnp.exp(nrm((L, 2, S5_GROUPS, S5_N), 0.02))
    s5_lam_im = math.pi * jnp.arange(S5_N, dtype=f32) + nrm((L, 2, S5_GROUPS, S5_N), 0.01)
    s5_log_step = unif((L, 2, S5_GROUPS), math.log(1e-3), math.log(1e-1))
    s5_b_re = nrm((L, 2, S5_GROUPS, S5_N, S5_GROUP_CH), (2 * S5_GROUP_CH) ** -0.5)
    s5_b_im = nrm((L, 2, S5_GROUPS, S5_N, S5_GROUP_CH), (2 * S5_GROUP_CH) ** -0.5)
    s5_c_re = nrm((L, 2, S5_GROUPS, S5_GROUP_CH, S5_N), S5_N ** -0.5)
    s5_c_im = nrm((L, 2, S5_GROUPS, S5_GROUP_CH, S5_N), S5_N ** -0.5)
    s5_d = nrm((L, S5_CH), 1.0)
    s5_glu_w = nrm((L, S5_CH, S5_CH), S5_CH ** -0.5)
    s5_glu_b = nrm((L, S5_CH), 0.01)
    w_branch = nrm((L, N_BRANCH, BRANCH_W, D), DN_BETA * BRANCH_W ** -0.5)
    w_out = nrm((L, D, D), DN_BETA * D ** -0.5)
    return {"x": x, "c": c, "ctx": ctx, "c_ctx": c_ctx, "w_mod": w_mod, "b_mod": b_mod,
            "ln_g": ln_g, "ln_b": ln_b, "ffn_w1": ffn_w1, "ffn_w3": ffn_w3, "ffn_w2": ffn_w2,
            "w_in": w_in, "att_lam": att_lam, "att_subln": att_subln,
            "ssd_conv_w": ssd_conv_w, "ssd_conv_b": ssd_conv_b, "ssd_a_log": ssd_a_log,
            "ssd_dt_bias": ssd_dt_bias, "ssd_d": ssd_d, "ssd_norm": ssd_norm,
            "s5_lam_re": s5_lam_re, "s5_lam_im": s5_lam_im, "s5_log_step": s5_log_step,
            "s5_b_re": s5_b_re, "s5_b_im": s5_b_im, "s5_c_re": s5_c_re, "s5_c_im": s5_c_im,
            "s5_d": s5_d, "s5_glu_w": s5_glu_w, "s5_glu_b": s5_glu_b,
            "w_branch": w_branch, "w_out": w_out}


def reference(x, c, ctx, c_ctx, w_mod, b_mod, ln_g, ln_b, ffn_w1, ffn_w3, ffn_w2, w_in,
              att_lam, att_subln, ssd_conv_w, ssd_conv_b, ssd_a_log, ssd_dt_bias, ssd_d, ssd_norm,
              s5_lam_re, s5_lam_im, s5_log_step, s5_b_re, s5_b_im, s5_c_re, s5_c_im,
              s5_d, s5_glu_w, s5_glu_b, w_branch, w_out):
    h, hc = x, ctx
    for i in range(DEPTH):
        mod = (jax.nn.silu(c) @ w_mod[i] + b_mod[i]).reshape(c.shape[0], 1, N_MOD, D_MODEL)
        mod_c = (jax.nn.silu(c_ctx) @ w_mod[i] + b_mod[i]).reshape(N_MOD, D_MODEL)
        lam_init = LAMBDA_INIT_BASE - LAMBDA_INIT_SPAN * math.exp(-LAMBDA_INIT_RATE * i)
        h = half_ffn(h, mod, 0, ffn_w1[i, 0], ffn_w3[i, 0], ffn_w2[i, 0], ln_g[i, 0], ln_b[i, 0])
        hc = half_ffn(hc, mod_c, 0, ffn_w1[i, 0], ffn_w3[i, 0], ffn_w2[i, 0], ln_g[i, 0], ln_b[i, 0])
        sh, sc, gt = adaln(mod, 1)
        shc, scc, gtc = adaln(mod_c, 1)
        y, yc = token_mixer(modulate(h, sh, sc), modulate(hc, shc, scc), w_in[i],
                            att_lam[i], att_subln[i], lam_init,
                            ssd_conv_w[i], ssd_conv_b[i], ssd_a_log[i], ssd_dt_bias[i], ssd_d[i], ssd_norm[i],
                            s5_lam_re[i], s5_lam_im[i], s5_log_step[i], s5_b_re[i], s5_b_im[i],
                            s5_c_re[i], s5_c_im[i], s5_d[i], s5_glu_w[i], s5_glu_b[i],
                            w_branch[i], w_out[i])
        h = post_norm(h, gt * y, ln_g[i, 1], ln_b[i, 1])
        h = half_ffn(h, mod, 2, ffn_w1[i, 1], ffn_w3[i, 1], ffn_w2[i, 1], ln_g[i, 2], ln_b[i, 2])
        if i + 1 < DEPTH:
            hc = post_norm(hc, gtc * yc, ln_g[i, 1], ln_b[i, 1])
            hc = half_ffn(hc, mod_c, 2, ffn_w1[i, 1], ffn_w3[i, 1], ffn_w2[i, 1], ln_g[i, 2], ln_b[i, 2])
    return h
```

```python
import functools
import math

import jax
import jax.numpy as jnp
from jax import lax
from jax.experimental import pallas as pl
from jax.experimental.pallas import tpu as pltpu

F32 = jnp.float32
BF16 = jnp.bfloat16

D_MODEL = 2048
DEPTH = 2
GRID_W = 64
DN_ALPHA = (2 * DEPTH) ** 0.25
N_SUB = 3
N_MOD = 3 * N_SUB
FFN_HALF = 0.5
D_FF = 5632
LN_EPS = 1e-5
RMS_EPS = 1e-6
BRANCH_W = D_MODEL // 2
N_BRANCH = 3
ATT_DH = 64
ATT_DV = 2 * ATT_DH
ATT_HEADS = BRANCH_W // ATT_DV
ROPE_BASE = 10000.0
ROPE_FREQS = ATT_DH // 4
LAMBDA_INIT_BASE = 0.8
LAMBDA_INIT_SPAN = 0.6
LAMBDA_INIT_RATE = 0.3
SSD_P = 64
SSD_HEADS = BRANCH_W // SSD_P
SSD_GROUPS = 4
SSD_HPG = SSD_HEADS // SSD_GROUPS
SSD_N = 128
SSD_CONV = 5
SSD_INNER = SSD_HEADS * SSD_P
SSD_BC_W = SSD_GROUPS * SSD_N
SSD_XBC_W = SSD_INNER + 2 * SSD_BC_W
SSD_NORM_GROUP = SSD_INNER // SSD_GROUPS
S5_CH = BRANCH_W
S5_GROUP_CH = 16
S5_GROUPS = S5_CH // S5_GROUP_CH
S5_N = 64

LANES = 128
SUBLANES = 8
VMEM_LIMIT_BYTES = 56 * 1024 * 1024
MOD_ROWS = 8
TM = 512
TN_FF = 512
TN_PROJ = 512
TN_MOD = 1024
ATT_TQ = 256
ATT_TK = 512
SSD_L = 128
SSD_TM = 256
S5_L = 16
S5_CW = S5_L * S5_GROUP_CH
DT_PAD = LANES


def _params(*sem):
    return pltpu.CompilerParams(dimension_semantics=sem, vmem_limit_bytes=VMEM_LIMIT_BYTES)


class Layout:
    def __init__(self, batch, seq, ctx):
        self.batch, self.seq, self.ctx = batch, seq, ctx
        self.n_lat = batch * seq
        self.n_tok = batch * (seq + ctx)
        for t in (TM, SSD_TM, ATT_TQ, SSD_L):
            assert seq % t == 0 and (batch * ctx) % t == 0, (seq, ctx, t)
        assert ctx == SSD_TM and ctx % SSD_L == 0 and seq % ATT_TK == 0
        assert batch * 2 == SUBLANES and seq % GRID_W == 0

    def sample_of_tile(self, i, tile):
        return jnp.where(i < self.n_lat // tile, i // (self.seq // tile), self.batch)


def _post_norm_emit(h, upd, lng, lnb, nsh, nsc, ho_ref, hmo_ref):
    t = DN_ALPHA * h + upd
    mu = jnp.mean(t, axis=-1, keepdims=True)
    tc = t - mu
    var = jnp.mean(tc * tc, axis=-1, keepdims=True)
    hn = tc * lax.rsqrt(var + LN_EPS) * lng + lnb
    ho_ref[...] = hn
    hmo_ref[...] = (hn * (1.0 + nsc) + nsh).astype(hmo_ref.dtype)


def _mod_kernel(c_ref, w_ref, b_ref, o_ref):
    c = c_ref[...]
    s = (c * jax.nn.sigmoid(c)).astype(BF16)
    o_ref[...] = jnp.dot(s, w_ref[...].astype(BF16), preferred_element_type=F32) + b_ref[...]


def _mod_all(cc, w_mod, b_mod):
    depth, d, n = w_mod.shape
    return pl.pallas_call(
        _mod_kernel,
        grid=(depth, n // TN_MOD),
        in_specs=[pl.BlockSpec((MOD_ROWS, d), lambda l, j: (0, 0)),
                  pl.BlockSpec((None, d, TN_MOD), lambda l, j: (l, 0, j)),
                  pl.BlockSpec((None, 1, TN_MOD), lambda l, j: (l, 0, j))],
        out_specs=pl.BlockSpec((None, MOD_ROWS, TN_MOD), lambda l, j: (l, 0, j)),
        out_shape=jax.ShapeDtypeStruct((depth, MOD_ROWS, n), F32),
        compiler_params=_params("parallel", "parallel"),
        name="mod_matmul",
    )(cc, w_mod, b_mod.reshape(depth, 1, n))


def _modulate_kernel(h_ref, sh_ref, sc_ref, o_ref):
    o_ref[...] = (h_ref[...] * (1.0 + sc_ref[...]) + sh_ref[...]).astype(o_ref.dtype)


def _modulate(lay, h, sh, sc):
    vec = pl.BlockSpec((None, 1, D_MODEL), lambda i: (lay.sample_of_tile(i, TM), 0, 0))
    row = pl.BlockSpec((TM, D_MODEL), lambda i: (i, 0))
    return pl.pallas_call(
        _modulate_kernel, grid=(lay.n_tok // TM,),
        in_specs=[row, vec, vec], out_specs=row,
        out_shape=jax.ShapeDtypeStruct((lay.n_tok, D_MODEL), BF16),
        compiler_params=_params("parallel"), name="modulate",
    )(h, sh, sc)


def _ffn_kernel(hm_ref, h_ref, w1_ref, w3_ref, w2_ref, gate_ref, lng_ref, lnb_ref, nsh_ref, nsc_ref,
                ho_ref, hmo_ref, acc_ref):
    j = pl.program_id(1)
    hm = hm_ref[...]
    a = jnp.dot(hm, w1_ref[...], preferred_element_type=F32)
    b = jnp.dot(hm, w3_ref[...], preferred_element_type=F32)
    p = (a * jax.nn.sigmoid(a) * b).astype(BF16)
    contrib = jnp.dot(p, w2_ref[...], preferred_element_type=F32)

    @pl.when(j == 0)
    def _():
        acc_ref[...] = contrib

    @pl.when(j > 0)
    def _():
        acc_ref[...] += contrib

    @pl.when(j == pl.num_programs(1) - 1)
    def _():
        upd = (FFN_HALF * gate_ref[...]) * acc_ref[...]
        _post_norm_emit(h_ref[...], upd, lng_ref[...], lnb_ref[...], nsh_ref[...], nsc_ref[...],
                        ho_ref, hmo_ref)


def _half_ffn(lay, hm, h, w1, w3, w2, gate, lng, lnb, nsh, nsc):
    vec = pl.BlockSpec((None, 1, D_MODEL), lambda i, j: (lay.sample_of_tile(i, TM), 0, 0))
    par = pl.BlockSpec((1, D_MODEL), lambda i, j: (0, 0))
    row = pl.BlockSpec((TM, D_MODEL), lambda i, j: (i, 0))
    return pl.pallas_call(
        _ffn_kernel,
        grid=(lay.n_tok // TM, D_FF // TN_FF),
        in_specs=[row, row,
                  pl.BlockSpec((D_MODEL, TN_FF), lambda i, j: (0, j)),
                  pl.BlockSpec((D_MODEL, TN_FF), lambda i, j: (0, j)),
                  pl.BlockSpec((TN_FF, D_MODEL), lambda i, j: (j, 0)),
                  vec, par, par, vec, vec],
        out_specs=[row, row],
        out_shape=[jax.ShapeDtypeStruct((lay.n_tok, D_MODEL), F32),
                   jax.ShapeDtypeStruct((lay.n_tok, D_MODEL), BF16)],
        scratch_shapes=[pltpu.VMEM((TM, D_MODEL), F32)],
        compiler_params=_params("parallel", "arbitrary"), name="half_ffn",
    )(hm, h, w1, w3, w2, gate, lng, lnb, nsh, nsc)


def _proj_kernel(x_ref, w_ref, o_ref):
    o_ref[...] = jnp.dot(x_ref[...], w_ref[...], preferred_element_type=F32).astype(o_ref.dtype)


def _proj(lay, hm, w, out_dtype, name):
    n = w.shape[1]
    tn = min(TN_PROJ, n)
    return pl.pallas_call(
        _proj_kernel, grid=(lay.n_tok // TM, n // tn),
        in_specs=[pl.BlockSpec((TM, D_MODEL), lambda i, j: (i, 0)),
                  pl.BlockSpec((D_MODEL, tn), lambda i, j: (0, j))],
        out_specs=pl.BlockSpec((TM, tn), lambda i, j: (i, j)),
        out_shape=jax.ShapeDtypeStruct((lay.n_tok, n), out_dtype),
        compiler_params=_params("parallel", "parallel"), name=name,
    )(hm, w)


def _proj_qk_kernel(x_ref, w_ref, cos_ref, sin_ref, o_ref, *, n_lat_tiles, n_q_tiles):
    i, j = pl.program_id(0), pl.program_id(1)
    acc = jnp.dot(x_ref[...], w_ref[...], preferred_element_type=F32)
    scale = jnp.where(j < n_q_tiles, ATT_DH ** -0.5, 1.0).astype(F32)

    @pl.when(i < n_lat_tiles)
    def _():
        cos, sin = cos_ref[...], sin_ref[...]
        lane = lax.broadcasted_iota(jnp.int32, cos.shape, 1)
        first = (lane % (2 * ROPE_FREQS)) < ROPE_FREQS
        for k in range(acc.shape[1] // LANES):
            t = acc[:, k * LANES:(k + 1) * LANES]
            partner = jnp.where(first, pltpu.roll(t, LANES - ROPE_FREQS, 1), pltpu.roll(t, ROPE_FREQS, 1))
            o_ref[:, k * LANES:(k + 1) * LANES] = ((t * cos + partner * sin) * scale).astype(o_ref.dtype)

    @pl.when(i >= n_lat_tiles)
    def _():
        o_ref[...] = (acc * scale).astype(o_ref.dtype)


def _rope_tables(seq):
    rows = seq // GRID_W
    row = jnp.repeat(jnp.arange(rows), GRID_W)
    col = jnp.tile(jnp.arange(GRID_W), rows)
    inv = ROPE_BASE ** (-jnp.arange(ROPE_FREQS, dtype=F32) / ROPE_FREQS)
    ar, ac = row[:, None] * inv, col[:, None] * inv
    cos = jnp.concatenate([jnp.cos(ar), jnp.cos(ar), jnp.cos(ac), jnp.cos(ac)], axis=1)
    sin = jnp.concatenate([-jnp.sin(ar), jnp.sin(ar), -jnp.sin(ac), jnp.sin(ac)], axis=1)
    return jnp.tile(cos, (1, 2)), jnp.tile(sin, (1, 2))


def _proj_qk(lay, hm, w, cos, sin):
    n = w.shape[1]
    tps = lay.seq // TM
    kern = functools.partial(_proj_qk_kernel, n_lat_tiles=lay.n_lat // TM, n_q_tiles=(n // 2) // TN_PROJ)
    tab = pl.BlockSpec((TM, LANES), lambda i, j: (i % tps, 0))
    return pl.pallas_call(
        kern, grid=(lay.n_tok // TM, n // TN_PROJ),
        in_specs=[pl.BlockSpec((TM, D_MODEL), lambda i, j: (i, 0)),
                  pl.BlockSpec((D_MODEL, TN_PROJ), lambda i, j: (0, j)), tab, tab],
        out_specs=pl.BlockSpec((TM, TN_PROJ), lambda i, j: (i, j)),
        out_shape=jax.ShapeDtypeStruct((lay.n_tok, n), BF16),
        compiler_params=_params("parallel", "parallel"), name="proj_qk",
    )(hm, w, cos, sin)


def _attn_kernel(lamv_ref, subln_ref, q_ref, kc_ref, vc_ref, *rest, n_lat_chunks, lam_init):
    if n_lat_chunks:
        kl_ref, vl_ref, o_ref = rest
    else:
        _, o_ref = rest
    lv = lamv_ref[...]
    lam = (jnp.exp(jnp.sum(lv[0:1] * lv[1:2], axis=-1, keepdims=True))
           - jnp.exp(jnp.sum(lv[2:3] * lv[3:4], axis=-1, keepdims=True)) + lam_init)
    q = q_ref[...]
    lane = lax.broadcasted_iota(jnp.int32, q.shape, 1)
    zero = jnp.zeros_like(q)
    qs = (jnp.where(lane < ATT_DH, q, zero), jnp.where(lane >= ATT_DH, q, zero))

    def scores(qm, k):
        return lax.dot_general(qm, k, (((1,), (1,)), ((), ())), preferred_element_type=F32)

    kc, vc = kc_ref[...], vc_ref[...]
    carry = []
    for qm in qs:
        s = scores(qm, kc)
        m = jnp.max(s, axis=-1, keepdims=True)
        e = jnp.exp(s - m)
        carry += [m, jnp.sum(e, axis=-1, keepdims=True),
                  jnp.dot(e.astype(BF16), vc, preferred_element_type=F32)]

    if n_lat_chunks:
        def body(c, carry):
            off = pl.multiple_of(c * ATT_TK, ATT_TK)
            k = kl_ref[pl.ds(off, ATT_TK), :]
            v = vl_ref[pl.ds(off, ATT_TK), :]
            out = []
            for jm, qm in enumerate(qs):
                m, l, acc = carry[3 * jm:3 * jm + 3]
                s = scores(qm, k)
                m_new = jnp.maximum(m, jnp.max(s, axis=-1, keepdims=True))
                alpha = jnp.exp(m - m_new)
                e = jnp.exp(s - m_new)
                out += [m_new, alpha * l + jnp.sum(e, axis=-1, keepdims=True),
                        alpha * acc + jnp.dot(e.astype(BF16), v, preferred_element_type=F32)]
            return tuple(out)

        carry = lax.fori_loop(0, n_lat_chunks, body, tuple(carry))

    _, l0, a0, _, l1, a1 = carry
    o = a0 / l0 - lam * (a1 / l1)
    o = o * lax.rsqrt(jnp.mean(o * o, axis=-1, keepdims=True) + RMS_EPS) * subln_ref[...] * (1.0 - lam_init)
    o_ref[...] = o.astype(o_ref.dtype)


def _diff_attention(lay, qk, v, att_lam, subln, lam_init):
    b_, s_, c_ = lay.batch, lay.seq, lay.ctx
    kcol = BRANCH_W // LANES
    lamv = pl.BlockSpec((4, ATT_DH), lambda *_: (0, 0))
    sub = pl.BlockSpec((1, ATT_DV), lambda *_: (0, 0))
    out_shape = jax.ShapeDtypeStruct((lay.n_tok, BRANCH_W), BF16)
    ctx_blk = lay.n_lat // c_
    tq = s_ // ATT_TQ
    lat = pl.pallas_call(
        functools.partial(_attn_kernel, n_lat_chunks=s_ // ATT_TK, lam_init=lam_init),
        grid=(b_, ATT_HEADS, tq),
        in_specs=[lamv, sub,
                  pl.BlockSpec((ATT_TQ, LANES), lambda b, h, i: (b * tq + i, h)),
                  pl.BlockSpec((c_, LANES), lambda b, h, i: (ctx_blk + b, kcol + h)),
                  pl.BlockSpec((c_, LANES), lambda b, h, i: (ctx_blk + b, h)),
                  pl.BlockSpec((s_, LANES), lambda b, h, i: (b, kcol + h)),
                  pl.BlockSpec((s_, LANES), lambda b, h, i: (b, h))],
        out_specs=pl.BlockSpec((ATT_TQ, LANES), lambda b, h, i: (b * tq + i, h)),
        out_shape=out_shape, input_output_aliases={},
        compiler_params=_params("parallel", "parallel", "arbitrary"), name="attn_latent",
    )(att_lam, subln, qk, qk, v, qk, v)
    tqc = c_ // ATT_TQ
    out = pl.pallas_call(
        functools.partial(_attn_kernel, n_lat_chunks=0, lam_init=lam_init),
        grid=(b_, ATT_HEADS, tqc),
        in_specs=[lamv, sub,
                  pl.BlockSpec((ATT_TQ, LANES), lambda b, h, i: (lay.n_lat // ATT_TQ + b * tqc + i, h)),
                  pl.BlockSpec((c_, LANES), lambda b, h, i: (ctx_blk + b, kcol + h)),
                  pl.BlockSpec((c_, LANES), lambda b, h, i: (ctx_blk + b, h)),
                  pl.BlockSpec(memory_space=pl.ANY)],
        out_specs=pl.BlockSpec((ATT_TQ, LANES), lambda b, h, i: (lay.n_lat // ATT_TQ + b * tqc + i, h)),
        out_shape=out_shape, input_output_aliases={5: 0},
        compiler_params=_params("parallel", "parallel", "arbitrary"), name="attn_context",
    )(att_lam, subln, qk, qk, v, lat)
    return out


def _ssd_prep_kernel(prev_ref, cur_ref, next_ref, dt_ref, cw_ref, cb_ref, dtb_ref,
                     xs_ref, bm_ref, cm_ref, dts_ref, dtst_ref, ext_ref, *, tiles_per_seq, n_lat_tiles):
    i = pl.program_id(0)
    is_ctx = i >= n_lat_tiles
    first = jnp.logical_or(is_ctx, i % tiles_per_seq == 0)
    last = jnp.logical_or(is_ctx, i % tiles_per_seq == tiles_per_seq - 1)
    tm = cur_ref.shape[0]
    ext_ref[0:SUBLANES, :] = jnp.where(first, 0.0, prev_ref[...])
    ext_ref[SUBLANES:SUBLANES + tm, :] = cur_ref[...]
    ext_ref[SUBLANES + tm:, :] = jnp.where(last, 0.0, next_ref[...])
    acc = jnp.zeros(cur_ref.shape, F32) + cb_ref[...]
    for k in range(SSD_CONV):
        start = SUBLANES + k - SSD_CONV // 2
        acc = acc + ext_ref[start:start + tm, :] * cw_ref[k:k + 1, :]
    act = acc * jax.nn.sigmoid(acc)
    xs_ref[...] = act[:, :SSD_INNER]
    bm_ref[...] = act[:, SSD_INNER:SSD_INNER + SSD_BC_W].astype(bm_ref.dtype)
    cm_ref[...] = act[:, SSD_INNER + SSD_BC_W:].astype(cm_ref.dtype)
    x = dt_ref[...] + dtb_ref[...]
    sp = jnp.maximum(x, 0.0) + jnp.log1p(jnp.exp(-jnp.abs(x)))
    dts_ref[0] = sp
    dts_ref[1] = pltpu.roll(sp, DT_PAD - SSD_HEADS, 1)
    spt = sp.T
    dtst_ref[0] = spt[0:SSD_HEADS]
    dtst_ref[1] = spt[SSD_HEADS:2 * SSD_HEADS]


def _ssd_prep(lay, xbc, dt, conv_w, conv_b, dt_bias):
    n, tm = lay.n_tok, SSD_TM
    sub_per_tile = tm // SUBLANES
    n_sub = n // SUBLANES
    kern = functools.partial(_ssd_prep_kernel, tiles_per_seq=lay.seq // tm, n_lat_tiles=lay.n_lat // tm)
    row = lambda w: pl.BlockSpec((tm, w), lambda i: (i, 0))
    return pl.pallas_call(
        kern, grid=(n // tm,),
        in_specs=[pl.BlockSpec((SUBLANES, SSD_XBC_W), lambda i: (jnp.maximum(i * sub_per_tile - 1, 0), 0)),
                  row(SSD_XBC_W),
                  pl.BlockSpec((SUBLANES, SSD_XBC_W),
                               lambda i: (jnp.minimum((i + 1) * sub_per_tile, n_sub - 1), 0)),
                  row(DT_PAD),
                  pl.BlockSpec((SUBLANES, SSD_XBC_W), lambda i: (0, 0)),
                  pl.BlockSpec((1, SSD_XBC_W), lambda i: (0, 0)),
                  pl.BlockSpec((1, DT_PAD), lambda i: (0, 0))],
        out_specs=[row(SSD_INNER), row(SSD_BC_W), row(SSD_BC_W),
                   pl.BlockSpec((2, tm, DT_PAD), lambda i: (0, i, 0)),
                   pl.BlockSpec((2, SSD_HEADS, tm), lambda i: (0, 0, i))],
        out_shape=[jax.ShapeDtypeStruct((n, SSD_INNER), F32),
                   jax.ShapeDtypeStruct((n, SSD_BC_W), BF16),
                   jax.ShapeDtypeStruct((n, SSD_BC_W), BF16),
                   jax.ShapeDtypeStruct((2, n, DT_PAD), F32),
                   jax.ShapeDtypeStruct((2, SSD_HEADS, n), F32)],
        scratch_shapes=[pltpu.VMEM((tm + 2 * SUBLANES, SSD_XBC_W), F32)],
        compiler_params=_params("parallel"), name="ssd_prep",
    )(xbc, xbc, xbc, dt, conv_w, conv_b, dt_bias)


def _split_dot(a, x, x_is_lhs=False):
    out = None
    r = x
    for _ in range(3):
        t = r.astype(BF16)
        r = r - t.astype(F32)
        d = (jnp.dot(t, a, preferred_element_type=F32) if x_is_lhs
             else jnp.dot(a, t, preferred_element_type=F32))
        out = d if out is None else out + d
    return out


def _ssd_kernel(xs_ref, bm_ref, cm_ref, dts_ref, dtst_ref, arow_ref, acol_ref, y_ref, h_ref):
    sign = 1 - 2 * pl.program_id(1)

    @pl.when(pl.program_id(2) == 0)
    def _():
        h_ref[...] = jnp.zeros(h_ref.shape, F32)

    ll = xs_ref.shape[0]
    r = lax.broadcasted_iota(jnp.int32, (ll, ll), 0)
    c = lax.broadcasted_iota(jnp.int32, (ll, ll), 1)
    mask = (r - c) * sign >= 0
    mask_t = (r - c) * sign <= 0
    one_hot = lambda m: jnp.where(m, 1.0, 0.0).astype(BF16)
    dts = dts_ref[...]
    da = dts * arow_ref[...]
    da_t = dtst_ref[...] * acol_ref[...]
    cs = _split_dot(one_hot(mask), da)
    cs_t = _split_dot(one_hot(mask_t), da_t, x_is_lhs=True)
    tot = jnp.sum(da, axis=0, keepdims=True)
    to_end = jnp.exp(tot - cs)
    from_start = jnp.exp(cs)
    chunk_decay = jnp.exp(tot)
    lane = lax.broadcasted_iota(jnp.int32, (ll, LANES), 1)
    lo_half = lane < SSD_P

    def pair_expand(t, hd):
        rows = t.shape[0]
        return jnp.where(lo_half[:rows], jnp.broadcast_to(t[:, hd:hd + 1], (rows, LANES)),
                         jnp.broadcast_to(t[:, hd + 1:hd + 2], (rows, LANES)))

    gw = SSD_HPG * SSD_P
    for g in range(SSD_GROUPS):
        bg = bm_ref[:, g * SSD_N:(g + 1) * SSD_N]
        cg = cm_ref[:, g * SSD_N:(g + 1) * SSD_N]
        cb = lax.dot_general(cg, bg, (((1,), (1,)), ((), ())), preferred_element_type=F32)
        y_parts, xw_parts, fs_parts, cd_parts = [], [], [], []
        for pr in range(SSD_HPG // 2):
            hd = g * SSD_HPG + 2 * pr
            xdt = xs_ref[:, hd * SSD_P:(hd + 2) * SSD_P] * pair_expand(dts, hd)
            xw_parts.append((xdt * pair_expand(to_end, hd)).astype(BF16))
            fs_parts.append(pair_expand(from_start, hd))
            cd_parts.append(pair_expand(chunk_decay, hd))
            yp = None
            for k in range(2):
                seg = cs[:, hd + k:hd + k + 1] - cs_t[hd + k:hd + k + 1, :]
                dec = jnp.where(mask, jnp.exp(jnp.where(mask, seg, 0.0)), 0.0)
                w = (cb * dec).astype(BF16)
                xk = jnp.where(lo_half if k == 0 else jnp.logical_not(lo_half), xdt, 0.0).astype(BF16)
                d = jnp.dot(w, xk, preferred_element_type=F32)
                yp = d if yp is None else yp + d
            y_parts.append(yp)
        xw = jnp.concatenate(xw_parts, axis=1)
        states = lax.dot_general(bg, xw, (((0,), (0,)), ((), ())), preferred_element_type=F32)
        h_prev = h_ref[g]
        y_off = jnp.dot(cg, h_prev.astype(BF16), preferred_element_type=F32) * jnp.concatenate(fs_parts, axis=1)
        y_ref[:, g * gw:(g + 1) * gw] = jnp.concatenate(y_parts, axis=1) + y_off
        h_ref[g] = h_prev * jnp.concatenate(cd_parts, axis=1) + states


def _ssd_scan(lay, xs, bm, cm, dts, dtst, a_row, a_col):
    ll = SSD_L
    n_ctx, n_lat = lay.ctx // ll, lay.seq // ll
    ctx_base = lay.n_lat // ll

    def rb(b, d, s):
        cstep = jnp.where(d == 0, s, n_ctx - 1 - s)
        lstep = jnp.where(d == 0, s - n_ctx, n_lat - 1 - (s - n_ctx))
        return jnp.where(s < n_ctx, ctx_base + b * n_ctx + cstep, b * n_lat + lstep)

    row = lambda w: pl.BlockSpec((ll, w), lambda b, d, s: (rb(b, d, s), 0))
    return pl.pallas_call(
        _ssd_kernel, grid=(lay.batch, 2, n_ctx + n_lat),
        in_specs=[row(SSD_INNER), row(SSD_BC_W), row(SSD_BC_W),
                  pl.BlockSpec((None, ll, DT_PAD), lambda b, d, s: (d, rb(b, d, s), 0)),
                  pl.BlockSpec((None, SSD_HEADS, ll), lambda b, d, s: (d, 0, rb(b, d, s))),
                  pl.BlockSpec((None, 1, DT_PAD), lambda b, d, s: (d, 0, 0)),
                  pl.BlockSpec((None, SSD_HEADS, 1), lambda b, d, s: (d, 0, 0))],
        out_specs=pl.BlockSpec((None, ll, SSD_INNER), lambda b, d, s: (d, rb(b, d, s), 0)),
        out_shape=jax.ShapeDtypeStruct((2, lay.n_tok, SSD_INNER), F32),
        scratch_shapes=[pltpu.VMEM((SSD_GROUPS, SSD_N, SSD_HPG * SSD_P), F32)],
        compiler_params=_params("parallel", "parallel", "arbitrary"), name="ssd_scan",
    )(xs, bm, cm, dts, dtst, a_row, a_col)


def _ssd_gate_norm_kernel(y0_ref, y1_ref, xs_ref, z_ref, dsk_ref, nw_ref, o_ref):
    z = z_ref[...]
    gated = (y0_ref[...] + y1_ref[...] + xs_ref[...] * dsk_ref[...]) * (z * jax.nn.sigmoid(z))
    nw = nw_ref[...]
    for g in range(SSD_GROUPS):
        sl = slice(g * SSD_NORM_GROUP, (g + 1) * SSD_NORM_GROUP)
        t = gated[:, sl]
        o_ref[:, sl] = (t * lax.rsqrt(jnp.mean(t * t, axis=-1, keepdims=True) + RMS_EPS)
                        * nw[:, sl]).astype(o_ref.dtype)


def _ssd_gate_norm(lay, ydir, xs, z, dsk, nw):
    tm = SSD_TM
    row = pl.BlockSpec((tm, SSD_INNER), lambda i: (i, 0))
    par = pl.BlockSpec((1, SSD_INNER), lambda i: (0, 0))
    return pl.pallas_call(
        _ssd_gate_norm_kernel, grid=(lay.n_tok // tm,),
        in_specs=[pl.BlockSpec((None, tm, SSD_INNER), lambda i: (0, i, 0)),
                  pl.BlockSpec((None, tm, SSD_INNER), lambda i: (1, i, 0)), row, row, par, par],
        out_specs=row, out_shape=jax.ShapeDtypeStruct((lay.n_tok, SSD_INNER), BF16),
        compiler_params=_params("parallel"), name="ssd_gate_norm",
    )(ydir, ydir, xs, z, dsk, nw)


def _s5_operators(lam_re, lam_im, log_step, b_re, b_im, c_re, c_im):
    hp = lax.Precision.HIGHEST
    ll, hh = S5_L, S5_GROUP_CH
    step = jnp.exp(log_step)[..., None, None]
    d = jnp.arange(ll + 1, dtype=F32)
    p_mag = jnp.exp(lam_re[..., None] * step * d)
    p_ang = lam_im[..., None] * step * d
    p_re, p_im = p_mag * jnp.cos(p_ang), p_mag * jnp.sin(p_ang)
    ab_re, ab_im = p_re[..., 1], p_im[..., 1]
    den = lam_re * lam_re + lam_im * lam_im
    k_re = ((ab_re - 1.0) * lam_re + ab_im * lam_im) / den
    k_im = (ab_im * lam_re - (ab_re - 1.0) * lam_im) / den
    bb_re = k_re[..., None] * b_re - k_im[..., None] * b_im
    bb_im = k_re[..., None] * b_im + k_im[..., None] * b_re
    cp_re = c_re[..., None] * p_re[:, :, None] - c_im[..., None] * p_im[:, :, None]
    cp_im = c_re[..., None] * p_im[:, :, None] + c_im[..., None] * p_re[:, :, None]
    kern = (jnp.einsum('zghnd,zgnk->zgdhk', cp_re, bb_re, precision=hp)
            - jnp.einsum('zghnd,zgnk->zgdhk', cp_im, bb_im, precision=hp))
    s_idx = jnp.arange(ll)[:, None]
    l_idx = jnp.arange(ll)[None, :]
    t_f = jnp.where((l_idx >= s_idx)[None, :, :, None, None], kern[0][:, jnp.clip(l_idx - s_idx, 0, ll)], 0.0)
    t_b = jnp.where((s_idx >= l_idx)[None, :, :, None, None], kern[1][:, jnp.clip(s_idx - l_idx, 0, ll)], 0.0)
    toep = (t_f + t_b).transpose(0, 1, 4, 2, 3).reshape(S5_GROUPS, S5_CW, S5_CW)

    def state_in(z, powers):
        pr, pi = p_re[z][..., powers], p_im[z][..., powers]
        re = pr[..., None] * bb_re[z][:, :, None] - pi[..., None] * bb_im[z][:, :, None]
        im = pr[..., None] * bb_im[z][:, :, None] + pi[..., None] * bb_re[z][:, :, None]
        re = re.transpose(0, 2, 3, 1).reshape(S5_GROUPS, S5_CW, S5_N)
        im = im.transpose(0, 2, 3, 1).reshape(S5_GROUPS, S5_CW, S5_N)
        return jnp.concatenate([re, im], axis=-1)

    ws_f = state_in(0, ll - 1 - jnp.arange(ll))
    ws_b = state_in(1, jnp.arange(ll))
    w1 = jnp.concatenate([toep, ws_f, ws_b], axis=-1)

    def state_out(z, powers):
        re = cp_re[z][..., powers].transpose(0, 2, 3, 1).reshape(S5_GROUPS, S5_N, S5_CW)
        im = cp_im[z][..., powers].transpose(0, 2, 3, 1).reshape(S5_GROUPS, S5_N, S5_CW)
        return jnp.concatenate([re, -im], axis=1)

    wo = jnp.concatenate([state_out(0, jnp.arange(ll) + 1), state_out(1, ll - jnp.arange(ll))], axis=1)
    ar, ai = p_re[..., ll], p_im[..., ll]
    a1 = jnp.concatenate([ar, ar], axis=-1)
    a2 = jnp.concatenate([-ai, ai], axis=-1)
    zeros = jnp.zeros_like(a1[0])
    av = jnp.stack([a1[0], a2[0], a1[1], a2[1], zeros, zeros, zeros, zeros], axis=1)
    return w1.astype(BF16), wo.astype(BF16), av


def _s5_kernel(u_ref, w1_ref, wo_ref, av_ref, y_ref, sf_ref, sfs_ref, sb_ref, sbs_ref, hf_ref, hb_ref,
               *, n_ctx_tiles, n_tiles):
    p = jnp.dot(u_ref[...], w1_ref[...], preferred_element_type=F32)
    nst = 2 * S5_N
    sf = p[:, S5_CW:S5_CW + nst]
    sb = p[:, S5_CW + nst:]
    sf_ref[...] = sf
    sfs_ref[...] = pltpu.roll(sf, S5_N, 1)
    sb_ref[...] = sb
    sbs_ref[...] = pltpu.roll(sb, S5_N, 1)
    av = av_ref[...]
    half = SUBLANES // 2
    a1f, a2f = (jnp.broadcast_to(av[k:k + 1], (half, nst)) for k in (0, 1))
    a1b, a2b = (jnp.broadcast_to(av[k:k + 1], (half, nst)) for k in (2, 3))

    def body(j, carry):
        hf, hfs, hb, hbs = carry
        of = pl.multiple_of(j * SUBLANES, SUBLANES)
        s, ss = sf_ref[pl.ds(of, SUBLANES), :], sfs_ref[pl.ds(of, SUBLANES), :]
        h1 = a1f * hf + a2f * hfs + s[:half]
        h1s = a1f * hfs - a2f * hf + ss[:half]
        hf_ref[pl.ds(of, SUBLANES), :] = jnp.concatenate([hf, h1], axis=0)
        h2 = a1f * h1 + a2f * h1s + s[half:]
        h2s = a1f * h1s - a2f * h1 + ss[half:]
        jb = jnp.where(j < n_ctx_tiles, n_ctx_tiles - 1 - j, n_tiles - 1 - (j - n_ctx_tiles))
        ob = pl.multiple_of(jb * SUBLANES, SUBLANES)
        s, ss = sb_ref[pl.ds(ob, SUBLANES), :], sbs_ref[pl.ds(ob, SUBLANES), :]
        g1 = a1b * hb + a2b * hbs + s[half:]
        g1s = a1b * hbs - a2b * hb + ss[half:]
        hb_ref[pl.ds(ob, SUBLANES), :] = jnp.concatenate([g1, hb], axis=0)
        g2 = a1b * g1 + a2b * g1s + s[:half]
        g2s = a1b * g1s - a2b * g1 + ss[:half]
        return h2, h2s, g2, g2s

    z = jnp.zeros((half, nst), F32)
    lax.fori_loop(0, n_tiles, body, (z, z, z, z))
    wo = wo_ref[...]
    y_ref[...] = (p[:, :S5_CW]
                  + jnp.dot(hf_ref[...].astype(BF16), wo[:nst], preferred_element_type=F32)
                  + jnp.dot(hb_ref[...].astype(BF16), wo[nst:], preferred_element_type=F32))


def _s5_chunked(lay, uc, w1, wo, av):
    rows = lay.n_tok // S5_L
    n_ctx_rows = lay.batch * lay.ctx // S5_L
    kern = functools.partial(_s5_kernel, n_ctx_tiles=n_ctx_rows // SUBLANES, n_tiles=rows // SUBLANES)
    st = pltpu.VMEM((rows, 2 * S5_N), F32)
    return pl.pallas_call(
        kern, grid=(S5_GROUPS,),
        in_specs=[pl.BlockSpec((None, rows, S5_CW), lambda g: (g, 0, 0)),
                  pl.BlockSpec((None, S5_CW, S5_CW + 4 * S5_N), lambda g: (g, 0, 0)),
                  pl.BlockSpec((None, 4 * S5_N, S5_CW), lambda g: (g, 0, 0)),
                  pl.BlockSpec((None, SUBLANES, 2 * S5_N), lambda g: (g, 0, 0))],
        out_specs=pl.BlockSpec((None, rows, S5_CW), lambda g: (g, 0, 0)),
        out_shape=jax.ShapeDtypeStruct((S5_GROUPS, rows, S5_CW), F32),
        scratch_shapes=[st, st, st, st, st, st],
        compiler_params=_params("parallel"), name="s5_chunked",
    )(uc, w1, wo, av)


def _to_chunk_rows(lay, u):
    def part(t, length):
        t = t.reshape(lay.batch, length // S5_L, S5_L, S5_GROUPS, S5_GROUP_CH)
        return t.transpose(3, 1, 0, 2, 4).reshape(S5_GROUPS, (length // S5_L) * lay.batch, S5_CW)
    return jnp.concatenate([part(u[lay.n_lat:], lay.ctx), part(u[:lay.n_lat], lay.seq)], axis=1)


def _from_chunk_rows(lay, y):
    def part(t, length):
        t = t.reshape(S5_GROUPS, length // S5_L, lay.batch, S5_L, S5_GROUP_CH)
        return t.transpose(2, 1, 3, 0, 4).reshape(lay.batch * length, S5_CH)
    n_ctx_rows = lay.batch * lay.ctx // S5_L
    return jnp.concatenate([part(y[:, n_ctx_rows:], lay.seq), part(y[:, :n_ctx_rows], lay.ctx)], axis=0)


def _s5_glu_kernel(ys_ref, u_ref, dsk_ref, w_ref, b_ref, o_ref):
    t = ys_ref[...] + u_ref[...] * dsk_ref[...]
    t = 0.5 * t * (1.0 + jnp.tanh(math.sqrt(2.0 / math.pi) * (t + 0.044715 * (t * t * t))))
    gate = jnp.dot(t.astype(BF16), w_ref[...], preferred_element_type=F32) + b_ref[...]
    o_ref[...] = (t * jax.nn.sigmoid(gate)).astype(o_ref.dtype)


def _s5_glu(lay, ys, u, dsk, w, b):
    row = pl.BlockSpec((TM, S5_CH), lambda i: (i, 0))
    par = pl.BlockSpec((1, S5_CH), lambda i: (0, 0))
    return pl.pallas_call(
        _s5_glu_kernel, grid=(lay.n_tok // TM,),
        in_specs=[row, row, par, pl.BlockSpec((S5_CH, S5_CH), lambda i: (0, 0)), par],
        out_specs=row, out_shape=jax.ShapeDtypeStruct((lay.n_tok, S5_CH), BF16),
        compiler_params=_params("parallel"), name="s5_glu",
    )(ys, u, dsk, w, b)


def _merge_kernel(oa_ref, os_ref, o5_ref, ga_ref, gs_ref, g5_ref, wa_ref, ws_ref, w5_ref, o_ref):
    acc = None
    for o, g, w in ((oa_ref, ga_ref, wa_ref), (os_ref, gs_ref, ws_ref), (o5_ref, g5_ref, w5_ref)):
        t = jax.nn.sigmoid(g[...]) * jnp.dot(o[...], w[...], preferred_element_type=F32)
        acc = t if acc is None else acc + t
    o_ref[...] = acc.astype(o_ref.dtype)


def _merge(lay, o_att, o_ssd, o_s5, g, w_branch):
    nt = D_MODEL // TN_PROJ
    row = pl.BlockSpec((TM, BRANCH_W), lambda i, j: (i, 0))
    gate = lambda k: pl.BlockSpec((TM, TN_PROJ), lambda i, j: (i, k * nt + j))
    wb = lambda k: pl.BlockSpec((None, BRANCH_W, TN_PROJ), lambda i, j: (k, 0, j))
    return pl.pallas_call(
        _merge_kernel, grid=(lay.n_tok // TM, nt),
        in_specs=[row, row, row, gate(0), gate(1), gate(2), wb(0), wb(1), wb(2)],
        out_specs=pl.BlockSpec((TM, TN_PROJ), lambda i, j: (i, j)),
        out_shape=jax.ShapeDtypeStruct((lay.n_tok, D_MODEL), BF16),
        compiler_params=_params("parallel", "parallel"), name="branch_merge",
    )(o_att, o_ssd, o_s5, g, g, g, w_branch, w_branch, w_branch)


def _out_norm_kernel(mx_ref, h_ref, w_ref, gate_ref, lng_ref, lnb_ref, nsh_ref, nsc_ref, ho_ref, hmo_ref):
    y = jnp.dot(mx_ref[...], w_ref[...], preferred_element_type=F32)
    _post_norm_emit(h_ref[...], gate_ref[...] * y, lng_ref[...], lnb_ref[...], nsh_ref[...], nsc_ref[...],
                    ho_ref, hmo_ref)


def _out_norm(lay, mixed, h, w_out, gate, lng, lnb, nsh, nsc):
    vec = pl.BlockSpec((None, 1, D_MODEL), lambda i: (lay.sample_of_tile(i, TM), 0, 0))
    par = pl.BlockSpec((1, D_MODEL), lambda i: (0, 0))
    row = pl.BlockSpec((TM, D_MODEL), lambda i: (i, 0))
    return pl.pallas_call(
        _out_norm_kernel, grid=(lay.n_tok // TM,),
        in_specs=[row, row, pl.BlockSpec((D_MODEL, D_MODEL), lambda i: (0, 0)), vec, par, par, vec, vec],
        out_specs=[row, row],
        out_shape=[jax.ShapeDtypeStruct((lay.n_tok, D_MODEL), F32),
                   jax.ShapeDtypeStruct((lay.n_tok, D_MODEL), BF16)],
        compiler_params=_params("parallel"), name="out_norm",
    )(mixed, h, w_out, gate, lng, lnb, nsh, nsc)


def _pad_cols(t, width):
    return jnp.pad(t, [(0, 0)] * (t.ndim - 1) + [(0, width - t.shape[-1])])


def _token_mixer(lay, hm, rope, lam_init, w_in, att_lam, att_subln, conv_w, conv_b, a_log, dt_bias,
                 ssd_d, ssd_norm, s5_ops, s5_d, glu_w, glu_b, w_branch):
    cuts = [0]
    for w in (2 * BRANCH_W, BRANCH_W, SSD_INNER, SSD_XBC_W, 2 * SSD_HEADS, S5_CH, N_BRANCH * D_MODEL):
        cuts.append(cuts[-1] + w)
    w_qk, w_v, w_z, w_xbc, w_dt, w_u, w_g = (w_in[:, a:b].astype(BF16) for a, b in zip(cuts[:-1], cuts[1:]))
    qk = _proj_qk(lay, hm, w_qk, *rope)
    v = _proj(lay, hm, w_v, BF16, "proj_v")
    z = _proj(lay, hm, w_z, F32, "proj_z")
    xbc = _proj(lay, hm, w_xbc, F32, "proj_xbc")
    dt = _proj(lay, hm, _pad_cols(w_dt, DT_PAD), F32, "proj_dt")
    u = _proj(lay, hm, w_u, F32, "proj_u")
    g = _proj(lay, hm, w_g, F32, "proj_gate")

    o_att = _diff_attention(lay, qk, v, att_lam, att_subln.reshape(1, ATT_DV), lam_init)

    conv_w8 = jnp.pad(conv_w, ((0, SUBLANES - SSD_CONV), (0, 0)))
    xs, bm, cm, dts, dtst = _ssd_prep(lay, xbc, dt, conv_w8, conv_b.reshape(1, -1),
                                      _pad_cols(dt_bias.reshape(1, -1), DT_PAD))
    a = -jnp.exp(a_log.astype(F32))
    ydir = _ssd_scan(lay, xs, bm, cm, dts, dtst, _pad_cols(a, DT_PAD)[:, None, :], a[:, :, None])
    o_ssd = _ssd_gate_norm(lay, ydir, xs, z, jnp.repeat(ssd_d, SSD_P).reshape(1, -1),
                           ssd_norm.reshape(1, -1))

    ys = _s5_chunked(lay, _to_chunk_rows(lay, u.astype(BF16)), *s5_ops)
    o_s5 = _s5_glu(lay, _from_chunk_rows(lay, ys), u, s5_d.reshape(1, -1), glu_w.astype(BF16),
                   glu_b.reshape(1, -1))

    return _merge(lay, o_att, o_ssd, o_s5, g, w_branch.astype(BF16))


def _trunk(lay, x, c, ctx, c_ctx, w_mod, b_mod, ln_g, ln_b, ffn_w1, ffn_w3, ffn_w2, w_in,
           att_lam, att_subln, ssd_conv_w, ssd_conv_b, ssd_a_log, ssd_dt_bias, ssd_d, ssd_norm,
           s5_lam_re, s5_lam_im, s5_log_step, s5_b_re, s5_b_im, s5_c_re, s5_c_im,
           s5_d, s5_glu_w, s5_glu_b, w_branch, w_out):
    depth = w_mod.shape[0]
    h = jnp.concatenate([x.reshape(lay.n_lat, D_MODEL), ctx.reshape(-1, D_MODEL)], axis=0)
    cc = jnp.concatenate([c, c_ctx[None], jnp.zeros((MOD_ROWS - lay.batch - 1, D_MODEL), F32)], axis=0)
    mod = _mod_all(cc, w_mod, b_mod).reshape(depth, MOD_ROWS, N_MOD, 1, D_MODEL)
    mvec = lambda l, k: mod[l, :, k]
    zero_vec = jnp.zeros((MOD_ROWS, 1, D_MODEL), F32)
    rope = _rope_tables(lay.seq)
    lnp = lambda l, k: (ln_g[l, k].reshape(1, -1), ln_b[l, k].reshape(1, -1))

    hm = _modulate(lay, h, mvec(0, 0), mvec(0, 1))
    for l in range(depth):
        lam_init = LAMBDA_INIT_BASE - LAMBDA_INIT_SPAN * math.exp(-LAMBDA_INIT_RATE * l)
        h, hm = _half_ffn(lay, hm, h, ffn_w1[l, 0].astype(BF16), ffn_w3[l, 0].astype(BF16),
                          ffn_w2[l, 0].astype(BF16), mvec(l, 2), *lnp(l, 0), mvec(l, 3), mvec(l, 4))
        s5_ops = _s5_operators(s5_lam_re[l], s5_lam_im[l], s5_log_step[l], s5_b_re[l], s5_b_im[l],
                               s5_c_re[l], s5_c_im[l])
        mixed = _token_mixer(lay, hm, rope, lam_init, w_in[l], att_lam[l], att_subln[l],
                             ssd_conv_w[l], ssd_conv_b[l], ssd_a_log[l], ssd_dt_bias[l], ssd_d[l],
                             ssd_norm[l], s5_ops, s5_d[l], s5_glu_w[l], s5_glu_b[l], w_branch[l])
        h, hm = _out_norm(lay, mixed, h, w_out[l].astype(BF16), mvec(l, 5), *lnp(l, 1),
                          mvec(l, 6), mvec(l, 7))
        nxt = (mvec(l + 1, 0), mvec(l + 1, 1)) if l + 1 < depth else (zero_vec, zero_vec)
        h, hm = _half_ffn(lay, hm, h, ffn_w1[l, 1].astype(BF16), ffn_w3[l, 1].astype(BF16),
                          ffn_w2[l, 1].astype(BF16), mvec(l, 8), *lnp(l, 2), *nxt)
    return h[:lay.n_lat].reshape(x.shape)


def kernel(x, c, ctx, c_ctx, w_mod, b_mod, ln_g, ln_b, ffn_w1, ffn_w3, ffn_w2, w_in, att_lam, att_subln, ssd_conv_w, ssd_conv_b, ssd_a_log, ssd_dt_bias, ssd_d, ssd_norm, s5_lam_re, s5_lam_im, s5_log_step, s5_b_re, s5_b_im, s5_c_re, s5_c_im, s5_d, s5_glu_w, s5_glu_b, w_branch, w_out):
    lay = Layout(x.shape[0], x.shape[1], ctx.shape[1])
    return _trunk(lay, x, c, ctx, c_ctx, w_mod, b_mod, ln_g, ln_b, ffn_w1, ffn_w3, ffn_w2, w_in,
                  att_lam, att_subln, ssd_conv_w, ssd_conv_b, ssd_a_log, ssd_dt_bias, ssd_d, ssd_norm,
                  s5_lam_re, s5_lam_im, s5_log_step, s5_b_re, s5_b_im, s5_c_re, s5_c_im,
                  s5_d, s5_glu_w, s5_glu_b, w_branch, w_out)
```

```python
import functools
import math

import jax
import jax.numpy as jnp
from jax import lax
from jax.experimental import pallas as pl
from jax.experimental.pallas import tpu as pltpu

F32 = jnp.float32
BF16 = jnp.bfloat16
LOG2_E = math.log2(math.e)

D_MODEL = 2048
DEPTH = 2
GRID_W = 64
DN_ALPHA = (2 * DEPTH) ** 0.25
N_SUB = 3
N_MOD = 3 * N_SUB
FFN_HALF = 0.5
D_FF = 5632
LN_EPS = 1e-5
RMS_EPS = 1e-6
BRANCH_W = D_MODEL // 2
N_BRANCH = 3
ATT_DH = 64
ATT_DV = 2 * ATT_DH
ATT_HEADS = BRANCH_W // ATT_DV
ROPE_BASE = 10000.0
ROPE_FREQS = ATT_DH // 4
LAMBDA_INIT_BASE = 0.8
LAMBDA_INIT_SPAN = 0.6
LAMBDA_INIT_RATE = 0.3
SSD_P = 64
SSD_HEADS = BRANCH_W // SSD_P
SSD_GROUPS = 4
SSD_HPG = SSD_HEADS // SSD_GROUPS
SSD_N = 128
SSD_CONV = 5
SSD_INNER = SSD_HEADS * SSD_P
SSD_BC_W = SSD_GROUPS * SSD_N
SSD_XBC_W = SSD_INNER + 2 * SSD_BC_W
SSD_NORM_GROUP = SSD_INNER // SSD_GROUPS
S5_CH = BRANCH_W
S5_GROUP_CH = 16
S5_GROUPS = S5_CH // S5_GROUP_CH
S5_N = 64

LANES = 128
SUBLANES = 8
VMEM_LIMIT_BYTES = 56 * 1024 * 1024
MOD_ROWS = 8
TM = 512
TN_FF = 512
TN_PROJ = 1024
TN_MERGE = 512
TN_MOD = 1024
ATT_TQ = 1024
ATT_TK = 1024
SSD_L = 128
SSD_TM = 256
S5_L = 16
S5_CW = S5_L * S5_GROUP_CH
DT_PAD = LANES


def _params(*sem):
    return pltpu.CompilerParams(dimension_semantics=sem, vmem_limit_bytes=VMEM_LIMIT_BYTES)


class Layout:
    def __init__(self, batch, seq, ctx):
        self.batch, self.seq, self.ctx = batch, seq, ctx
        self.n_lat = batch * seq
        self.n_tok = batch * (seq + ctx)
        for t in (TM, SSD_TM, SSD_L):
            assert seq % t == 0 and (batch * ctx) % t == 0, (seq, ctx, t)
        assert ctx == SSD_TM and ctx % SSD_L == 0
        assert seq % min(ATT_TQ, seq) == 0 and seq % min(ATT_TK, seq) == 0
        assert batch * 2 == SUBLANES and seq % GRID_W == 0

    def sample_of_tile(self, i, tile):
        return jnp.where(i < self.n_lat // tile, i // (self.seq // tile), self.batch)


def _post_norm_emit(h, upd, lng, lnb, nsh, nsc, ho_ref, hmo_ref):
    t = DN_ALPHA * h + upd
    mu = jnp.mean(t, axis=-1, keepdims=True)
    tc = t - mu
    var = jnp.mean(tc * tc, axis=-1, keepdims=True)
    hn = tc * lax.rsqrt(var + LN_EPS) * lng + lnb
    ho_ref[...] = hn
    hmo_ref[...] = (hn * (1.0 + nsc) + nsh).astype(hmo_ref.dtype)


def _mod_kernel(c_ref, w_ref, b_ref, o_ref):
    c = c_ref[...]
    s = (c * jax.nn.sigmoid(c)).astype(BF16)
    o_ref[...] = jnp.dot(s, w_ref[...].astype(BF16), preferred_element_type=F32) + b_ref[...]


def _mod_all(cc, w_mod, b_mod):
    depth, d, n = w_mod.shape
    return pl.pallas_call(
        _mod_kernel,
        grid=(depth, n // TN_MOD),
        in_specs=[pl.BlockSpec((MOD_ROWS, d), lambda l, j: (0, 0)),
                  pl.BlockSpec((None, d, TN_MOD), lambda l, j: (l, 0, j)),
                  pl.BlockSpec((None, 1, TN_MOD), lambda l, j: (l, 0, j))],
        out_specs=pl.BlockSpec((None, MOD_ROWS, TN_MOD), lambda l, j: (l, 0, j)),
        out_shape=jax.ShapeDtypeStruct((depth, MOD_ROWS, n), F32),
        compiler_params=_params("parallel", "parallel"),
        name="mod_matmul",
    )(cc, w_mod, b_mod.reshape(depth, 1, n))


def _modulate_kernel(h_ref, sh_ref, sc_ref, o_ref):
    o_ref[...] = (h_ref[...] * (1.0 + sc_ref[...]) + sh_ref[...]).astype(o_ref.dtype)


def _modulate(lay, h, sh, sc):
    vec = pl.BlockSpec((None, 1, D_MODEL), lambda i: (lay.sample_of_tile(i, TM), 0, 0))
    row = pl.BlockSpec((TM, D_MODEL), lambda i: (i, 0))
    return pl.pallas_call(
        _modulate_kernel, grid=(lay.n_tok // TM,),
        in_specs=[row, vec, vec], out_specs=row,
        out_shape=jax.ShapeDtypeStruct((lay.n_tok, D_MODEL), BF16),
        compiler_params=_params("parallel"), name="modulate",
    )(h, sh, sc)


def _ffn_kernel(hm_ref, h_ref, w1_ref, w3_ref, w2_ref, gate_ref, lng_ref, lnb_ref, nsh_ref, nsc_ref,
                ho_ref, hmo_ref, acc_ref):
    j = pl.program_id(1)
    hm = hm_ref[...]
    a = jnp.dot(hm, w1_ref[...], preferred_element_type=F32)
    b = jnp.dot(hm, w3_ref[...], preferred_element_type=F32)
    p = (a * jax.nn.sigmoid(a) * b).astype(BF16)
    contrib = jnp.dot(p, w2_ref[...], preferred_element_type=F32)

    @pl.when(j == 0)
    def _():
        acc_ref[...] = contrib

    @pl.when(j > 0)
    def _():
        acc_ref[...] += contrib

    @pl.when(j == pl.num_programs(1) - 1)
    def _():
        upd = (FFN_HALF * gate_ref[...]) * acc_ref[...]
        _post_norm_emit(h_ref[...], upd, lng_ref[...], lnb_ref[...], nsh_ref[...], nsc_ref[...],
                        ho_ref, hmo_ref)


def _half_ffn(lay, hm, h, w1, w3, w2, gate, lng, lnb, nsh, nsc):
    vec = pl.BlockSpec((None, 1, D_MODEL), lambda i, j: (lay.sample_of_tile(i, TM), 0, 0))
    par = pl.BlockSpec((1, D_MODEL), lambda i, j: (0, 0))
    row = pl.BlockSpec((TM, D_MODEL), lambda i, j: (i, 0))
    return pl.pallas_call(
        _ffn_kernel,
        grid=(lay.n_tok // TM, D_FF // TN_FF),
        in_specs=[row, row,
                  pl.BlockSpec((D_MODEL, TN_FF), lambda i, j: (0, j)),
                  pl.BlockSpec((D_MODEL, TN_FF), lambda i, j: (0, j)),
                  pl.BlockSpec((TN_FF, D_MODEL), lambda i, j: (j, 0)),
                  vec, par, par, vec, vec],
        out_specs=[row, row],
        out_shape=[jax.ShapeDtypeStruct((lay.n_tok, D_MODEL), F32),
                   jax.ShapeDtypeStruct((lay.n_tok, D_MODEL), BF16)],
        scratch_shapes=[pltpu.VMEM((TM, D_MODEL), F32)],
        compiler_params=_params("parallel", "arbitrary"), name="half_ffn",
    )(hm, h, w1, w3, w2, gate, lng, lnb, nsh, nsc)


def _proj_kernel(x_ref, w_ref, o_ref):
    o_ref[...] = jnp.dot(x_ref[...], w_ref[...], preferred_element_type=F32).astype(o_ref.dtype)


def _proj(lay, hm, w, out_dtype, name):
    n = w.shape[1]
    tn = min(TN_PROJ, n)
    return pl.pallas_call(
        _proj_kernel, grid=(lay.n_tok // TM, n // tn),
        in_specs=[pl.BlockSpec((TM, D_MODEL), lambda i, j: (i, 0)),
                  pl.BlockSpec((D_MODEL, tn), lambda i, j: (0, j))],
        out_specs=pl.BlockSpec((TM, tn), lambda i, j: (i, j)),
        out_shape=jax.ShapeDtypeStruct((lay.n_tok, n), out_dtype),
        compiler_params=_params("parallel", "parallel"), name=name,
    )(hm, w)


def _proj_qk_kernel(x_ref, w_ref, cos_ref, sin_ref, o_ref, *, n_lat_tiles, n_q_tiles):
    i, j = pl.program_id(0), pl.program_id(1)
    acc = jnp.dot(x_ref[...], w_ref[...], preferred_element_type=F32)
    is_q = j < n_q_tiles
    scale = jnp.where(is_q, ATT_DH ** -0.5 * LOG2_E, 1.0).astype(F32)
    lane = lax.broadcasted_iota(jnp.int32, (acc.shape[0], LANES), 1)
    low = lane < ATT_DH
    pad = jnp.where(lane == ATT_DH, jnp.where(is_q, 0.0, 1.0).astype(F32), 0.0)

    def emit(rotate):
        if rotate:
            cos, sin = cos_ref[...], sin_ref[...]
            first = (lane % (2 * ROPE_FREQS)) < ROPE_FREQS
        for k in range(acc.shape[1] // LANES):
            t = acc[:, k * LANES:(k + 1) * LANES]
            if rotate:
                partner = jnp.where(first, pltpu.roll(t, LANES - ROPE_FREQS, 1), pltpu.roll(t, ROPE_FREQS, 1))
                t = t * cos + partner * sin
            t = t * scale
            o_ref[:, 2 * k * LANES:(2 * k + 1) * LANES] = jnp.where(low, t, pad).astype(o_ref.dtype)
            o_ref[:, (2 * k + 1) * LANES:(2 * k + 2) * LANES] = (
                jnp.where(low, pltpu.roll(t, ATT_DH, 1), pad).astype(o_ref.dtype))

    @pl.when(i < n_lat_tiles)
    def _():
        emit(True)

    @pl.when(i >= n_lat_tiles)
    def _():
        emit(False)


def _rope_tables(seq):
    rows = seq // GRID_W
    row = jnp.repeat(jnp.arange(rows), GRID_W)
    col = jnp.tile(jnp.arange(GRID_W), rows)
    inv = ROPE_BASE ** (-jnp.arange(ROPE_FREQS, dtype=F32) / ROPE_FREQS)
    ar, ac = row[:, None] * inv, col[:, None] * inv
    cos = jnp.concatenate([jnp.cos(ar), jnp.cos(ar), jnp.cos(ac), jnp.cos(ac)], axis=1)
    sin = jnp.concatenate([-jnp.sin(ar), jnp.sin(ar), -jnp.sin(ac), jnp.sin(ac)], axis=1)
    return jnp.tile(cos, (1, 2)), jnp.tile(sin, (1, 2))


def _proj_qk(lay, hm, w, cos, sin):
    n = w.shape[1]
    tps = lay.seq // TM
    kern = functools.partial(_proj_qk_kernel, n_lat_tiles=lay.n_lat // TM, n_q_tiles=(n // 2) // TN_PROJ)
    tab = pl.BlockSpec((TM, LANES), lambda i, j: (i % tps, 0))
    return pl.pallas_call(
        kern, grid=(lay.n_tok // TM, n // TN_PROJ),
        in_specs=[pl.BlockSpec((TM, D_MODEL), lambda i, j: (i, 0)),
                  pl.BlockSpec((D_MODEL, TN_PROJ), lambda i, j: (0, j)), tab, tab],
        out_specs=pl.BlockSpec((TM, 2 * TN_PROJ), lambda i, j: (i, j)),
        out_shape=jax.ShapeDtypeStruct((lay.n_tok, 2 * n), BF16),
        compiler_params=_params("parallel", "parallel"), name="proj_qk",
    )(hm, w, cos, sin)


def _proj_v_kernel(x_ref, w_ref, o_ref):
    acc = jnp.dot(x_ref[...], w_ref[...], preferred_element_type=F32)
    lane = lax.broadcasted_iota(jnp.int32, (acc.shape[0], LANES), 1)
    ones_col = jnp.where(lane == 0, 1.0, 0.0).astype(o_ref.dtype)
    for k in range(acc.shape[1] // LANES):
        o_ref[:, 2 * k * LANES:(2 * k + 1) * LANES] = acc[:, k * LANES:(k + 1) * LANES].astype(o_ref.dtype)
        o_ref[:, (2 * k + 1) * LANES:(2 * k + 2) * LANES] = ones_col


def _proj_v(lay, hm, w):
    n = w.shape[1]
    return pl.pallas_call(
        _proj_v_kernel, grid=(lay.n_tok // TM, n // TN_PROJ),
        in_specs=[pl.BlockSpec((TM, D_MODEL), lambda i, j: (i, 0)),
                  pl.BlockSpec((D_MODEL, TN_PROJ), lambda i, j: (0, j))],
        out_specs=pl.BlockSpec((TM, 2 * TN_PROJ), lambda i, j: (i, j)),
        out_shape=jax.ShapeDtypeStruct((lay.n_tok, 2 * n), BF16),
        compiler_params=_params("parallel", "parallel"), name="proj_v",
    )(hm, w)


def _attn_kernel(lamv_ref, subln_ref, q0_ref, q1_ref, kc0_ref, kc1_ref, vc_ref, *rest, n_lat_chunks, tk,
                 lam_init):
    if n_lat_chunks:
        kl0_ref, kl1_ref, vl_ref, o_ref = rest
    else:
        kl0_ref = kl1_ref = vl_ref = None
        _, o_ref = rest
    lv = lamv_ref[...]
    lam = (jnp.exp(jnp.sum(lv[0:1] * lv[1:2], axis=-1, keepdims=True))
           - jnp.exp(jnp.sum(lv[2:3] * lv[3:4], axis=-1, keepdims=True)) + lam_init)

    def scores(q, k):
        return lax.dot_general(q, k, (((1,), (1,)), ((), ())), preferred_element_type=F32)

    def row_max(s):
        m = s[:, :LANES]
        for t in range(1, s.shape[1] // LANES):
            m = jnp.maximum(m, s[:, t * LANES:(t + 1) * LANES])
        return jnp.max(m, axis=-1, keepdims=True)

    def lat_chunk(ref, c):
        return ref[pl.ds(pl.multiple_of(c * tk, tk), tk), :]

    def first(q, k, v):
        s = scores(q, k)
        m = row_max(s)
        return m, jnp.dot(jnp.exp2(s - m).astype(BF16), v, preferred_element_type=F32)

    def update(state, q, k, v):
        m, acc = state
        s = scores(q, k)
        m_new = jnp.maximum(m, row_max(s))
        e = jnp.exp2(s - m_new).astype(BF16)
        return m_new, jnp.exp2(m - m_new) * acc + jnp.dot(e, v, preferred_element_type=F32)

    qs = (q0_ref[...], q1_ref[...])
    kls = (kl0_ref, kl1_ref)
    vc = vc_ref[...]
    states = tuple(first(q, kc_ref[...], vc) for q, kc_ref in zip(qs, (kc0_ref, kc1_ref)))
    if n_lat_chunks:
        states = lax.fori_loop(
            0, n_lat_chunks,
            lambda c, st: tuple(update(s_, q, lat_chunk(kl, c), lat_chunk(vl_ref, c))
                                for s_, q, kl in zip(st, qs, kls)), states)
    outs = [a[:, :ATT_DV] / a[:, ATT_DV:ATT_DV + 1] for _, a in states]
    o = outs[0] - lam * outs[1]
    o = o * lax.rsqrt(jnp.mean(o * o, axis=-1, keepdims=True) + RMS_EPS) * subln_ref[...] * (1.0 - lam_init)
    o_ref[...] = o.astype(o_ref.dtype)


def _diff_attention(lay, qk, v, att_lam, subln, lam_init):
    b_, s_, c_ = lay.batch, lay.seq, lay.ctx
    kcol = 2 * ATT_HEADS
    lamv = pl.BlockSpec((4, ATT_DH), lambda *_: (0, 0))
    sub = pl.BlockSpec((1, ATT_DV), lambda *_: (0, 0))
    out_shape = jax.ShapeDtypeStruct((lay.n_tok, BRANCH_W), BF16)
    ctx_blk = lay.n_lat // c_
    vw = 2 * ATT_DV

    def specs(tq, q_row):
        return [lamv, sub,
                pl.BlockSpec((tq, LANES), lambda b, h, i: (q_row(b, i), 2 * h)),
                pl.BlockSpec((tq, LANES), lambda b, h, i: (q_row(b, i), 2 * h + 1)),
                pl.BlockSpec((c_, LANES), lambda b, h, i: (ctx_blk + b, kcol + 2 * h)),
                pl.BlockSpec((c_, LANES), lambda b, h, i: (ctx_blk + b, kcol + 2 * h + 1)),
                pl.BlockSpec((c_, vw), lambda b, h, i: (ctx_blk + b, h))]

    tq, tk = min(ATT_TQ, s_), min(ATT_TK, s_)
    n_q = s_ // tq
    lat_row = lambda b, i: b * n_q + i
    lat = pl.pallas_call(
        functools.partial(_attn_kernel, n_lat_chunks=s_ // tk, tk=tk, lam_init=lam_init),
        grid=(b_, ATT_HEADS, n_q),
        in_specs=specs(tq, lat_row) + [
            pl.BlockSpec((s_, LANES), lambda b, h, i: (b, kcol + 2 * h)),
            pl.BlockSpec((s_, LANES), lambda b, h, i: (b, kcol + 2 * h + 1)),
            pl.BlockSpec((s_, vw), lambda b, h, i: (b, h))],
        out_specs=pl.BlockSpec((tq, LANES), lambda b, h, i: (lat_row(b, i), h)),
        out_shape=out_shape,
        compiler_params=_params("parallel", "parallel", "arbitrary"), name="attn_latent",
    )(att_lam, subln, qk, qk, qk, qk, v, qk, qk, v)
    ctx_row = lambda b, i: ctx_blk + b
    return pl.pallas_call(
        functools.partial(_attn_kernel, n_lat_chunks=0, tk=0, lam_init=lam_init),
        grid=(b_, ATT_HEADS, 1),
        in_specs=specs(c_, ctx_row) + [pl.BlockSpec(memory_space=pl.ANY)],
        out_specs=pl.BlockSpec((c_, LANES), lambda b, h, i: (ctx_row(b, i), h)),
        out_shape=out_shape, input_output_aliases={7: 0},
        compiler_params=_params("parallel", "parallel", "arbitrary"), name="attn_context",
    )(att_lam, subln, qk, qk, qk, qk, v, lat)


def _ssd_prep_kernel(prev_ref, cur_ref, next_ref, dt_ref, cw_ref, cb_ref, dtb_ref,
                     xs_ref, bm_ref, cm_ref, dts_ref, dtst_ref, ext_ref, *, tiles_per_seq, n_lat_tiles):
    i = pl.program_id(0)
    is_ctx = i >= n_lat_tiles
    first = jnp.logical_or(is_ctx, i % tiles_per_seq == 0)
    last = jnp.logical_or(is_ctx, i % tiles_per_seq == tiles_per_seq - 1)
    tm = cur_ref.shape[0]
    ext_ref[0:SUBLANES, :] = jnp.where(first, 0.0, prev_ref[...])
    ext_ref[SUBLANES:SUBLANES + tm, :] = cur_ref[...]
    ext_ref[SUBLANES + tm:, :] = jnp.where(last, 0.0, next_ref[...])
    acc = jnp.zeros(cur_ref.shape, F32) + cb_ref[...]
    for k in range(SSD_CONV):
        start = SUBLANES + k - SSD_CONV // 2
        acc = acc + ext_ref[start:start + tm, :] * cw_ref[k:k + 1, :]
    act = acc * jax.nn.sigmoid(acc)
    xs_ref[...] = act[:, :SSD_INNER]
    bm_ref[...] = act[:, SSD_INNER:SSD_INNER + SSD_BC_W].astype(bm_ref.dtype)
    cm_ref[...] = act[:, SSD_INNER + SSD_BC_W:].astype(cm_ref.dtype)
    x = dt_ref[...] + dtb_ref[...]
    sp = jnp.maximum(x, 0.0) + jnp.log1p(jnp.exp(-jnp.abs(x)))
    dts_ref[0] = sp
    dts_ref[1] = pltpu.roll(sp, DT_PAD - SSD_HEADS, 1)
    spt = sp.T
    dtst_ref[0] = spt[0:SSD_HEADS]
    dtst_ref[1] = spt[SSD_HEADS:2 * SSD_HEADS]


def _ssd_prep(lay, xbc, dt, conv_w, conv_b, dt_bias):
    n, tm = lay.n_tok, SSD_TM
    sub_per_tile = tm // SUBLANES
    n_sub = n // SUBLANES
    kern = functools.partial(_ssd_prep_kernel, tiles_per_seq=lay.seq // tm, n_lat_tiles=lay.n_lat // tm)
    row = lambda w: pl.BlockSpec((tm, w), lambda i: (i, 0))
    return pl.pallas_call(
        kern, grid=(n // tm,),
        in_specs=[pl.BlockSpec((SUBLANES, SSD_XBC_W), lambda i: (jnp.maximum(i * sub_per_tile - 1, 0), 0)),
                  row(SSD_XBC_W),
                  pl.BlockSpec((SUBLANES, SSD_XBC_W),
                               lambda i: (jnp.minimum((i + 1) * sub_per_tile, n_sub - 1), 0)),
                  row(DT_PAD),
                  pl.BlockSpec((SUBLANES, SSD_XBC_W), lambda i: (0, 0)),
                  pl.BlockSpec((1, SSD_XBC_W), lambda i: (0, 0)),
                  pl.BlockSpec((1, DT_PAD), lambda i: (0, 0))],
        out_specs=[row(SSD_INNER), row(SSD_BC_W), row(SSD_BC_W),
                   pl.BlockSpec((2, tm, DT_PAD), lambda i: (0, i, 0)),
                   pl.BlockSpec((2, SSD_HEADS, tm), lambda i: (0, 0, i))],
        out_shape=[jax.ShapeDtypeStruct((n, SSD_INNER), F32),
                   jax.ShapeDtypeStruct((n, SSD_BC_W), BF16),
                   jax.ShapeDtypeStruct((n, SSD_BC_W), BF16),
                   jax.ShapeDtypeStruct((2, n, DT_PAD), F32),
                   jax.ShapeDtypeStruct((2, SSD_HEADS, n), F32)],
        scratch_shapes=[pltpu.VMEM((tm + 2 * SUBLANES, SSD_XBC_W), F32)],
        compiler_params=_params("parallel"), name="ssd_prep",
    )(xbc, xbc, xbc, dt, conv_w, conv_b, dt_bias)


def _split_dot(a, x, x_is_lhs=False):
    out = None
    r = x
    for _ in range(3):
        t = r.astype(BF16)
        r = r - t.astype(F32)
        d = (jnp.dot(t, a, preferred_element_type=F32) if x_is_lhs
             else jnp.dot(a, t, preferred_element_type=F32))
        out = d if out is None else out + d
    return out


def _ssd_kernel(xs_ref, bm_ref, cm_ref, dts_ref, dtst_ref, arow_ref, acol_ref, y_ref, h_ref):
    sign = 1 - 2 * pl.program_id(1)

    @pl.when(pl.program_id(2) == 0)
    def _():
        h_ref[...] = jnp.zeros(h_ref.shape, F32)

    ll = xs_ref.shape[0]
    r = lax.broadcasted_iota(jnp.int32, (ll, ll), 0)
    c = lax.broadcasted_iota(jnp.int32, (ll, ll), 1)
    mask = (r - c) * sign >= 0
    mask_t = (r - c) * sign <= 0
    one_hot = lambda m: jnp.where(m, 1.0, 0.0).astype(BF16)
    dts = dts_ref[...]
    da = dts * arow_ref[...]
    da_t = dtst_ref[...] * acol_ref[...]
    cs = _split_dot(one_hot(mask), da)
    cs_t = _split_dot(one_hot(mask_t), da_t, x_is_lhs=True)
    tot = jnp.sum(da, axis=0, keepdims=True)
    to_end = jnp.exp(tot - cs)
    from_start = jnp.exp(cs)
    chunk_decay = jnp.exp(tot)
    lane = lax.broadcasted_iota(jnp.int32, (ll, LANES), 1)
    lo_half = lane < SSD_P

    def pair_expand(t, hd):
        rows = t.shape[0]
        return jnp.where(lo_half[:rows], jnp.broadcast_to(t[:, hd:hd + 1], (rows, LANES)),
                         jnp.broadcast_to(t[:, hd + 1:hd + 2], (rows, LANES)))

    gw = SSD_HPG * SSD_P
    for g in range(SSD_GROUPS):
        bg = bm_ref[:, g * SSD_N:(g + 1) * SSD_N]
        cg = cm_ref[:, g * SSD_N:(g + 1) * SSD_N]
        cb = lax.dot_general(cg, bg, (((1,), (1,)), ((), ())), preferred_element_type=F32)
        y_parts, xw_parts, fs_parts, cd_parts = [], [], [], []
        for pr in range(SSD_HPG // 2):
            hd = g * SSD_HPG + 2 * pr
            xdt = xs_ref[:, hd * SSD_P:(hd + 2) * SSD_P] * pair_expand(dts, hd)
            xw_parts.append((xdt * pair_expand(to_end, hd)).astype(BF16))
            fs_parts.append(pair_expand(from_start, hd))
            cd_parts.append(pair_expand(chunk_decay, hd))
            yp = None
            for k in range(2):
                seg = cs[:, hd + k:hd + k + 1] - cs_t[hd + k:hd + k + 1, :]
                dec = jnp.where(mask, jnp.exp(jnp.where(mask, seg, 0.0)), 0.0)
                w = (cb * dec).astype(BF16)
                xk = jnp.where(lo_half if k == 0 else jnp.logical_not(lo_half), xdt, 0.0).astype(BF16)
                d = jnp.dot(w, xk, preferred_element_type=F32)
                yp = d if yp is None else yp + d
            y_parts.append(yp)
        xw = jnp.concatenate(xw_parts, axis=1)
        states = lax.dot_general(bg, xw, (((0,), (0,)), ((), ())), preferred_element_type=F32)
        h_prev = h_ref[g]
        y_off = jnp.dot(cg, h_prev.astype(BF16), preferred_element_type=F32) * jnp.concatenate(fs_parts, axis=1)
        y_ref[:, g * gw:(g + 1) * gw] = jnp.concatenate(y_parts, axis=1) + y_off
        h_ref[g] = h_prev * jnp.concatenate(cd_parts, axis=1) + states


def _ssd_scan(lay, xs, bm, cm, dts, dtst, a_row, a_col):
    ll = SSD_L
    n_ctx, n_lat = lay.ctx // ll, lay.seq // ll
    ctx_base = lay.n_lat // ll

    def rb(b, d, s):
        cstep = jnp.where(d == 0, s, n_ctx - 1 - s)
        lstep = jnp.where(d == 0, s - n_ctx, n_lat - 1 - (s - n_ctx))
        return jnp.where(s < n_ctx, ctx_base + b * n_ctx + cstep, b * n_lat + lstep)

    row = lambda w: pl.BlockSpec((ll, w), lambda b, d, s: (rb(b, d, s), 0))
    return pl.pallas_call(
        _ssd_kernel, grid=(lay.batch, 2, n_ctx + n_lat),
        in_specs=[row(SSD_INNER), row(SSD_BC_W), row(SSD_BC_W),
                  pl.BlockSpec((None, ll, DT_PAD), lambda b, d, s: (d, rb(b, d, s), 0)),
                  pl.BlockSpec((None, SSD_HEADS, ll), lambda b, d, s: (d, 0, rb(b, d, s))),
                  pl.BlockSpec((None, 1, DT_PAD), lambda b, d, s: (d, 0, 0)),
                  pl.BlockSpec((None, SSD_HEADS, 1), lambda b, d, s: (d, 0, 0))],
        out_specs=pl.BlockSpec((None, ll, SSD_INNER), lambda b, d, s: (d, rb(b, d, s), 0)),
        out_shape=jax.ShapeDtypeStruct((2, lay.n_tok, SSD_INNER), F32),
        scratch_shapes=[pltpu.VMEM((SSD_GROUPS, SSD_N, SSD_HPG * SSD_P), F32)],
        compiler_params=_params("parallel", "parallel", "arbitrary"), name="ssd_scan",
    )(xs, bm, cm, dts, dtst, a_row, a_col)


def _ssd_gate_norm_kernel(y0_ref, y1_ref, xs_ref, z_ref, dsk_ref, nw_ref, o_ref):
    z = z_ref[...]
    gated = (y0_ref[...] + y1_ref[...] + xs_ref[...] * dsk_ref[...]) * (z * jax.nn.sigmoid(z))
    nw = nw_ref[...]
    for g in range(SSD_GROUPS):
        sl = slice(g * SSD_NORM_GROUP, (g + 1) * SSD_NORM_GROUP)
        t = gated[:, sl]
        o_ref[:, sl] = (t * lax.rsqrt(jnp.mean(t * t, axis=-1, keepdims=True) + RMS_EPS)
                        * nw[:, sl]).astype(o_ref.dtype)


def _ssd_gate_norm(lay, ydir, xs, z, dsk, nw):
    tm = SSD_TM
    row = pl.BlockSpec((tm, SSD_INNER), lambda i: (i, 0))
    par = pl.BlockSpec((1, SSD_INNER), lambda i: (0, 0))
    return pl.pallas_call(
        _ssd_gate_norm_kernel, grid=(lay.n_tok // tm,),
        in_specs=[pl.BlockSpec((None, tm, SSD_INNER), lambda i: (0, i, 0)),
                  pl.BlockSpec((None, tm, SSD_INNER), lambda i: (1, i, 0)), row, row, par, par],
        out_specs=row, out_shape=jax.ShapeDtypeStruct((lay.n_tok, SSD_INNER), BF16),
        compiler_params=_params("parallel"), name="ssd_gate_norm",
    )(ydir, ydir, xs, z, dsk, nw)


def _s5_operators(lam_re, lam_im, log_step, b_re, b_im, c_re, c_im):
    hp = lax.Precision.HIGHEST
    ll, hh = S5_L, S5_GROUP_CH
    step = jnp.exp(log_step)[..., None, None]
    d = jnp.arange(ll + 1, dtype=F32)
    p_mag = jnp.exp(lam_re[..., None] * step * d)
    p_ang = lam_im[..., None] * step * d
    p_re, p_im = p_mag * jnp.cos(p_ang), p_mag * jnp.sin(p_ang)
    ab_re, ab_im = p_re[..., 1], p_im[..., 1]
    den = lam_re * lam_re + lam_im * lam_im
    k_re = ((ab_re - 1.0) * lam_re + ab_im * lam_im) / den
    k_im = (ab_im * lam_re - (ab_re - 1.0) * lam_im) / den
    bb_re = k_re[..., None] * b_re - k_im[..., None] * b_im
    bb_im = k_re[..., None] * b_im + k_im[..., None] * b_re
    cp_re = c_re[..., None] * p_re[:, :, None] - c_im[..., None] * p_im[:, :, None]
    cp_im = c_re[..., None] * p_im[:, :, None] + c_im[..., None] * p_re[:, :, None]
    kern = (jnp.einsum('zghnd,zgnk->zgdhk', cp_re, bb_re, precision=hp)
            - jnp.einsum('zghnd,zgnk->zgdhk', cp_im, bb_im, precision=hp))
    s_idx = jnp.arange(ll)[:, None]
    l_idx = jnp.arange(ll)[None, :]
    t_f = jnp.where((l_idx >= s_idx)[None, :, :, None, None], kern[0][:, jnp.clip(l_idx - s_idx, 0, ll)], 0.0)
    t_b = jnp.where((s_idx >= l_idx)[None, :, :, None, None], kern[1][:, jnp.clip(s_idx - l_idx, 0, ll)], 0.0)
    toep = (t_f + t_b).transpose(0, 1, 4, 2, 3).reshape(S5_GROUPS, S5_CW, S5_CW)

    def state_in(z, powers):
        pr, pi = p_re[z][..., powers], p_im[z][..., powers]
        re = pr[..., None] * bb_re[z][:, :, None] - pi[..., None] * bb_im[z][:, :, None]
        im = pr[..., None] * bb_im[z][:, :, None] + pi[..., None] * bb_re[z][:, :, None]
        re = re.transpose(0, 2, 3, 1).reshape(S5_GROUPS, S5_CW, S5_N)
        im = im.transpose(0, 2, 3, 1).reshape(S5_GROUPS, S5_CW, S5_N)
        return jnp.concatenate([re, im], axis=-1)

    ws_f = state_in(0, ll - 1 - jnp.arange(ll))
    ws_b = state_in(1, jnp.arange(ll))
    w1 = jnp.concatenate([toep, ws_f, ws_b], axis=-1)

    def state_out(z, powers):
        re = cp_re[z][..., powers].transpose(0, 2, 3, 1).reshape(S5_GROUPS, S5_N, S5_CW)
        im = cp_im[z][..., powers].transpose(0, 2, 3, 1).reshape(S5_GROUPS, S5_N, S5_CW)
        return jnp.concatenate([re, -im], axis=1)

    wo = jnp.concatenate([state_out(0, jnp.arange(ll) + 1), state_out(1, ll - jnp.arange(ll))], axis=1)
    ar, ai = p_re[..., ll], p_im[..., ll]
    a1 = jnp.concatenate([ar, ar], axis=-1)
    a2 = jnp.concatenate([-ai, ai], axis=-1)
    zeros = jnp.zeros_like(a1[0])
    av = jnp.stack([a1[0], a2[0], a1[1], a2[1], zeros, zeros, zeros, zeros], axis=1)
    return w1.astype(BF16), wo.astype(BF16), av


def _s5_kernel(u_ref, w1_ref, wo_ref, av_ref, y_ref, sf_ref, sfs_ref, sb_ref, sbs_ref, hf_ref, hb_ref,
               *, n_ctx_tiles, n_tiles):
    p = jnp.dot(u_ref[...], w1_ref[...], preferred_element_type=F32)
    nst = 2 * S5_N
    sf = p[:, S5_CW:S5_CW + nst]
    sb = p[:, S5_CW + nst:]
    sf_ref[...] = sf
    sfs_ref[...] = pltpu.roll(sf, S5_N, 1)
    sb_ref[...] = sb
    sbs_ref[...] = pltpu.roll(sb, S5_N, 1)
    av = av_ref[...]
    half = SUBLANES // 2
    a1f, a2f = (jnp.broadcast_to(av[k:k + 1], (half, nst)) for k in (0, 1))
    a1b, a2b = (jnp.broadcast_to(av[k:k + 1], (half, nst)) for k in (2, 3))

    def body(j, carry):
        hf, hfs, hb, hbs = carry
        of = pl.multiple_of(j * SUBLANES, SUBLANES)
        s, ss = sf_ref[pl.ds(of, SUBLANES), :], sfs_ref[pl.ds(of, SUBLANES), :]
        h1 = a1f * hf + a2f * hfs + s[:half]
        h1s = a1f * hfs - a2f * hf + ss[:half]
        hf_ref[pl.ds(of, SUBLANES), :] = jnp.concatenate([hf, h1], axis=0)
        h2 = a1f * h1 + a2f * h1s + s[half:]
        h2s = a1f * h1s - a2f * h1 + ss[half:]
        jb = jnp.where(j < n_ctx_tiles, n_ctx_tiles - 1 - j, n_tiles - 1 - (j - n_ctx_tiles))
        ob = pl.multiple_of(jb * SUBLANES, SUBLANES)
        s, ss = sb_ref[pl.ds(ob, SUBLANES), :], sbs_ref[pl.ds(ob, SUBLANES), :]
        g1 = a1b * hb + a2b * hbs + s[half:]
        g1s = a1b * hbs - a2b * hb + ss[half:]
        hb_ref[pl.ds(ob, SUBLANES), :] = jnp.concatenate([g1, hb], axis=0)
        g2 = a1b * g1 + a2b * g1s + s[:half]
        g2s = a1b * g1s - a2b * g1 + ss[:half]
        return h2, h2s, g2, g2s

    z = jnp.zeros((half, nst), F32)
    lax.fori_loop(0, n_tiles, body, (z, z, z, z))
    wo = wo_ref[...]
    y_ref[...] = (p[:, :S5_CW]
                  + jnp.dot(hf_ref[...].astype(BF16), wo[:nst], preferred_element_type=F32)
                  + jnp.dot(hb_ref[...].astype(BF16), wo[nst:], preferred_element_type=F32))


def _s5_chunked(lay, uc, w1, wo, av):
    rows = lay.n_tok // S5_L
    n_ctx_rows = lay.batch * lay.ctx // S5_L
    kern = functools.partial(_s5_kernel, n_ctx_tiles=n_ctx_rows // SUBLANES, n_tiles=rows // SUBLANES)
    st = pltpu.VMEM((rows, 2 * S5_N), F32)
    return pl.pallas_call(
        kern, grid=(S5_GROUPS,),
        in_specs=[pl.BlockSpec((None, rows, S5_CW), lambda g: (g, 0, 0)),
                  pl.BlockSpec((None, S5_CW, S5_CW + 4 * S5_N), lambda g: (g, 0, 0)),
                  pl.BlockSpec((None, 4 * S5_N, S5_CW), lambda g: (g, 0, 0)),
                  pl.BlockSpec((None, SUBLANES, 2 * S5_N), lambda g: (g, 0, 0))],
        out_specs=pl.BlockSpec((None, rows, S5_CW), lambda g: (g, 0, 0)),
        out_shape=jax.ShapeDtypeStruct((S5_GROUPS, rows, S5_CW), F32),
        scratch_shapes=[st, st, st, st, st, st],
        compiler_params=_params("parallel"), name="s5_chunked",
    )(uc, w1, wo, av)


def _to_chunk_rows(lay, u):
    def part(t, length):
        t = t.reshape(lay.batch, length // S5_L, S5_L, S5_GROUPS, S5_GROUP_CH)
        return t.transpose(3, 1, 0, 2, 4).reshape(S5_GROUPS, (length // S5_L) * lay.batch, S5_CW)
    return jnp.concatenate([part(u[lay.n_lat:], lay.ctx), part(u[:lay.n_lat], lay.seq)], axis=1)


def _from_chunk_rows(lay, y):
    def part(t, length):
        t = t.reshape(S5_GROUPS, length // S5_L, lay.batch, S5_L, S5_GROUP_CH)
        return t.transpose(2, 1, 3, 0, 4).reshape(lay.batch * length, S5_CH)
    n_ctx_rows = lay.batch * lay.ctx // S5_L
    return jnp.concatenate([part(y[:, n_ctx_rows:], lay.seq), part(y[:, :n_ctx_rows], lay.ctx)], axis=0)


def _s5_glu_kernel(ys_ref, u_ref, dsk_ref, w_ref, b_ref, o_ref):
    t = ys_ref[...] + u_ref[...] * dsk_ref[...]
    t = 0.5 * t * (1.0 + jnp.tanh(math.sqrt(2.0 / math.pi) * (t + 0.044715 * (t * t * t))))
    gate = jnp.dot(t.astype(BF16), w_ref[...], preferred_element_type=F32) + b_ref[...]
    o_ref[...] = (t * jax.nn.sigmoid(gate)).astype(o_ref.dtype)


def _s5_glu(lay, ys, u, dsk, w, b):
    row = pl.BlockSpec((TM, S5_CH), lambda i: (i, 0))
    par = pl.BlockSpec((1, S5_CH), lambda i: (0, 0))
    return pl.pallas_call(
        _s5_glu_kernel, grid=(lay.n_tok // TM,),
        in_specs=[row, row, par, pl.BlockSpec((S5_CH, S5_CH), lambda i: (0, 0)), par],
        out_specs=row, out_shape=jax.ShapeDtypeStruct((lay.n_tok, S5_CH), BF16),
        compiler_params=_params("parallel"), name="s5_glu",
    )(ys, u, dsk, w, b)


def _merge_kernel(oa_ref, os_ref, o5_ref, ga_ref, gs_ref, g5_ref, wa_ref, ws_ref, w5_ref, o_ref):
    acc = None
    for o, g, w in ((oa_ref, ga_ref, wa_ref), (os_ref, gs_ref, ws_ref), (o5_ref, g5_ref, w5_ref)):
        t = jax.nn.sigmoid(g[...]) * jnp.dot(o[...], w[...], preferred_element_type=F32)
        acc = t if acc is None else acc + t
    o_ref[...] = acc.astype(o_ref.dtype)


def _merge(lay, o_att, o_ssd, o_s5, g, w_branch):
    nt = D_MODEL // TN_MERGE
    row = pl.BlockSpec((TM, BRANCH_W), lambda i, j: (i, 0))
    gate = lambda k: pl.BlockSpec((TM, TN_MERGE), lambda i, j: (i, k * nt + j))
    wb = lambda k: pl.BlockSpec((None, BRANCH_W, TN_MERGE), lambda i, j: (k, 0, j))
    return pl.pallas_call(
        _merge_kernel, grid=(lay.n_tok // TM, nt),
        in_specs=[row, row, row, gate(0), gate(1), gate(2), wb(0), wb(1), wb(2)],
        out_specs=pl.BlockSpec((TM, TN_MERGE), lambda i, j: (i, j)),
        out_shape=jax.ShapeDtypeStruct((lay.n_tok, D_MODEL), BF16),
        compiler_params=_params("parallel", "parallel"), name="branch_merge",
    )(o_att, o_ssd, o_s5, g, g, g, w_branch, w_branch, w_branch)


def _out_norm_kernel(mx_ref, h_ref, w_ref, gate_ref, lng_ref, lnb_ref, nsh_ref, nsc_ref, ho_ref, hmo_ref):
    y = jnp.dot(mx_ref[...], w_ref[...], preferred_element_type=F32)
    _post_norm_emit(h_ref[...], gate_ref[...] * y, lng_ref[...], lnb_ref[...], nsh_ref[...], nsc_ref[...],
                    ho_ref, hmo_ref)


def _out_norm(lay, mixed, h, w_out, gate, lng, lnb, nsh, nsc):
    vec = pl.BlockSpec((None, 1, D_MODEL), lambda i: (lay.sample_of_tile(i, TM), 0, 0))
    par = pl.BlockSpec((1, D_MODEL), lambda i: (0, 0))
    row = pl.BlockSpec((TM, D_MODEL), lambda i: (i, 0))
    return pl.pallas_call(
        _out_norm_kernel, grid=(lay.n_tok // TM,),
        in_specs=[row, row, pl.BlockSpec((D_MODEL, D_MODEL), lambda i: (0, 0)), vec, par, par, vec, vec],
        out_specs=[row, row],
        out_shape=[jax.ShapeDtypeStruct((lay.n_tok, D_MODEL), F32),
                   jax.ShapeDtypeStruct((lay.n_tok, D_MODEL), BF16)],
        compiler_params=_params("parallel"), name="out_norm",
    )(mixed, h, w_out, gate, lng, lnb, nsh, nsc)


def _pad_cols(t, width):
    return jnp.pad(t, [(0, 0)] * (t.ndim - 1) + [(0, width - t.shape[-1])])


def _token_mixer(lay, hm, rope, lam_init, w_in, att_lam, att_subln, conv_w, conv_b, a_log, dt_bias,
                 ssd_d, ssd_norm, s5_ops, s5_d, glu_w, glu_b, w_branch):
    cuts = [0]
    for w in (2 * BRANCH_W, BRANCH_W, SSD_INNER, SSD_XBC_W, 2 * SSD_HEADS, S5_CH, N_BRANCH * D_MODEL):
        cuts.append(cuts[-1] + w)
    w_qk, w_v, w_z, w_xbc, w_dt, w_u, w_g = (w_in[:, a:b].astype(BF16) for a, b in zip(cuts[:-1], cuts[1:]))
    qk = _proj_qk(lay, hm, w_qk, *rope)
    v = _proj_v(lay, hm, w_v)
    z = _proj(lay, hm, w_z, F32, "proj_z")
    xbc = _proj(lay, hm, w_xbc, F32, "proj_xbc")
    dt = _proj(lay, hm, _pad_cols(w_dt, DT_PAD), F32, "proj_dt")
    u = _proj(lay, hm, w_u, F32, "proj_u")
    g = _proj(lay, hm, w_g, F32, "proj_gate")

    o_att = _diff_attention(lay, qk, v, att_lam, att_subln.reshape(1, ATT_DV), lam_init)

    conv_w8 = jnp.pad(conv_w, ((0, SUBLANES - SSD_CONV), (0, 0)))
    xs, bm, cm, dts, dtst = _ssd_prep(lay, xbc, dt, conv_w8, conv_b.reshape(1, -1),
                                      _pad_cols(dt_bias.reshape(1, -1), DT_PAD))
    a = -jnp.exp(a_log.astype(F32))
    ydir = _ssd_scan(lay, xs, bm, cm, dts, dtst, _pad_cols(a, DT_PAD)[:, None, :], a[:, :, None])
    o_ssd = _ssd_gate_norm(lay, ydir, xs, z, jnp.repeat(ssd_d, SSD_P).reshape(1, -1),
                           ssd_norm.reshape(1, -1))

    ys = _s5_chunked(lay, _to_chunk_rows(lay, u.astype(BF16)), *s5_ops)
    o_s5 = _s5_glu(lay, _from_chunk_rows(lay, ys), u, s5_d.reshape(1, -1), glu_w.astype(BF16),
                   glu_b.reshape(1, -1))

    return _merge(lay, o_att, o_ssd, o_s5, g, w_branch.astype(BF16))


def _trunk(lay, x, c, ctx, c_ctx, w_mod, b_mod, ln_g, ln_b, ffn_w1, ffn_w3, ffn_w2, w_in,
           att_lam, att_subln, ssd_conv_w, ssd_conv_b, ssd_a_log, ssd_dt_bias, ssd_d, ssd_norm,
           s5_lam_re, s5_lam_im, s5_log_step, s5_b_re, s5_b_im, s5_c_re, s5_c_im,
           s5_d, s5_glu_w, s5_glu_b, w_branch, w_out):
    depth = w_mod.shape[0]
    h = jnp.concatenate([x.reshape(lay.n_lat, D_MODEL), ctx.reshape(-1, D_MODEL)], axis=0)
    cc = jnp.concatenate([c, c_ctx[None], jnp.zeros((MOD_ROWS - lay.batch - 1, D_MODEL), F32)], axis=0)
    mod = _mod_all(cc, w_mod, b_mod).reshape(depth, MOD_ROWS, N_MOD, 1, D_MODEL)
    mvec = lambda l, k: mod[l, :, k]
    zero_vec = jnp.zeros((MOD_ROWS, 1, D_MODEL), F32)
    rope = _rope_tables(lay.seq)
    lnp = lambda l, k: (ln_g[l, k].reshape(1, -1), ln_b[l, k].reshape(1, -1))

    hm = _modulate(lay, h, mvec(0, 0), mvec(0, 1))
    for l in range(depth):
        lam_init = LAMBDA_INIT_BASE - LAMBDA_INIT_SPAN * math.exp(-LAMBDA_INIT_RATE * l)
        h, hm = _half_ffn(lay, hm, h, ffn_w1[l, 0].astype(BF16), ffn_w3[l, 0].astype(BF16),
                          ffn_w2[l, 0].astype(BF16), mvec(l, 2), *lnp(l, 0), mvec(l, 3), mvec(l, 4))
        s5_ops = _s5_operators(s5_lam_re[l], s5_lam_im[l], s5_log_step[l], s5_b_re[l], s5_b_im[l],
                               s5_c_re[l], s5_c_im[l])
        mixed = _token_mixer(lay, hm, rope, lam_init, w_in[l], att_lam[l], att_subln[l],
                             ssd_conv_w[l], ssd_conv_b[l], ssd_a_log[l], ssd_dt_bias[l], ssd_d[l],
                             ssd_norm[l], s5_ops, s5_d[l], s5_glu_w[l], s5_glu_b[l], w_branch[l])
        h, hm = _out_norm(lay, mixed, h, w_out[l].astype(BF16), mvec(l, 5), *lnp(l, 1),
                          mvec(l, 6), mvec(l, 7))
        nxt = (mvec(l + 1, 0), mvec(l + 1, 1)) if l + 1 < depth else (zero_vec, zero_vec)
        h, hm = _half_ffn(lay, hm, h, ffn_w1[l, 1].astype(BF16), ffn_w3[l, 1].astype(BF16),
                          ffn_w2[l, 1].astype(BF16), mvec(l, 8), *lnp(l, 2), *nxt)
    return h[:lay.n_lat].reshape(x.shape)


def kernel(x, c, ctx, c_ctx, w_mod, b_mod, ln_g, ln_b, ffn_w1, ffn_w3, ffn_w2, w_in, att_lam, att_subln, ssd_conv_w, ssd_conv_b, ssd_a_log, ssd_dt_bias, ssd_d, ssd_norm, s5_lam_re, s5_lam_im, s5_log_step, s5_b_re, s5_b_im, s5_c_re, s5_c_im, s5_d, s5_glu_w, s5_glu_b, w_branch, w_out):
    lay = Layout(x.shape[0], x.shape[1], ctx.shape[1])
    return _trunk(lay, x, c, ctx, c_ctx, w_mod, b_mod, ln_g, ln_b, ffn_w1, ffn_w3, ffn_w2, w_in,
                  att_lam, att_subln, ssd_conv_w, ssd_conv_b, ssd_a_log, ssd_dt_bias, ssd_d, ssd_norm,
                  s5_lam_re, s5_lam_im, s5_log_step, s5_b_re, s5_b_im, s5_c_re, s5_c_im,
                  s5_d, s5_glu_w, s5_glu_b, w_branch, w_out)
```

```python
import functools
import math

import jax
import jax.numpy as jnp
from jax import lax
from jax.experimental import pallas as pl
from jax.experimental.pallas import tpu as pltpu

F32 = jnp.float32
BF16 = jnp.bfloat16
LOG2_E = math.log2(math.e)

D_MODEL = 2048
DEPTH = 2
GRID_W = 64
DN_ALPHA = (2 * DEPTH) ** 0.25
N_SUB = 3
N_MOD = 3 * N_SUB
FFN_HALF = 0.5
D_FF = 5632
LN_EPS = 1e-5
RMS_EPS = 1e-6
BRANCH_W = D_MODEL // 2
N_BRANCH = 3
ATT_DH = 64
ATT_DV = 2 * ATT_DH
ATT_HEADS = BRANCH_W // ATT_DV
ROPE_BASE = 10000.0
ROPE_FREQS = ATT_DH // 4
LAMBDA_INIT_BASE = 0.8
LAMBDA_INIT_SPAN = 0.6
LAMBDA_INIT_RATE = 0.3
SSD_P = 64
SSD_HEADS = BRANCH_W // SSD_P
SSD_GROUPS = 4
SSD_HPG = SSD_HEADS // SSD_GROUPS
SSD_N = 128
SSD_CONV = 5
SSD_INNER = SSD_HEADS * SSD_P
SSD_BC_W = SSD_GROUPS * SSD_N
SSD_XBC_W = SSD_INNER + 2 * SSD_BC_W
SSD_NORM_GROUP = SSD_INNER // SSD_GROUPS
S5_CH = BRANCH_W
S5_GROUP_CH = 16
S5_GROUPS = S5_CH // S5_GROUP_CH
S5_N = 64

LANES = 128
SUBLANES = 8
VMEM_LIMIT_BYTES = 56 * 1024 * 1024
MOD_ROWS = 8
TM = 512
TN_FF = 512
TN_PROJ = 1024
TN_MERGE = 512
TN_MOD = 1024
ATT_V_ROWS = ATT_DV + 16
ATT_TQ = 1024
ATT_TK = 1024
SSD_L = 128
SSD_TM = 256
S5_L = 16
S5_CW = S5_L * S5_GROUP_CH
DT_PAD = LANES


def _params(*sem):
    return pltpu.CompilerParams(dimension_semantics=sem, vmem_limit_bytes=VMEM_LIMIT_BYTES)


class Layout:
    def __init__(self, batch, seq, ctx):
        self.batch, self.seq, self.ctx = batch, seq, ctx
        self.n_lat = batch * seq
        self.n_tok = batch * (seq + ctx)
        for t in (TM, SSD_TM, SSD_L):
            assert seq % t == 0 and (batch * ctx) % t == 0, (seq, ctx, t)
        assert ctx == SSD_TM and ctx % SSD_L == 0
        assert seq % min(ATT_TQ, seq) == 0 and seq % min(ATT_TK, seq) == 0
        assert batch * 2 == SUBLANES and seq % GRID_W == 0

    def sample_of_tile(self, i, tile):
        return jnp.where(i < self.n_lat // tile, i // (self.seq // tile), self.batch)


def _post_norm_emit(h, upd, lng, lnb, nsh, nsc, ho_ref, hmo_ref):
    t = DN_ALPHA * h + upd
    mu = jnp.mean(t, axis=-1, keepdims=True)
    tc = t - mu
    var = jnp.mean(tc * tc, axis=-1, keepdims=True)
    hn = tc * lax.rsqrt(var + LN_EPS) * lng + lnb
    ho_ref[...] = hn
    hmo_ref[...] = (hn * (1.0 + nsc) + nsh).astype(hmo_ref.dtype)


def _mod_kernel(c_ref, w_ref, b_ref, o_ref):
    c = c_ref[...]
    s = (c * jax.nn.sigmoid(c)).astype(BF16)
    o_ref[...] = jnp.dot(s, w_ref[...].astype(BF16), preferred_element_type=F32) + b_ref[...]


def _mod_all(cc, w_mod, b_mod):
    depth, d, n = w_mod.shape
    return pl.pallas_call(
        _mod_kernel,
        grid=(depth, n // TN_MOD),
        in_specs=[pl.BlockSpec((MOD_ROWS, d), lambda l, j: (0, 0)),
                  pl.BlockSpec((None, d, TN_MOD), lambda l, j: (l, 0, j)),
                  pl.BlockSpec((None, 1, TN_MOD), lambda l, j: (l, 0, j))],
        out_specs=pl.BlockSpec((None, MOD_ROWS, TN_MOD), lambda l, j: (l, 0, j)),
        out_shape=jax.ShapeDtypeStruct((depth, MOD_ROWS, n), F32),
        compiler_params=_params("parallel", "parallel"),
        name="mod_matmul",
    )(cc, w_mod, b_mod.reshape(depth, 1, n))


def _modulate_kernel(h_ref, sh_ref, sc_ref, o_ref):
    o_ref[...] = (h_ref[...] * (1.0 + sc_ref[...]) + sh_ref[...]).astype(o_ref.dtype)


def _modulate(lay, h, sh, sc):
    vec = pl.BlockSpec((None, 1, D_MODEL), lambda i: (lay.sample_of_tile(i, TM), 0, 0))
    row = pl.BlockSpec((TM, D_MODEL), lambda i: (i, 0))
    return pl.pallas_call(
        _modulate_kernel, grid=(lay.n_tok // TM,),
        in_specs=[row, vec, vec], out_specs=row,
        out_shape=jax.ShapeDtypeStruct((lay.n_tok, D_MODEL), BF16),
        compiler_params=_params("parallel"), name="modulate",
    )(h, sh, sc)


def _ffn_kernel(hm_ref, h_ref, w1_ref, w3_ref, w2_ref, gate_ref, lng_ref, lnb_ref, nsh_ref, nsc_ref,
                ho_ref, hmo_ref, acc_ref):
    j = pl.program_id(1)

    @pl.when(j == 0)
    def _():
        acc_ref[...] = jnp.zeros(acc_ref.shape, F32)

    hm = hm_ref[...]
    a = jnp.dot(hm, w1_ref[...], preferred_element_type=F32)
    b = jnp.dot(hm, w3_ref[...], preferred_element_type=F32)
    p = (a * jax.nn.sigmoid(a) * b).astype(BF16)
    acc_ref[...] += jnp.dot(p, w2_ref[...], preferred_element_type=F32)

    @pl.when(j == pl.num_programs(1) - 1)
    def _():
        upd = (FFN_HALF * gate_ref[...]) * acc_ref[...]
        _post_norm_emit(h_ref[...], upd, lng_ref[...], lnb_ref[...], nsh_ref[...], nsc_ref[...],
                        ho_ref, hmo_ref)


def _half_ffn(lay, rows, hm, h, w1, w3, w2, gate, lng, lnb, nsh, nsc):
    vec = pl.BlockSpec((None, 1, D_MODEL), lambda i, j: (lay.sample_of_tile(i, TM), 0, 0))
    par = pl.BlockSpec((1, D_MODEL), lambda i, j: (0, 0))
    row = pl.BlockSpec((TM, D_MODEL), lambda i, j: (i, 0))
    return pl.pallas_call(
        _ffn_kernel,
        grid=(rows // TM, D_FF // TN_FF),
        in_specs=[row, row,
                  pl.BlockSpec((D_MODEL, TN_FF), lambda i, j: (0, j)),
                  pl.BlockSpec((D_MODEL, TN_FF), lambda i, j: (0, j)),
                  pl.BlockSpec((TN_FF, D_MODEL), lambda i, j: (j, 0)),
                  vec, par, par, vec, vec],
        out_specs=[row, row],
        out_shape=[jax.ShapeDtypeStruct((rows, D_MODEL), F32),
                   jax.ShapeDtypeStruct((rows, D_MODEL), BF16)],
        scratch_shapes=[pltpu.VMEM((TM, D_MODEL), F32)],
        compiler_params=_params("parallel", "arbitrary"), name="half_ffn",
    )(hm, h, w1, w3, w2, gate, lng, lnb, nsh, nsc)


def _proj_kernel(x_ref, w_ref, o_ref):
    o_ref[...] = jnp.dot(x_ref[...], w_ref[...], preferred_element_type=F32).astype(o_ref.dtype)


def _proj(lay, hm, w, out_dtype, name):
    n = w.shape[1]
    tn = min(TN_PROJ, n)
    return pl.pallas_call(
        _proj_kernel, grid=(lay.n_tok // TM, n // tn),
        in_specs=[pl.BlockSpec((TM, D_MODEL), lambda i, j: (i, 0)),
                  pl.BlockSpec((D_MODEL, tn), lambda i, j: (0, j))],
        out_specs=pl.BlockSpec((TM, tn), lambda i, j: (i, j)),
        out_shape=jax.ShapeDtypeStruct((lay.n_tok, n), out_dtype),
        compiler_params=_params("parallel", "parallel"), name=name,
    )(hm, w)


def _proj_qk_kernel(x_ref, w_ref, cos_ref, sin_ref, o_ref, *, n_lat_tiles, is_q):
    i = pl.program_id(0)
    acc = jnp.dot(x_ref[...], w_ref[...], preferred_element_type=F32)
    lane = lax.broadcasted_iota(jnp.int32, (acc.shape[0], LANES), 1)
    low_rows = lax.broadcasted_iota(jnp.int32, (LANES, acc.shape[0]), 0) < ATT_DH

    def emit(rotate):
        if rotate:
            cos, sin = cos_ref[...], sin_ref[...]
            first = (lane % (2 * ROPE_FREQS)) < ROPE_FREQS
        for h in range(acc.shape[1] // LANES):
            t = acc[:, h * LANES:(h + 1) * LANES]
            if rotate:
                partner = jnp.where(first, pltpu.roll(t, LANES - ROPE_FREQS, 1), pltpu.roll(t, ROPE_FREQS, 1))
                t = t * cos + partner * sin
            if is_q:
                tt = (t * (ATT_DH ** -0.5 * LOG2_E)).T
                o_ref[h, 0] = jnp.where(low_rows, tt, 0.0).astype(o_ref.dtype)
                o_ref[h, 1] = jnp.where(low_rows, 0.0, tt).astype(o_ref.dtype)
            else:
                o_ref[:, h * LANES:(h + 1) * LANES] = t.astype(o_ref.dtype)

    @pl.when(i < n_lat_tiles)
    def _():
        emit(True)

    @pl.when(i >= n_lat_tiles)
    def _():
        emit(False)


def _rope_tables(seq):
    rows = seq // GRID_W
    row = jnp.repeat(jnp.arange(rows), GRID_W)
    col = jnp.tile(jnp.arange(GRID_W), rows)
    inv = ROPE_BASE ** (-jnp.arange(ROPE_FREQS, dtype=F32) / ROPE_FREQS)
    ar, ac = row[:, None] * inv, col[:, None] * inv
    cos = jnp.concatenate([jnp.cos(ar), jnp.cos(ar), jnp.cos(ac), jnp.cos(ac)], axis=1)
    sin = jnp.concatenate([-jnp.sin(ar), jnp.sin(ar), -jnp.sin(ac), jnp.sin(ac)], axis=1)
    return jnp.tile(cos, (1, 2)), jnp.tile(sin, (1, 2))


def _proj_qk(lay, hm, w, cos, sin, is_q):
    tps = lay.seq // TM
    kern = functools.partial(_proj_qk_kernel, n_lat_tiles=lay.n_lat // TM, is_q=is_q)
    tab = pl.BlockSpec((TM, LANES), lambda i: (i % tps, 0))
    if is_q:
        out_spec = pl.BlockSpec((ATT_HEADS, 2, LANES, TM), lambda i: (0, 0, 0, i))
        out_shape = jax.ShapeDtypeStruct((ATT_HEADS, 2, LANES, lay.n_tok), BF16)
    else:
        out_spec = pl.BlockSpec((TM, BRANCH_W), lambda i: (i, 0))
        out_shape = jax.ShapeDtypeStruct((lay.n_tok, BRANCH_W), BF16)
    return pl.pallas_call(
        kern, grid=(lay.n_tok // TM,),
        in_specs=[pl.BlockSpec((TM, D_MODEL), lambda i: (i, 0)),
                  pl.BlockSpec((D_MODEL, BRANCH_W), lambda i: (0, 0)), tab, tab],
        out_specs=out_spec, out_shape=out_shape,
        compiler_params=_params("parallel"), name="proj_q" if is_q else "proj_k",
    )(hm, w, cos, sin)


def _proj_v_kernel(x_ref, w_ref, o_ref):
    acc = jnp.dot(x_ref[...], w_ref[...], preferred_element_type=F32)
    tail = ATT_V_ROWS - ATT_DV
    ones_row = jnp.where(lax.broadcasted_iota(jnp.int32, (tail, acc.shape[0]), 0) == 0, 1.0, 0.0)
    for h in range(acc.shape[1] // LANES):
        o_ref[h, :ATT_DV, :] = acc[:, h * LANES:(h + 1) * LANES].T.astype(o_ref.dtype)
        o_ref[h, ATT_DV:, :] = ones_row.astype(o_ref.dtype)


def _proj_v(lay, hm, w):
    return pl.pallas_call(
        _proj_v_kernel, grid=(lay.n_tok // TM,),
        in_specs=[pl.BlockSpec((TM, D_MODEL), lambda i: (i, 0)),
                  pl.BlockSpec((D_MODEL, BRANCH_W), lambda i: (0, 0))],
        out_specs=pl.BlockSpec((ATT_HEADS, ATT_V_ROWS, TM), lambda i: (0, 0, i)),
        out_shape=jax.ShapeDtypeStruct((ATT_HEADS, ATT_V_ROWS, lay.n_tok), BF16),
        compiler_params=_params("parallel"), name="proj_v",
    )(hm, w)


def _attn_kernel(lamv_ref, subln_ref, q0_ref, q1_ref, kc_ref, vc_ref, *rest, n_lat_chunks, tk, lam_init):
    if n_lat_chunks:
        kl_ref, vl_ref, o_ref = rest
    else:
        _, o_ref = rest
    lv = lamv_ref[...]
    lam = (jnp.exp(jnp.sum(lv[0:1] * lv[1:2], axis=-1, keepdims=True))
           - jnp.exp(jnp.sum(lv[2:3] * lv[3:4], axis=-1, keepdims=True)) + lam_init)

    def first(qt, k, vt):
        s = jnp.dot(k, qt, preferred_element_type=F32)
        m = jnp.max(s, axis=0, keepdims=True)
        return m, jnp.dot(vt, jnp.exp2(s - m).astype(BF16), preferred_element_type=F32)

    def update(state, qt, k, vt):
        m, acc = state
        s = jnp.dot(k, qt, preferred_element_type=F32)
        m_new = jnp.maximum(m, jnp.max(s, axis=0, keepdims=True))
        e = jnp.exp2(s - m_new).astype(BF16)
        return m_new, jnp.exp2(m - m_new) * acc + jnp.dot(vt, e, preferred_element_type=F32)

    qts = (q0_ref[...], q1_ref[...])
    kc, vtc = kc_ref[...], vc_ref[...]
    states = tuple(first(qt, kc, vtc) for qt in qts)
    if n_lat_chunks:
        def body(c, st):
            off = pl.multiple_of(c * tk, tk)
            k, vt = kl_ref[pl.ds(off, tk), :], vl_ref[:, pl.ds(off, tk)]
            return tuple(update(s_, qt, k, vt) for s_, qt in zip(st, qts))

        states = lax.fori_loop(0, n_lat_chunks, body, states)
    outs = [a[:ATT_DV] / a[ATT_DV:ATT_DV + 1] for _, a in states]
    o = outs[0] - lam * outs[1]
    o = o * lax.rsqrt(jnp.mean(o * o, axis=0, keepdims=True) + RMS_EPS) * subln_ref[...] * (1.0 - lam_init)
    o_ref[...] = o.T.astype(o_ref.dtype)


def _diff_attention(lay, qt, k, vt, att_lam, subln, lam_init):
    b_, s_, c_ = lay.batch, lay.seq, lay.ctx
    lamv = pl.BlockSpec((4, ATT_DH), lambda *_: (0, 0))
    sub = pl.BlockSpec((ATT_DV, 1), lambda *_: (0, 0))
    out_shape = jax.ShapeDtypeStruct((lay.n_tok, BRANCH_W), BF16)
    ctx_blk = lay.n_lat // c_

    def specs(tq, q_blk):
        return [lamv, sub,
                pl.BlockSpec((None, None, LANES, tq), lambda b, h, i: (h, 0, 0, q_blk(b, i))),
                pl.BlockSpec((None, None, LANES, tq), lambda b, h, i: (h, 1, 0, q_blk(b, i))),
                pl.BlockSpec((c_, LANES), lambda b, h, i: (ctx_blk + b, h)),
                pl.BlockSpec((None, ATT_V_ROWS, c_), lambda b, h, i: (h, 0, ctx_blk + b))]

    tq, tk = min(ATT_TQ, s_), min(ATT_TK, s_)
    n_q = s_ // tq
    lat_blk = lambda b, i: b * n_q + i
    lat = pl.pallas_call(
        functools.partial(_attn_kernel, n_lat_chunks=s_ // tk, tk=tk, lam_init=lam_init),
        grid=(b_, ATT_HEADS, n_q),
        in_specs=specs(tq, lat_blk) + [
            pl.BlockSpec((s_, LANES), lambda b, h, i: (b, h)),
            pl.BlockSpec((None, ATT_V_ROWS, s_), lambda b, h, i: (h, 0, b))],
        out_specs=pl.BlockSpec((tq, LANES), lambda b, h, i: (lat_blk(b, i), h)),
        out_shape=out_shape,
        compiler_params=_params("parallel", "parallel", "arbitrary"), name="attn_latent",
    )(att_lam, subln, qt, qt, k, vt, k, vt)
    ctx_q = lambda b, i: ctx_blk + b
    return pl.pallas_call(
        functools.partial(_attn_kernel, n_lat_chunks=0, tk=0, lam_init=lam_init),
        grid=(b_, ATT_HEADS, 1),
        in_specs=specs(c_, ctx_q) + [pl.BlockSpec(memory_space=pl.ANY)],
        out_specs=pl.BlockSpec((c_, LANES), lambda b, h, i: (ctx_q(b, i), h)),
        out_shape=out_shape, input_output_aliases={6: 0},
        compiler_params=_params("parallel", "parallel", "arbitrary"), name="attn_context",
    )(att_lam, subln, qt, qt, k, vt, lat)


def _ssd_prep_kernel(prev_ref, cur_ref, next_ref, dt_ref, cw_ref, cb_ref, dtb_ref,
                     xs_ref, bm_ref, cm_ref, dts_ref, dtst_ref, ext_ref, *, tiles_per_seq, n_lat_tiles):
    i = pl.program_id(0)
    is_ctx = i >= n_lat_tiles
    first = jnp.logical_or(is_ctx, i % tiles_per_seq == 0)
    last = jnp.logical_or(is_ctx, i % tiles_per_seq == tiles_per_seq - 1)
    tm = cur_ref.shape[0]
    ext_ref[0:SUBLANES, :] = jnp.where(first, 0.0, prev_ref[...])
    ext_ref[SUBLANES:SUBLANES + tm, :] = cur_ref[...]
    ext_ref[SUBLANES + tm:, :] = jnp.where(last, 0.0, next_ref[...])
    acc = jnp.zeros(cur_ref.shape, F32) + cb_ref[...]
    for k in range(SSD_CONV):
        start = SUBLANES + k - SSD_CONV // 2
        acc = acc + ext_ref[start:start + tm, :] * cw_ref[k:k + 1, :]
    act = acc * jax.nn.sigmoid(acc)
    xs_ref[...] = act[:, :SSD_INNER]
    bm_ref[...] = act[:, SSD_INNER:SSD_INNER + SSD_BC_W].astype(bm_ref.dtype)
    cm_ref[...] = act[:, SSD_INNER + SSD_BC_W:].astype(cm_ref.dtype)
    x = dt_ref[...] + dtb_ref[...]
    sp = jnp.maximum(x, 0.0) + jnp.log1p(jnp.exp(-jnp.abs(x)))
    dts_ref[0] = sp
    dts_ref[1] = pltpu.roll(sp, DT_PAD - SSD_HEADS, 1)
    spt = sp.T
    dtst_ref[0] = spt[0:SSD_HEADS]
    dtst_ref[1] = spt[SSD_HEADS:2 * SSD_HEADS]


def _ssd_prep(lay, xbc, dt, conv_w, conv_b, dt_bias):
    n, tm = lay.n_tok, SSD_TM
    sub_per_tile = tm // SUBLANES
    n_sub = n // SUBLANES
    kern = functools.partial(_ssd_prep_kernel, tiles_per_seq=lay.seq // tm, n_lat_tiles=lay.n_lat // tm)
    row = lambda w: pl.BlockSpec((tm, w), lambda i: (i, 0))
    return pl.pallas_call(
        kern, grid=(n // tm,),
        in_specs=[pl.BlockSpec((SUBLANES, SSD_XBC_W), lambda i: (jnp.maximum(i * sub_per_tile - 1, 0), 0)),
                  row(SSD_XBC_W),
                  pl.BlockSpec((SUBLANES, SSD_XBC_W),
                               lambda i: (jnp.minimum((i + 1) * sub_per_tile, n_sub - 1), 0)),
                  row(DT_PAD),
                  pl.BlockSpec((SUBLANES, SSD_XBC_W), lambda i: (0, 0)),
                  pl.BlockSpec((1, SSD_XBC_W), lambda i: (0, 0)),
                  pl.BlockSpec((1, DT_PAD), lambda i: (0, 0))],
        out_specs=[row(SSD_INNER), row(SSD_BC_W), row(SSD_BC_W),
                   pl.BlockSpec((2, tm, DT_PAD), lambda i: (0, i, 0)),
                   pl.BlockSpec((2, SSD_HEADS, tm), lambda i: (0, 0, i))],
        out_shape=[jax.ShapeDtypeStruct((n, SSD_INNER), F32),
                   jax.ShapeDtypeStruct((n, SSD_BC_W), BF16),
                   jax.ShapeDtypeStruct((n, SSD_BC_W), BF16),
                   jax.ShapeDtypeStruct((2, n, DT_PAD), F32),
                   jax.ShapeDtypeStruct((2, SSD_HEADS, n), F32)],
        scratch_shapes=[pltpu.VMEM((tm + 2 * SUBLANES, SSD_XBC_W), F32)],
        compiler_params=_params("parallel"), name="ssd_prep",
    )(xbc, xbc, xbc, dt, conv_w, conv_b, dt_bias)


def _split_dot(a, x, x_is_lhs=False):
    out = None
    r = x
    for _ in range(3):
        t = r.astype(BF16)
        r = r - t.astype(F32)
        d = (jnp.dot(t, a, preferred_element_type=F32) if x_is_lhs
             else jnp.dot(a, t, preferred_element_type=F32))
        out = d if out is None else out + d
    return out


def _ssd_kernel(xs_ref, bm_ref, cm_ref, dts_ref, dtst_ref, arow_ref, acol_ref, y_ref, h_ref):
    sign = 1 - 2 * pl.program_id(1)

    @pl.when(pl.program_id(2) == 0)
    def _():
        h_ref[...] = jnp.zeros(h_ref.shape, F32)

    ll = xs_ref.shape[0]
    r = lax.broadcasted_iota(jnp.int32, (ll, ll), 0)
    c = lax.broadcasted_iota(jnp.int32, (ll, ll), 1)
    mask = (r - c) * sign >= 0
    mask_t = (r - c) * sign <= 0
    one_hot = lambda m: jnp.where(m, 1.0, 0.0).astype(BF16)
    dts = dts_ref[...]
    da = dts * arow_ref[...]
    da_t = dtst_ref[...] * acol_ref[...]
    cs = _split_dot(one_hot(mask), da)
    cs_t = _split_dot(one_hot(mask_t), da_t, x_is_lhs=True)
    tot = jnp.sum(da, axis=0, keepdims=True)
    to_end = jnp.exp(tot - cs)
    from_start = jnp.exp(cs)
    chunk_decay = jnp.exp(tot)
    lane = lax.broadcasted_iota(jnp.int32, (ll, LANES), 1)
    lo_half = lane < SSD_P

    def pair_expand(t, hd):
        rows = t.shape[0]
        return jnp.where(lo_half[:rows], jnp.broadcast_to(t[:, hd:hd + 1], (rows, LANES)),
                         jnp.broadcast_to(t[:, hd + 1:hd + 2], (rows, LANES)))

    gw = SSD_HPG * SSD_P
    for g in range(SSD_GROUPS):
        bg = bm_ref[:, g * SSD_N:(g + 1) * SSD_N]
        cg = cm_ref[:, g * SSD_N:(g + 1) * SSD_N]
        cb = lax.dot_general(cg, bg, (((1,), (1,)), ((), ())), preferred_element_type=F32)
        y_parts, xw_parts, fs_parts, cd_parts = [], [], [], []
        for pr in range(SSD_HPG // 2):
            hd = g * SSD_HPG + 2 * pr
            xdt = xs_ref[:, hd * SSD_P:(hd + 2) * SSD_P] * pair_expand(dts, hd)
            xw_parts.append((xdt * pair_expand(to_end, hd)).astype(BF16))
            fs_parts.append(pair_expand(from_start, hd))
            cd_parts.append(pair_expand(chunk_decay, hd))
            yp = None
            for k in range(2):
                seg = cs[:, hd + k:hd + k + 1] - cs_t[hd + k:hd + k + 1, :]
                dec = jnp.where(mask, jnp.exp(jnp.where(mask, seg, 0.0)), 0.0)
                w = (cb * dec).astype(BF16)
                xk = jnp.where(lo_half if k == 0 else jnp.logical_not(lo_half), xdt, 0.0).astype(BF16)
                d = jnp.dot(w, xk, preferred_element_type=F32)
                yp = d if yp is None else yp + d
            y_parts.append(yp)
        xw = jnp.concatenate(xw_parts, axis=1)
        states = lax.dot_general(bg, xw, (((0,), (0,)), ((), ())), preferred_element_type=F32)
        h_prev = h_ref[g]
        y_off = jnp.dot(cg, h_prev.astype(BF16), preferred_element_type=F32) * jnp.concatenate(fs_parts, axis=1)
        y_ref[:, g * gw:(g + 1) * gw] = jnp.concatenate(y_parts, axis=1) + y_off
        h_ref[g] = h_prev * jnp.concatenate(cd_parts, axis=1) + states


def _ssd_scan(lay, xs, bm, cm, dts, dtst, a_row, a_col):
    ll = SSD_L
    n_ctx, n_lat = lay.ctx // ll, lay.seq // ll
    ctx_base = lay.n_lat // ll

    def rb(b, d, s):
        cstep = jnp.where(d == 0, s, n_ctx - 1 - s)
        lstep = jnp.where(d == 0, s - n_ctx, n_lat - 1 - (s - n_ctx))
        return jnp.where(s < n_ctx, ctx_base + b * n_ctx + cstep, b * n_lat + lstep)

    row = lambda w: pl.BlockSpec((ll, w), lambda b, d, s: (rb(b, d, s), 0))
    return pl.pallas_call(
        _ssd_kernel, grid=(lay.batch, 2, n_ctx + n_lat),
        in_specs=[row(SSD_INNER), row(SSD_BC_W), row(SSD_BC_W),
                  pl.BlockSpec((None, ll, DT_PAD), lambda b, d, s: (d, rb(b, d, s), 0)),
                  pl.BlockSpec((None, SSD_HEADS, ll), lambda b, d, s: (d, 0, rb(b, d, s))),
                  pl.BlockSpec((None, 1, DT_PAD), lambda b, d, s: (d, 0, 0)),
                  pl.BlockSpec((None, SSD_HEADS, 1), lambda b, d, s: (d, 0, 0))],
        out_specs=pl.BlockSpec((None, ll, SSD_INNER), lambda b, d, s: (d, rb(b, d, s), 0)),
        out_shape=jax.ShapeDtypeStruct((2, lay.n_tok, SSD_INNER), F32),
        scratch_shapes=[pltpu.VMEM((SSD_GROUPS, SSD_N, SSD_HPG * SSD_P), F32)],
        compiler_params=_params("parallel", "parallel", "arbitrary"), name="ssd_scan",
    )(xs, bm, cm, dts, dtst, a_row, a_col)


def _ssd_gate_norm_kernel(y0_ref, y1_ref, xs_ref, z_ref, dsk_ref, nw_ref, o_ref):
    z = z_ref[...]
    gated = (y0_ref[...] + y1_ref[...] + xs_ref[...] * dsk_ref[...]) * (z * jax.nn.sigmoid(z))
    nw = nw_ref[...]
    for g in range(SSD_GROUPS):
        sl = slice(g * SSD_NORM_GROUP, (g + 1) * SSD_NORM_GROUP)
        t = gated[:, sl]
        o_ref[:, sl] = (t * lax.rsqrt(jnp.mean(t * t, axis=-1, keepdims=True) + RMS_EPS)
                        * nw[:, sl]).astype(o_ref.dtype)


def _ssd_gate_norm(lay, ydir, xs, z, dsk, nw):
    tm = SSD_TM
    row = pl.BlockSpec((tm, SSD_INNER), lambda i: (i, 0))
    par = pl.BlockSpec((1, SSD_INNER), lambda i: (0, 0))
    return pl.pallas_call(
        _ssd_gate_norm_kernel, grid=(lay.n_tok // tm,),
        in_specs=[pl.BlockSpec((None, tm, SSD_INNER), lambda i: (0, i, 0)),
                  pl.BlockSpec((None, tm, SSD_INNER), lambda i: (1, i, 0)), row, row, par, par],
        out_specs=row, out_shape=jax.ShapeDtypeStruct((lay.n_tok, SSD_INNER), BF16),
        compiler_params=_params("parallel"), name="ssd_gate_norm",
    )(ydir, ydir, xs, z, dsk, nw)


def _s5_operators(lam_re, lam_im, log_step, b_re, b_im, c_re, c_im):
    hp = lax.Precision.HIGHEST
    ll, hh = S5_L, S5_GROUP_CH
    step = jnp.exp(log_step)[..., None, None]
    d = jnp.arange(ll + 1, dtype=F32)
    p_mag = jnp.exp(lam_re[..., None] * step * d)
    p_ang = lam_im[..., None] * step * d
    p_re, p_im = p_mag * jnp.cos(p_ang), p_mag * jnp.sin(p_ang)
    ab_re, ab_im = p_re[..., 1], p_im[..., 1]
    den = lam_re * lam_re + lam_im * lam_im
    k_re = ((ab_re - 1.0) * lam_re + ab_im * lam_im) / den
    k_im = (ab_im * lam_re - (ab_re - 1.0) * lam_im) / den
    bb_re = k_re[..., None] * b_re - k_im[..., None] * b_im
    bb_im = k_re[..., None] * b_im + k_im[..., None] * b_re
    cp_re = c_re[..., None] * p_re[:, :, None] - c_im[..., None] * p_im[:, :, None]
    cp_im = c_re[..., None] * p_im[:, :, None] + c_im[..., None] * p_re[:, :, None]
    kern = (jnp.einsum('zghnd,zgnk->zgdhk', cp_re, bb_re, precision=hp)
            - jnp.einsum('zghnd,zgnk->zgdhk', cp_im, bb_im, precision=hp))
    s_idx = jnp.arange(ll)[:, None]
    l_idx = jnp.arange(ll)[None, :]
    t_f = jnp.where((l_idx >= s_idx)[None, :, :, None, None], kern[0][:, jnp.clip(l_idx - s_idx, 0, ll)], 0.0)
    t_b = jnp.where((s_idx >= l_idx)[None, :, :, None, None], kern[1][:, jnp.clip(s_idx - l_idx, 0, ll)], 0.0)
    toep = (t_f + t_b).transpose(0, 1, 4, 2, 3).reshape(S5_GROUPS, S5_CW, S5_CW)

    def state_in(z, powers):
        pr, pi = p_re[z][..., powers], p_im[z][..., powers]
        re = pr[..., None] * bb_re[z][:, :, None] - pi[..., None] * bb_im[z][:, :, None]
        im = pr[..., None] * bb_im[z][:, :, None] + pi[..., None] * bb_re[z][:, :, None]
        re = re.transpose(0, 2, 3, 1).reshape(S5_GROUPS, S5_CW, S5_N)
        im = im.transpose(0, 2, 3, 1).reshape(S5_GROUPS, S5_CW, S5_N)
        return jnp.concatenate([re, im], axis=-1)

    ws_f = state_in(0, ll - 1 - jnp.arange(ll))
    ws_b = state_in(1, jnp.arange(ll))
    w1 = jnp.concatenate([toep, ws_f, ws_b], axis=-1)

    def state_out(z, powers):
        re = cp_re[z][..., powers].transpose(0, 2, 3, 1).reshape(S5_GROUPS, S5_N, S5_CW)
        im = cp_im[z][..., powers].transpose(0, 2, 3, 1).reshape(S5_GROUPS, S5_N, S5_CW)
        return jnp.concatenate([re, -im], axis=1)

    wo = jnp.concatenate([state_out(0, jnp.arange(ll) + 1), state_out(1, ll - jnp.arange(ll))], axis=1)
    ar, ai = p_re[..., ll], p_im[..., ll]
    a1 = jnp.concatenate([ar, ar], axis=-1)
    a2 = jnp.concatenate([-ai, ai], axis=-1)
    zeros = jnp.zeros_like(a1[0])
    av = jnp.stack([a1[0], a2[0], a1[1], a2[1], zeros, zeros, zeros, zeros], axis=1)
    return w1.astype(BF16), wo.astype(BF16), av


def _s5_kernel(u_ref, w1_ref, wo_ref, av_ref, y_ref, sf_ref, sfs_ref, sb_ref, sbs_ref, hf_ref, hb_ref,
               *, n_ctx_tiles, n_tiles):
    p = jnp.dot(u_ref[...], w1_ref[...], preferred_element_type=F32)
    nst = 2 * S5_N
    sf = p[:, S5_CW:S5_CW + nst]
    sb = p[:, S5_CW + nst:]
    sf_ref[...] = sf
    sfs_ref[...] = pltpu.roll(sf, S5_N, 1)
    sb_ref[...] = sb
    sbs_ref[...] = pltpu.roll(sb, S5_N, 1)
    av = av_ref[...]
    half = SUBLANES // 2
    a1f, a2f = (jnp.broadcast_to(av[k:k + 1], (half, nst)) for k in (0, 1))
    a1b, a2b = (jnp.broadcast_to(av[k:k + 1], (half, nst)) for k in (2, 3))

    def body(j, carry):
        hf, hfs, hb, hbs = carry
        of = pl.multiple_of(j * SUBLANES, SUBLANES)
        s, ss = sf_ref[pl.ds(of, SUBLANES), :], sfs_ref[pl.ds(of, SUBLANES), :]
        h1 = a1f * hf + a2f * hfs + s[:half]
        h1s = a1f * hfs - a2f * hf + ss[:half]
        hf_ref[pl.ds(of, SUBLANES), :] = jnp.concatenate([hf, h1], axis=0)
        h2 = a1f * h1 + a2f * h1s + s[half:]
        h2s = a1f * h1s - a2f * h1 + ss[half:]
        jb = jnp.where(j < n_ctx_tiles, n_ctx_tiles - 1 - j, n_tiles - 1 - (j - n_ctx_tiles))
        ob = pl.multiple_of(jb * SUBLANES, SUBLANES)
        s, ss = sb_ref[pl.ds(ob, SUBLANES), :], sbs_ref[pl.ds(ob, SUBLANES), :]
        g1 = a1b * hb + a2b * hbs + s[half:]
        g1s = a1b * hbs - a2b * hb + ss[half:]
        hb_ref[pl.ds(ob, SUBLANES), :] = jnp.concatenate([g1, hb], axis=0)
        g2 = a1b * g1 + a2b * g1s + s[:half]
        g2s = a1b * g1s - a2b * g1 + ss[:half]
        return h2, h2s, g2, g2s

    z = jnp.zeros((half, nst), F32)
    lax.fori_loop(0, n_tiles, body, (z, z, z, z))
    wo = wo_ref[...]
    y_ref[...] = (p[:, :S5_CW]
                  + jnp.dot(hf_ref[...].astype(BF16), wo[:nst], preferred_element_type=F32)
                  + jnp.dot(hb_ref[...].astype(BF16), wo[nst:], preferred_element_type=F32))


def _s5_chunked(lay, uc, w1, wo, av):
    rows = lay.n_tok // S5_L
    n_ctx_rows = lay.batch * lay.ctx // S5_L
    kern = functools.partial(_s5_kernel, n_ctx_tiles=n_ctx_rows // SUBLANES, n_tiles=rows // SUBLANES)
    st = pltpu.VMEM((rows, 2 * S5_N), F32)
    return pl.pallas_call(
        kern, grid=(S5_GROUPS,),
        in_specs=[pl.BlockSpec((None, rows, S5_CW), lambda g: (g, 0, 0)),
                  pl.BlockSpec((None, S5_CW, S5_CW + 4 * S5_N), lambda g: (g, 0, 0)),
                  pl.BlockSpec((None, 4 * S5_N, S5_CW), lambda g: (g, 0, 0)),
                  pl.BlockSpec((None, SUBLANES, 2 * S5_N), lambda g: (g, 0, 0))],
        out_specs=pl.BlockSpec((None, rows, S5_CW), lambda g: (g, 0, 0)),
        out_shape=jax.ShapeDtypeStruct((S5_GROUPS, rows, S5_CW), F32),
        scratch_shapes=[st, st, st, st, st, st],
        compiler_params=_params("parallel"), name="s5_chunked",
    )(uc, w1, wo, av)


def _to_chunk_rows(lay, u):
    def part(t, length):
        t = t.reshape(lay.batch, length // S5_L, S5_L, S5_GROUPS, S5_GROUP_CH)
        return t.transpose(3, 1, 0, 2, 4).reshape(S5_GROUPS, (length // S5_L) * lay.batch, S5_CW)
    return jnp.concatenate([part(u[lay.n_lat:], lay.ctx), part(u[:lay.n_lat], lay.seq)], axis=1)


def _from_chunk_rows(lay, y):
    def part(t, length):
        t = t.reshape(S5_GROUPS, length // S5_L, lay.batch, S5_L, S5_GROUP_CH)
        return t.transpose(2, 1, 3, 0, 4).reshape(lay.batch * length, S5_CH)
    n_ctx_rows = lay.batch * lay.ctx // S5_L
    return jnp.concatenate([part(y[:, n_ctx_rows:], lay.seq), part(y[:, :n_ctx_rows], lay.ctx)], axis=0)


def _s5_glu_kernel(ys_ref, u_ref, dsk_ref, w_ref, b_ref, o_ref):
    t = ys_ref[...] + u_ref[...] * dsk_ref[...]
    t = 0.5 * t * (1.0 + jnp.tanh(math.sqrt(2.0 / math.pi) * (t + 0.044715 * (t * t * t))))
    gate = jnp.dot(t.astype(BF16), w_ref[...], preferred_element_type=F32) + b_ref[...]
    o_ref[...] = (t * jax.nn.sigmoid(gate)).astype(o_ref.dtype)


def _s5_glu(lay, ys, u, dsk, w, b):
    row = pl.BlockSpec((TM, S5_CH), lambda i: (i, 0))
    par = pl.BlockSpec((1, S5_CH), lambda i: (0, 0))
    return pl.pallas_call(
        _s5_glu_kernel, grid=(lay.n_tok // TM,),
        in_specs=[row, row, par, pl.BlockSpec((S5_CH, S5_CH), lambda i: (0, 0)), par],
        out_specs=row, out_shape=jax.ShapeDtypeStruct((lay.n_tok, S5_CH), BF16),
        compiler_params=_params("parallel"), name="s5_glu",
    )(ys, u, dsk, w, b)


def _merge_kernel(oa_ref, os_ref, o5_ref, ga_ref, gs_ref, g5_ref, wa_ref, ws_ref, w5_ref, o_ref):
    acc = None
    for o, g, w in ((oa_ref, ga_ref, wa_ref), (os_ref, gs_ref, ws_ref), (o5_ref, g5_ref, w5_ref)):
        t = jax.nn.sigmoid(g[...]) * jnp.dot(o[...], w[...], preferred_element_type=F32)
        acc = t if acc is None else acc + t
    o_ref[...] = acc.astype(o_ref.dtype)


def _merge(rows, o_att, o_ssd, o_s5, g, w_branch):
    nt = D_MODEL // TN_MERGE
    row = pl.BlockSpec((TM, BRANCH_W), lambda i, j: (i, 0))
    gate = lambda k: pl.BlockSpec((TM, TN_MERGE), lambda i, j: (i, k * nt + j))
    wb = lambda k: pl.BlockSpec((None, BRANCH_W, TN_MERGE), lambda i, j: (k, 0, j))
    return pl.pallas_call(
        _merge_kernel, grid=(rows // TM, nt),
        in_specs=[row, row, row, gate(0), gate(1), gate(2), wb(0), wb(1), wb(2)],
        out_specs=pl.BlockSpec((TM, TN_MERGE), lambda i, j: (i, j)),
        out_shape=jax.ShapeDtypeStruct((rows, D_MODEL), BF16),
        compiler_params=_params("parallel", "parallel"), name="branch_merge",
    )(o_att, o_ssd, o_s5, g, g, g, w_branch, w_branch, w_branch)


def _out_norm_kernel(mx_ref, h_ref, w_ref, gate_ref, lng_ref, lnb_ref, nsh_ref, nsc_ref, ho_ref, hmo_ref):
    y = jnp.dot(mx_ref[...], w_ref[...], preferred_element_type=F32)
    _post_norm_emit(h_ref[...], gate_ref[...] * y, lng_ref[...], lnb_ref[...], nsh_ref[...], nsc_ref[...],
                    ho_ref, hmo_ref)


def _out_norm(lay, rows, mixed, h, w_out, gate, lng, lnb, nsh, nsc):
    vec = pl.BlockSpec((None, 1, D_MODEL), lambda i: (lay.sample_of_tile(i, TM), 0, 0))
    par = pl.BlockSpec((1, D_MODEL), lambda i: (0, 0))
    row = pl.BlockSpec((TM, D_MODEL), lambda i: (i, 0))
    return pl.pallas_call(
        _out_norm_kernel, grid=(rows // TM,),
        in_specs=[row, row, pl.BlockSpec((D_MODEL, D_MODEL), lambda i: (0, 0)), vec, par, par, vec, vec],
        out_specs=[row, row],
        out_shape=[jax.ShapeDtypeStruct((rows, D_MODEL), F32),
                   jax.ShapeDtypeStruct((rows, D_MODEL), BF16)],
        compiler_params=_params("parallel"), name="out_norm",
    )(mixed, h, w_out, gate, lng, lnb, nsh, nsc)


def _pad_cols(t, width):
    return jnp.pad(t, [(0, 0)] * (t.ndim - 1) + [(0, width - t.shape[-1])])


def _token_mixer(lay, rows_out, hm, rope, lam_init, w_in, att_lam, att_subln, conv_w, conv_b, a_log, dt_bias,
                 ssd_d, ssd_norm, s5_ops, s5_d, glu_w, glu_b, w_branch):
    cuts = [0]
    for w in (BRANCH_W, BRANCH_W, BRANCH_W, SSD_INNER, SSD_XBC_W, 2 * SSD_HEADS, S5_CH, N_BRANCH * D_MODEL):
        cuts.append(cuts[-1] + w)
    w_q, w_k, w_v, w_z, w_xbc, w_dt, w_u, w_g = (
        w_in[:, a:b].astype(BF16) for a, b in zip(cuts[:-1], cuts[1:]))
    qt = _proj_qk(lay, hm, w_q, *rope, is_q=True)
    k = _proj_qk(lay, hm, w_k, *rope, is_q=False)
    vt = _proj_v(lay, hm, w_v)
    z = _proj(lay, hm, w_z, F32, "proj_z")
    xbc = _proj(lay, hm, w_xbc, F32, "proj_xbc")
    dt = _proj(lay, hm, _pad_cols(w_dt, DT_PAD), F32, "proj_dt")
    u = _proj(lay, hm, w_u, F32, "proj_u")
    g = _proj(lay, hm, w_g, F32, "proj_gate")

    o_att = _diff_attention(lay, qt, k, vt, att_lam, att_subln.reshape(ATT_DV, 1), lam_init)

    conv_w8 = jnp.pad(conv_w, ((0, SUBLANES - SSD_CONV), (0, 0)))
    xs, bm, cm, dts, dtst = _ssd_prep(lay, xbc, dt, conv_w8, conv_b.reshape(1, -1),
                                      _pad_cols(dt_bias.reshape(1, -1), DT_PAD))
    a = -jnp.exp(a_log.astype(F32))
    ydir = _ssd_scan(lay, xs, bm, cm, dts, dtst, _pad_cols(a, DT_PAD)[:, None, :], a[:, :, None])
    o_ssd = _ssd_gate_norm(lay, ydir, xs, z, jnp.repeat(ssd_d, SSD_P).reshape(1, -1),
                           ssd_norm.reshape(1, -1))

    ys = _s5_chunked(lay, _to_chunk_rows(lay, u.astype(BF16)), *s5_ops)
    o_s5 = _s5_glu(lay, _from_chunk_rows(lay, ys), u, s5_d.reshape(1, -1), glu_w.astype(BF16),
                   glu_b.reshape(1, -1))

    return _merge(rows_out, o_att, o_ssd, o_s5, g, w_branch.astype(BF16))


def _trunk(lay, x, c, ctx, c_ctx, w_mod, b_mod, ln_g, ln_b, ffn_w1, ffn_w3, ffn_w2, w_in,
           att_lam, att_subln, ssd_conv_w, ssd_conv_b, ssd_a_log, ssd_dt_bias, ssd_d, ssd_norm,
           s5_lam_re, s5_lam_im, s5_log_step, s5_b_re, s5_b_im, s5_c_re, s5_c_im,
           s5_d, s5_glu_w, s5_glu_b, w_branch, w_out):
    depth = w_mod.shape[0]
    h = jnp.concatenate([x.reshape(lay.n_lat, D_MODEL), ctx.reshape(-1, D_MODEL)], axis=0)
    cc = jnp.concatenate([c, c_ctx[None], jnp.zeros((MOD_ROWS - lay.batch - 1, D_MODEL), F32)], axis=0)
    mod = _mod_all(cc, w_mod, b_mod).reshape(depth, MOD_ROWS, N_MOD, 1, D_MODEL)
    mvec = lambda l, k: mod[l, :, k]
    zero_vec = jnp.zeros((MOD_ROWS, 1, D_MODEL), F32)
    rope = _rope_tables(lay.seq)
    lnp = lambda l, k: (ln_g[l, k].reshape(1, -1), ln_b[l, k].reshape(1, -1))

    hm = _modulate(lay, h, mvec(0, 0), mvec(0, 1))
    for l in range(depth):
        lam_init = LAMBDA_INIT_BASE - LAMBDA_INIT_SPAN * math.exp(-LAMBDA_INIT_RATE * l)
        last = l + 1 == depth
        rows = lay.n_lat if last else lay.n_tok
        h, hm = _half_ffn(lay, lay.n_tok, hm, h, ffn_w1[l, 0].astype(BF16), ffn_w3[l, 0].astype(BF16),
                          ffn_w2[l, 0].astype(BF16), mvec(l, 2), *lnp(l, 0), mvec(l, 3), mvec(l, 4))
        s5_ops = _s5_operators(s5_lam_re[l], s5_lam_im[l], s5_log_step[l], s5_b_re[l], s5_b_im[l],
                               s5_c_re[l], s5_c_im[l])
        mixed = _token_mixer(lay, rows, hm, rope, lam_init, w_in[l], att_lam[l], att_subln[l],
                             ssd_conv_w[l], ssd_conv_b[l], ssd_a_log[l], ssd_dt_bias[l], ssd_d[l],
                             ssd_norm[l], s5_ops, s5_d[l], s5_glu_w[l], s5_glu_b[l], w_branch[l])
        h, hm = _out_norm(lay, rows, mixed, h, w_out[l].astype(BF16), mvec(l, 5), *lnp(l, 1),
                          mvec(l, 6), mvec(l, 7))
        nxt = (zero_vec, zero_vec) if last else (mvec(l + 1, 0), mvec(l + 1, 1))
        h, hm = _half_ffn(lay, rows, hm, h, ffn_w1[l, 1].astype(BF16), ffn_w3[l, 1].astype(BF16),
                          ffn_w2[l, 1].astype(BF16), mvec(l, 8), *lnp(l, 2), *nxt)
    return h.reshape(x.shape)


def kernel(x, c, ctx, c_ctx, w_mod, b_mod, ln_g, ln_b, ffn_w1, ffn_w3, ffn_w2, w_in, att_lam, att_subln, ssd_conv_w, ssd_conv_b, ssd_a_log, ssd_dt_bias, ssd_d, ssd_norm, s5_lam_re, s5_lam_im, s5_log_step, s5_b_re, s5_b_im, s5_c_re, s5_c_im, s5_d, s5_glu_w, s5_glu_b, w_branch, w_out):
    lay = Layout(x.shape[0], x.shape[1], ctx.shape[1])
    return _trunk(lay, x, c, ctx, c_ctx, w_mod, b_mod, ln_g, ln_b, ffn_w1, ffn_w3, ffn_w2, w_in,
                  att_lam, att_subln, ssd_conv_w, ssd_conv_b, ssd_a_log, ssd_dt_bias, ssd_d, ssd_norm,
                  s5_lam_re, s5_lam_im, s5_log_step, s5_b_re, s5_b_im, s5_c_re, s5_c_im,
                  s5_d, s5_glu_w, s5_glu_b, w_branch, w_out)
```

```python
import functools
import math

import jax
import jax.numpy as jnp
from jax import lax
from jax.experimental import pallas as pl
from jax.experimental.pallas import tpu as pltpu

F32 = jnp.float32
BF16 = jnp.bfloat16
LOG2_E = math.log2(math.e)

D_MODEL = 2048
DEPTH = 2
GRID_W = 64
DN_ALPHA = (2 * DEPTH) ** 0.25
N_SUB = 3
N_MOD = 3 * N_SUB
FFN_HALF = 0.5
D_FF = 5632
LN_EPS = 1e-5
RMS_EPS = 1e-6
BRANCH_W = D_MODEL // 2
N_BRANCH = 3
ATT_DH = 64
ATT_DV = 2 * ATT_DH
ATT_HEADS = BRANCH_W // ATT_DV
ROPE_BASE = 10000.0
ROPE_FREQS = ATT_DH // 4
LAMBDA_INIT_BASE = 0.8
LAMBDA_INIT_SPAN = 0.6
LAMBDA_INIT_RATE = 0.3
SSD_P = 64
SSD_HEADS = BRANCH_W // SSD_P
SSD_GROUPS = 4
SSD_HPG = SSD_HEADS // SSD_GROUPS
SSD_N = 128
SSD_CONV = 5
SSD_INNER = SSD_HEADS * SSD_P
SSD_BC_W = SSD_GROUPS * SSD_N
SSD_XBC_W = SSD_INNER + 2 * SSD_BC_W
SSD_NORM_GROUP = SSD_INNER // SSD_GROUPS
S5_CH = BRANCH_W
S5_GROUP_CH = 16
S5_GROUPS = S5_CH // S5_GROUP_CH
S5_N = 64

LANES = 128
SUBLANES = 8
VMEM_LIMIT_BYTES = 56 * 1024 * 1024
MOD_ROWS = 8
TM = 512
TN_FF = 512
TN_PROJ = 1024
TN_MERGE = 512
TN_MOD = 1024
ATT_V_ROWS = ATT_DV + 16
ATT_TQ = 1024
ATT_TK = 1024
SSD_L = 128
SSD_TM = 256
S5_L = 16
S5_CW = S5_L * S5_GROUP_CH
S5_GB = LANES // S5_GROUP_CH
S5_NB = S5_GROUPS // S5_GB
S5_TM = 256
S5_ROW_TILES = 4
BF16_ROWS = 2 * SUBLANES
DT_PAD = LANES


def _params(*sem):
    return pltpu.CompilerParams(dimension_semantics=sem, vmem_limit_bytes=VMEM_LIMIT_BYTES)


class Layout:
    def __init__(self, batch, seq, ctx):
        self.batch, self.seq, self.ctx = batch, seq, ctx
        self.n_lat = batch * seq
        self.n_tok = batch * (seq + ctx)
        for t in (TM, SSD_TM, SSD_L):
            assert seq % t == 0 and (batch * ctx) % t == 0, (seq, ctx, t)
        assert ctx == SSD_TM and ctx % SSD_L == 0 and ctx == S5_TM and seq % S5_TM == 0
        assert seq % min(ATT_TQ, seq) == 0 and seq % min(ATT_TK, seq) == 0
        assert batch * 2 == SUBLANES and seq % GRID_W == 0

    def sample_of_tile(self, i, tile):
        return jnp.where(i < self.n_lat // tile, i // (self.seq // tile), self.batch)


def _post_norm_emit(h, upd, lng, lnb, nsh, nsc, ho_ref, hmo_ref):
    t = DN_ALPHA * h + upd
    mu = jnp.mean(t, axis=-1, keepdims=True)
    tc = t - mu
    var = jnp.mean(tc * tc, axis=-1, keepdims=True)
    hn = tc * lax.rsqrt(var + LN_EPS) * lng + lnb
    ho_ref[...] = hn
    hmo_ref[...] = (hn * (1.0 + nsc) + nsh).astype(hmo_ref.dtype)


def _mod_kernel(c_ref, w_ref, b_ref, o_ref):
    c = c_ref[...]
    s = (c * jax.nn.sigmoid(c)).astype(BF16)
    o_ref[...] = jnp.dot(s, w_ref[...].astype(BF16), preferred_element_type=F32) + b_ref[...]


def _mod_all(cc, w_mod, b_mod):
    depth, d, n = w_mod.shape
    return pl.pallas_call(
        _mod_kernel,
        grid=(depth, n // TN_MOD),
        in_specs=[pl.BlockSpec((MOD_ROWS, d), lambda l, j: (0, 0)),
                  pl.BlockSpec((None, d, TN_MOD), lambda l, j: (l, 0, j)),
                  pl.BlockSpec((None, 1, TN_MOD), lambda l, j: (l, 0, j))],
        out_specs=pl.BlockSpec((None, MOD_ROWS, TN_MOD), lambda l, j: (l, 0, j)),
        out_shape=jax.ShapeDtypeStruct((depth, MOD_ROWS, n), F32),
        compiler_params=_params("parallel", "parallel"),
        name="mod_matmul",
    )(cc, w_mod, b_mod.reshape(depth, 1, n))


def _modulate_kernel(h_ref, sh_ref, sc_ref, o_ref):
    o_ref[...] = (h_ref[...] * (1.0 + sc_ref[...]) + sh_ref[...]).astype(o_ref.dtype)


def _modulate(lay, h, sh, sc):
    vec = pl.BlockSpec((None, 1, D_MODEL), lambda i: (lay.sample_of_tile(i, TM), 0, 0))
    row = pl.BlockSpec((TM, D_MODEL), lambda i: (i, 0))
    return pl.pallas_call(
        _modulate_kernel, grid=(lay.n_tok // TM,),
        in_specs=[row, vec, vec], out_specs=row,
        out_shape=jax.ShapeDtypeStruct((lay.n_tok, D_MODEL), BF16),
        compiler_params=_params("parallel"), name="modulate",
    )(h, sh, sc)


def _ffn_kernel(hm_ref, h_ref, w1_ref, w3_ref, w2_ref, gate_ref, lng_ref, lnb_ref, nsh_ref, nsc_ref,
                ho_ref, hmo_ref, acc_ref):
    j = pl.program_id(1)

    @pl.when(j == 0)
    def _():
        acc_ref[...] = jnp.zeros(acc_ref.shape, F32)

    hm = hm_ref[...]
    a = jnp.dot(hm, w1_ref[...], preferred_element_type=F32)
    b = jnp.dot(hm, w3_ref[...], preferred_element_type=F32)
    p = (a * jax.nn.sigmoid(a) * b).astype(BF16)
    acc_ref[...] += jnp.dot(p, w2_ref[...], preferred_element_type=F32)

    @pl.when(j == pl.num_programs(1) - 1)
    def _():
        upd = (FFN_HALF * gate_ref[...]) * acc_ref[...]
        _post_norm_emit(h_ref[...], upd, lng_ref[...], lnb_ref[...], nsh_ref[...], nsc_ref[...],
                        ho_ref, hmo_ref)


def _half_ffn(lay, rows, hm, h, w1, w3, w2, gate, lng, lnb, nsh, nsc):
    vec = pl.BlockSpec((None, 1, D_MODEL), lambda i, j: (lay.sample_of_tile(i, TM), 0, 0))
    par = pl.BlockSpec((1, D_MODEL), lambda i, j: (0, 0))
    row = pl.BlockSpec((TM, D_MODEL), lambda i, j: (i, 0))
    return pl.pallas_call(
        _ffn_kernel,
        grid=(rows // TM, D_FF // TN_FF),
        in_specs=[row, row,
                  pl.BlockSpec((D_MODEL, TN_FF), lambda i, j: (0, j)),
                  pl.BlockSpec((D_MODEL, TN_FF), lambda i, j: (0, j)),
                  pl.BlockSpec((TN_FF, D_MODEL), lambda i, j: (j, 0)),
                  vec, par, par, vec, vec],
        out_specs=[row, row],
        out_shape=[jax.ShapeDtypeStruct((rows, D_MODEL), F32),
                   jax.ShapeDtypeStruct((rows, D_MODEL), BF16)],
        scratch_shapes=[pltpu.VMEM((TM, D_MODEL), F32)],
        compiler_params=_params("parallel", "arbitrary"), name="half_ffn",
    )(hm, h, w1, w3, w2, gate, lng, lnb, nsh, nsc)


def _proj_kernel(x_ref, w_ref, o_ref):
    o_ref[...] = jnp.dot(x_ref[...], w_ref[...], preferred_element_type=F32).astype(o_ref.dtype)


def _proj(lay, hm, w, out_dtype, name):
    n = w.shape[1]
    tn = min(TN_PROJ, n)
    return pl.pallas_call(
        _proj_kernel, grid=(lay.n_tok // TM, n // tn),
        in_specs=[pl.BlockSpec((TM, D_MODEL), lambda i, j: (i, 0)),
                  pl.BlockSpec((D_MODEL, tn), lambda i, j: (0, j))],
        out_specs=pl.BlockSpec((TM, tn), lambda i, j: (i, j)),
        out_shape=jax.ShapeDtypeStruct((lay.n_tok, n), out_dtype),
        compiler_params=_params("parallel", "parallel"), name=name,
    )(hm, w)


def _proj_qk_kernel(x_ref, w_ref, cos_ref, sin_ref, o_ref, *, n_lat_tiles, is_q):
    i = pl.program_id(0)
    acc = jnp.dot(x_ref[...], w_ref[...], preferred_element_type=F32)
    lane = lax.broadcasted_iota(jnp.int32, (acc.shape[0], LANES), 1)
    low_rows = lax.broadcasted_iota(jnp.int32, (LANES, acc.shape[0]), 0) < ATT_DH

    def emit(rotate):
        if rotate:
            cos, sin = cos_ref[...], sin_ref[...]
            first = (lane % (2 * ROPE_FREQS)) < ROPE_FREQS
        for h in range(acc.shape[1] // LANES):
            t = acc[:, h * LANES:(h + 1) * LANES]
            if rotate:
                partner = jnp.where(first, pltpu.roll(t, LANES - ROPE_FREQS, 1), pltpu.roll(t, ROPE_FREQS, 1))
                t = t * cos + partner * sin
            if is_q:
                tt = (t * (ATT_DH ** -0.5 * LOG2_E)).T
                o_ref[h, 0] = jnp.where(low_rows, tt, 0.0).astype(o_ref.dtype)
                o_ref[h, 1] = jnp.where(low_rows, 0.0, tt).astype(o_ref.dtype)
            else:
                o_ref[:, h * LANES:(h + 1) * LANES] = t.astype(o_ref.dtype)

    @pl.when(i < n_lat_tiles)
    def _():
        emit(True)

    @pl.when(i >= n_lat_tiles)
    def _():
        emit(False)


def _rope_tables(seq):
    rows = seq // GRID_W
    row = jnp.repeat(jnp.arange(rows), GRID_W)
    col = jnp.tile(jnp.arange(GRID_W), rows)
    inv = ROPE_BASE ** (-jnp.arange(ROPE_FREQS, dtype=F32) / ROPE_FREQS)
    ar, ac = row[:, None] * inv, col[:, None] * inv
    cos = jnp.concatenate([jnp.cos(ar), jnp.cos(ar), jnp.cos(ac), jnp.cos(ac)], axis=1)
    sin = jnp.concatenate([-jnp.sin(ar), jnp.sin(ar), -jnp.sin(ac), jnp.sin(ac)], axis=1)
    return jnp.tile(cos, (1, 2)), jnp.tile(sin, (1, 2))


def _proj_qk(lay, hm, w, cos, sin, is_q):
    tps = lay.seq // TM
    kern = functools.partial(_proj_qk_kernel, n_lat_tiles=lay.n_lat // TM, is_q=is_q)
    tab = pl.BlockSpec((TM, LANES), lambda i: (i % tps, 0))
    if is_q:
        out_spec = pl.BlockSpec((ATT_HEADS, 2, LANES, TM), lambda i: (0, 0, 0, i))
        out_shape = jax.ShapeDtypeStruct((ATT_HEADS, 2, LANES, lay.n_tok), BF16)
    else:
        out_spec = pl.BlockSpec((TM, BRANCH_W), lambda i: (i, 0))
        out_shape = jax.ShapeDtypeStruct((lay.n_tok, BRANCH_W), BF16)
    return pl.pallas_call(
        kern, grid=(lay.n_tok // TM,),
        in_specs=[pl.BlockSpec((TM, D_MODEL), lambda i: (i, 0)),
                  pl.BlockSpec((D_MODEL, BRANCH_W), lambda i: (0, 0)), tab, tab],
        out_specs=out_spec, out_shape=out_shape,
        compiler_params=_params("parallel"), name="proj_q" if is_q else "proj_k",
    )(hm, w, cos, sin)


def _proj_v_kernel(x_ref, w_ref, o_ref):
    acc = jnp.dot(x_ref[...], w_ref[...], preferred_element_type=F32)
    tail = ATT_V_ROWS - ATT_DV
    ones_row = jnp.where(lax.broadcasted_iota(jnp.int32, (tail, acc.shape[0]), 0) == 0, 1.0, 0.0)
    for h in range(acc.shape[1] // LANES):
        o_ref[h, :ATT_DV, :] = acc[:, h * LANES:(h + 1) * LANES].T.astype(o_ref.dtype)
        o_ref[h, ATT_DV:, :] = ones_row.astype(o_ref.dtype)


def _proj_v(lay, hm, w):
    return pl.pallas_call(
        _proj_v_kernel, grid=(lay.n_tok // TM,),
        in_specs=[pl.BlockSpec((TM, D_MODEL), lambda i: (i, 0)),
                  pl.BlockSpec((D_MODEL, BRANCH_W), lambda i: (0, 0))],
        out_specs=pl.BlockSpec((ATT_HEADS, ATT_V_ROWS, TM), lambda i: (0, 0, i)),
        out_shape=jax.ShapeDtypeStruct((ATT_HEADS, ATT_V_ROWS, lay.n_tok), BF16),
        compiler_params=_params("parallel"), name="proj_v",
    )(hm, w)


def _attn_kernel(lamv_ref, subln_ref, q0_ref, q1_ref, kc_ref, vc_ref, *rest, n_lat_chunks, tk, lam_init):
    if n_lat_chunks:
        kl_ref, vl_ref, o_ref = rest
    else:
        _, o_ref = rest
    lv = lamv_ref[...]
    lam = (jnp.exp(jnp.sum(lv[0:1] * lv[1:2], axis=-1, keepdims=True))
           - jnp.exp(jnp.sum(lv[2:3] * lv[3:4], axis=-1, keepdims=True)) + lam_init)

    def first(qt, k, vt):
        s = jnp.dot(k, qt, preferred_element_type=F32)
        m = jnp.max(s, axis=0, keepdims=True)
        return m, jnp.dot(vt, jnp.exp2(s - m).astype(BF16), preferred_element_type=F32)

    def update(state, qt, k, vt):
        m, acc = state
        s = jnp.dot(k, qt, preferred_element_type=F32)
        m_new = jnp.maximum(m, jnp.max(s, axis=0, keepdims=True))
        e = jnp.exp2(s - m_new).astype(BF16)
        return m_new, jnp.exp2(m - m_new) * acc + jnp.dot(vt, e, preferred_element_type=F32)

    qts = (q0_ref[...], q1_ref[...])
    kc, vtc = kc_ref[...], vc_ref[...]
    states = tuple(first(qt, kc, vtc) for qt in qts)
    if n_lat_chunks:
        def body(c, st):
            off = pl.multiple_of(c * tk, tk)
            k, vt = kl_ref[pl.ds(off, tk), :], vl_ref[:, pl.ds(off, tk)]
            return tuple(update(s_, qt, k, vt) for s_, qt in zip(st, qts))

        states = lax.fori_loop(0, n_lat_chunks, body, states)
    outs = [a[:ATT_DV] / a[ATT_DV:ATT_DV + 1] for _, a in states]
    o = outs[0] - lam * outs[1]
    o = o * lax.rsqrt(jnp.mean(o * o, axis=0, keepdims=True) + RMS_EPS) * subln_ref[...] * (1.0 - lam_init)
    o_ref[...] = o.T.astype(o_ref.dtype)


def _diff_attention(lay, qt, k, vt, att_lam, subln, lam_init):
    b_, s_, c_ = lay.batch, lay.seq, lay.ctx
    lamv = pl.BlockSpec((4, ATT_DH), lambda *_: (0, 0))
    sub = pl.BlockSpec((ATT_DV, 1), lambda *_: (0, 0))
    out_shape = jax.ShapeDtypeStruct((lay.n_tok, BRANCH_W), BF16)
    ctx_blk = lay.n_lat // c_

    def specs(tq, q_blk):
        return [lamv, sub,
                pl.BlockSpec((None, None, LANES, tq), lambda b, h, i: (h, 0, 0, q_blk(b, i))),
                pl.BlockSpec((None, None, LANES, tq), lambda b, h, i: (h, 1, 0, q_blk(b, i))),
                pl.BlockSpec((c_, LANES), lambda b, h, i: (ctx_blk + b, h)),
                pl.BlockSpec((None, ATT_V_ROWS, c_), lambda b, h, i: (h, 0, ctx_blk + b))]

    tq, tk = min(ATT_TQ, s_), min(ATT_TK, s_)
    n_q = s_ // tq
    lat_blk = lambda b, i: b * n_q + i
    lat = pl.pallas_call(
        functools.partial(_attn_kernel, n_lat_chunks=s_ // tk, tk=tk, lam_init=lam_init),
        grid=(b_, ATT_HEADS, n_q),
        in_specs=specs(tq, lat_blk) + [
            pl.BlockSpec((s_, LANES), lambda b, h, i: (b, h)),
            pl.BlockSpec((None, ATT_V_ROWS, s_), lambda b, h, i: (h, 0, b))],
        out_specs=pl.BlockSpec((tq, LANES), lambda b, h, i: (lat_blk(b, i), h)),
        out_shape=out_shape,
        compiler_params=_params("parallel", "parallel", "arbitrary"), name="attn_latent",
    )(att_lam, subln, qt, qt, k, vt, k, vt)
    ctx_q = lambda b, i: ctx_blk + b
    return pl.pallas_call(
        functools.partial(_attn_kernel, n_lat_chunks=0, tk=0, lam_init=lam_init),
        grid=(b_, ATT_HEADS, 1),
        in_specs=specs(c_, ctx_q) + [pl.BlockSpec(memory_space=pl.ANY)],
        out_specs=pl.BlockSpec((c_, LANES), lambda b, h, i: (ctx_q(b, i), h)),
        out_shape=out_shape, input_output_aliases={6: 0},
        compiler_params=_params("parallel", "parallel", "arbitrary"), name="attn_context",
    )(att_lam, subln, qt, qt, k, vt, lat)


def _ssd_prep_kernel(prev_ref, cur_ref, next_ref, dt_ref, cw_ref, cb_ref, dtb_ref,
                     xs_ref, bm_ref, cm_ref, dts_ref, dtst_ref, ext_ref, *, tiles_per_seq, n_lat_tiles):
    i = pl.program_id(0)
    is_ctx = i >= n_lat_tiles
    first = jnp.logical_or(is_ctx, i % tiles_per_seq == 0)
    last = jnp.logical_or(is_ctx, i % tiles_per_seq == tiles_per_seq - 1)
    tm = cur_ref.shape[0]
    ext_ref[0:SUBLANES, :] = jnp.where(first, 0.0, prev_ref[...])
    ext_ref[SUBLANES:SUBLANES + tm, :] = cur_ref[...]
    ext_ref[SUBLANES + tm:, :] = jnp.where(last, 0.0, next_ref[...])
    acc = jnp.zeros(cur_ref.shape, F32) + cb_ref[...]
    for k in range(SSD_CONV):
        start = SUBLANES + k - SSD_CONV // 2
        acc = acc + ext_ref[start:start + tm, :] * cw_ref[k:k + 1, :]
    act = acc * jax.nn.sigmoid(acc)
    xs_ref[...] = act[:, :SSD_INNER]
    bm_ref[...] = act[:, SSD_INNER:SSD_INNER + SSD_BC_W].astype(bm_ref.dtype)
    cm_ref[...] = act[:, SSD_INNER + SSD_BC_W:].astype(cm_ref.dtype)
    x = dt_ref[...] + dtb_ref[...]
    sp = jnp.maximum(x, 0.0) + jnp.log1p(jnp.exp(-jnp.abs(x)))
    dts_ref[0] = sp
    dts_ref[1] = pltpu.roll(sp, DT_PAD - SSD_HEADS, 1)
    spt = sp.T
    dtst_ref[0] = spt[0:SSD_HEADS]
    dtst_ref[1] = spt[SSD_HEADS:2 * SSD_HEADS]


def _ssd_prep(lay, xbc, dt, conv_w, conv_b, dt_bias):
    n, tm = lay.n_tok, SSD_TM
    sub_per_tile = tm // SUBLANES
    n_sub = n // SUBLANES
    kern = functools.partial(_ssd_prep_kernel, tiles_per_seq=lay.seq // tm, n_lat_tiles=lay.n_lat // tm)
    row = lambda w: pl.BlockSpec((tm, w), lambda i: (i, 0))
    return pl.pallas_call(
        kern, grid=(n // tm,),
        in_specs=[pl.BlockSpec((SUBLANES, SSD_XBC_W), lambda i: (jnp.maximum(i * sub_per_tile - 1, 0), 0)),
                  row(SSD_XBC_W),
                  pl.BlockSpec((SUBLANES, SSD_XBC_W),
                               lambda i: (jnp.minimum((i + 1) * sub_per_tile, n_sub - 1), 0)),
                  row(DT_PAD),
                  pl.BlockSpec((SUBLANES, SSD_XBC_W), lambda i: (0, 0)),
                  pl.BlockSpec((1, SSD_XBC_W), lambda i: (0, 0)),
                  pl.BlockSpec((1, DT_PAD), lambda i: (0, 0))],
        out_specs=[row(SSD_INNER), row(SSD_BC_W), row(SSD_BC_W),
                   pl.BlockSpec((2, tm, DT_PAD), lambda i: (0, i, 0)),
                   pl.BlockSpec((2, SSD_HEADS, tm), lambda i: (0, 0, i))],
        out_shape=[jax.ShapeDtypeStruct((n, SSD_INNER), F32),
                   jax.ShapeDtypeStruct((n, SSD_BC_W), BF16),
                   jax.ShapeDtypeStruct((n, SSD_BC_W), BF16),
                   jax.ShapeDtypeStruct((2, n, DT_PAD), F32),
                   jax.ShapeDtypeStruct((2, SSD_HEADS, n), F32)],
        scratch_shapes=[pltpu.VMEM((tm + 2 * SUBLANES, SSD_XBC_W), F32)],
        compiler_params=_params("parallel"), name="ssd_prep",
    )(xbc, xbc, xbc, dt, conv_w, conv_b, dt_bias)


def _split_dot(a, x, x_is_lhs=False):
    out = None
    r = x
    for _ in range(3):
        t = r.astype(BF16)
        r = r - t.astype(F32)
        d = (jnp.dot(t, a, preferred_element_type=F32) if x_is_lhs
             else jnp.dot(a, t, preferred_element_type=F32))
        out = d if out is None else out + d
    return out


def _ssd_kernel(xs_ref, bm_ref, cm_ref, dts_ref, dtst_ref, arow_ref, acol_ref, y_ref, h_ref):
    sign = 1 - 2 * pl.program_id(1)

    @pl.when(pl.program_id(2) == 0)
    def _():
        h_ref[...] = jnp.zeros(h_ref.shape, F32)

    ll = xs_ref.shape[0]
    r = lax.broadcasted_iota(jnp.int32, (ll, ll), 0)
    c = lax.broadcasted_iota(jnp.int32, (ll, ll), 1)
    mask = (r - c) * sign >= 0
    mask_t = (r - c) * sign <= 0
    one_hot = lambda m: jnp.where(m, 1.0, 0.0).astype(BF16)
    dts = dts_ref[...]
    da = dts * arow_ref[...]
    da_t = dtst_ref[...] * acol_ref[...]
    cs = _split_dot(one_hot(mask), da)
    cs_t = _split_dot(one_hot(mask_t), da_t, x_is_lhs=True)
    tot = jnp.sum(da, axis=0, keepdims=True)
    to_end = jnp.exp(tot - cs)
    from_start = jnp.exp(cs)
    chunk_decay = jnp.exp(tot)
    lane = lax.broadcasted_iota(jnp.int32, (ll, LANES), 1)
    lo_half = lane < SSD_P

    def pair_expand(t, hd):
        rows = t.shape[0]
        return jnp.where(lo_half[:rows], jnp.broadcast_to(t[:, hd:hd + 1], (rows, LANES)),
                         jnp.broadcast_to(t[:, hd + 1:hd + 2], (rows, LANES)))

    gw = SSD_HPG * SSD_P
    for g in range(SSD_GROUPS):
        bg = bm_ref[:, g * SSD_N:(g + 1) * SSD_N]
        cg = cm_ref[:, g * SSD_N:(g + 1) * SSD_N]
        cb = lax.dot_general(cg, bg, (((1,), (1,)), ((), ())), preferred_element_type=F32)
        y_parts, xw_parts, fs_parts, cd_parts = [], [], [], []
        for pr in range(SSD_HPG // 2):
            hd = g * SSD_HPG + 2 * pr
            xdt = xs_ref[:, hd * SSD_P:(hd + 2) * SSD_P] * pair_expand(dts, hd)
            xw_parts.append((xdt * pair_expand(to_end, hd)).astype(BF16))
            fs_parts.append(pair_expand(from_start, hd))
            cd_parts.append(pair_expand(chunk_decay, hd))
            yp = None
            for k in range(2):
                seg = cs[:, hd + k:hd + k + 1] - cs_t[hd + k:hd + k + 1, :]
                dec = jnp.where(mask, jnp.exp(jnp.where(mask, seg, 0.0)), 0.0)
                w = (cb * dec).astype(BF16)
                xk = jnp.where(lo_half if k == 0 else jnp.logical_not(lo_half), xdt, 0.0).astype(BF16)
                d = jnp.dot(w, xk, preferred_element_type=F32)
                yp = d if yp is None else yp + d
            y_parts.append(yp)
        xw = jnp.concatenate(xw_parts, axis=1)
        states = lax.dot_general(bg, xw, (((0,), (0,)), ((), ())), preferred_element_type=F32)
        h_prev = h_ref[g]
        y_off = jnp.dot(cg, h_prev.astype(BF16), preferred_element_type=F32) * jnp.concatenate(fs_parts, axis=1)
        y_ref[:, g * gw:(g + 1) * gw] = jnp.concatenate(y_parts, axis=1) + y_off
        h_ref[g] = h_prev * jnp.concatenate(cd_parts, axis=1) + states


def _ssd_scan(lay, xs, bm, cm, dts, dtst, a_row, a_col):
    ll = SSD_L
    n_ctx, n_lat = lay.ctx // ll, lay.seq // ll
    ctx_base = lay.n_lat // ll

    def rb(b, d, s):
        cstep = jnp.where(d == 0, s, n_ctx - 1 - s)
        lstep = jnp.where(d == 0, s - n_ctx, n_lat - 1 - (s - n_ctx))
        return jnp.where(s < n_ctx, ctx_base + b * n_ctx + cstep, b * n_lat + lstep)

    row = lambda w: pl.BlockSpec((ll, w), lambda b, d, s: (rb(b, d, s), 0))
    return pl.pallas_call(
        _ssd_kernel, grid=(lay.batch, 2, n_ctx + n_lat),
        in_specs=[row(SSD_INNER), row(SSD_BC_W), row(SSD_BC_W),
                  pl.BlockSpec((None, ll, DT_PAD), lambda b, d, s: (d, rb(b, d, s), 0)),
                  pl.BlockSpec((None, SSD_HEADS, ll), lambda b, d, s: (d, 0, rb(b, d, s))),
                  pl.BlockSpec((None, 1, DT_PAD), lambda b, d, s: (d, 0, 0)),
                  pl.BlockSpec((None, SSD_HEADS, 1), lambda b, d, s: (d, 0, 0))],
        out_specs=pl.BlockSpec((None, ll, SSD_INNER), lambda b, d, s: (d, rb(b, d, s), 0)),
        out_shape=jax.ShapeDtypeStruct((2, lay.n_tok, SSD_INNER), F32),
        scratch_shapes=[pltpu.VMEM((SSD_GROUPS, SSD_N, SSD_HPG * SSD_P), F32)],
        compiler_params=_params("parallel", "parallel", "arbitrary"), name="ssd_scan",
    )(xs, bm, cm, dts, dtst, a_row, a_col)


def _ssd_gate_norm_kernel(y0_ref, y1_ref, xs_ref, z_ref, dsk_ref, nw_ref, o_ref):
    z = z_ref[...]
    gated = (y0_ref[...] + y1_ref[...] + xs_ref[...] * dsk_ref[...]) * (z * jax.nn.sigmoid(z))
    nw = nw_ref[...]
    for g in range(SSD_GROUPS):
        sl = slice(g * SSD_NORM_GROUP, (g + 1) * SSD_NORM_GROUP)
        t = gated[:, sl]
        o_ref[:, sl] = (t * lax.rsqrt(jnp.mean(t * t, axis=-1, keepdims=True) + RMS_EPS)
                        * nw[:, sl]).astype(o_ref.dtype)


def _ssd_gate_norm(lay, ydir, xs, z, dsk, nw):
    tm = SSD_TM
    row = pl.BlockSpec((tm, SSD_INNER), lambda i: (i, 0))
    par = pl.BlockSpec((1, SSD_INNER), lambda i: (0, 0))
    return pl.pallas_call(
        _ssd_gate_norm_kernel, grid=(lay.n_tok // tm,),
        in_specs=[pl.BlockSpec((None, tm, SSD_INNER), lambda i: (0, i, 0)),
                  pl.BlockSpec((None, tm, SSD_INNER), lambda i: (1, i, 0)), row, row, par, par],
        out_specs=row, out_shape=jax.ShapeDtypeStruct((lay.n_tok, SSD_INNER), BF16),
        compiler_params=_params("parallel"), name="ssd_gate_norm",
    )(ydir, ydir, xs, z, dsk, nw)


def _s5_operators(lam_re, lam_im, log_step, b_re, b_im, c_re, c_im):
    hp = lax.Precision.HIGHEST
    ll, hh = S5_L, S5_GROUP_CH
    step = jnp.exp(log_step)[..., None, None]
    d = jnp.arange(ll + 1, dtype=F32)
    p_mag = jnp.exp(lam_re[..., None] * step * d)
    p_ang = lam_im[..., None] * step * d
    p_re, p_im = p_mag * jnp.cos(p_ang), p_mag * jnp.sin(p_ang)
    ab_re, ab_im = p_re[..., 1], p_im[..., 1]
    den = lam_re * lam_re + lam_im * lam_im
    k_re = ((ab_re - 1.0) * lam_re + ab_im * lam_im) / den
    k_im = (ab_im * lam_re - (ab_re - 1.0) * lam_im) / den
    bb_re = k_re[..., None] * b_re - k_im[..., None] * b_im
    bb_im = k_re[..., None] * b_im + k_im[..., None] * b_re
    cp_re = c_re[..., None] * p_re[:, :, None] - c_im[..., None] * p_im[:, :, None]
    cp_im = c_re[..., None] * p_im[:, :, None] + c_im[..., None] * p_re[:, :, None]
    kern = (jnp.einsum('zghnd,zgnk->zgdhk', cp_re, bb_re, precision=hp)
            - jnp.einsum('zghnd,zgnk->zgdhk', cp_im, bb_im, precision=hp))
    s_idx = jnp.arange(ll)[:, None]
    l_idx = jnp.arange(ll)[None, :]
    t_f = jnp.where((l_idx >= s_idx)[None, :, :, None, None], kern[0][:, jnp.clip(l_idx - s_idx, 0, ll)], 0.0)
    t_b = jnp.where((s_idx >= l_idx)[None, :, :, None, None], kern[1][:, jnp.clip(s_idx - l_idx, 0, ll)], 0.0)
    toep = (t_f + t_b).transpose(0, 1, 4, 2, 3).reshape(S5_GROUPS, S5_CW, S5_CW)

    def state_in(z, powers):
        pr, pi = p_re[z][..., powers], p_im[z][..., powers]
        re = pr[..., None] * bb_re[z][:, :, None] - pi[..., None] * bb_im[z][:, :, None]
        im = pr[..., None] * bb_im[z][:, :, None] + pi[..., None] * bb_re[z][:, :, None]
        re = re.transpose(0, 2, 3, 1).reshape(S5_GROUPS, S5_CW, S5_N)
        im = im.transpose(0, 2, 3, 1).reshape(S5_GROUPS, S5_CW, S5_N)
        return jnp.concatenate([re, im], axis=-1)

    ws_f = state_in(0, ll - 1 - jnp.arange(ll))
    ws_b = state_in(1, jnp.arange(ll))
    w1 = jnp.concatenate([toep, ws_f, ws_b], axis=-1)

    def state_out(z, powers):
        re = cp_re[z][..., powers].transpose(0, 2, 3, 1).reshape(S5_GROUPS, S5_N, S5_CW)
        im = cp_im[z][..., powers].transpose(0, 2, 3, 1).reshape(S5_GROUPS, S5_N, S5_CW)
        return jnp.concatenate([re, -im], axis=1)

    wo = jnp.concatenate([state_out(0, jnp.arange(ll) + 1), state_out(1, ll - jnp.arange(ll))], axis=1)
    ar, ai = p_re[..., ll], p_im[..., ll]
    a1 = jnp.concatenate([ar, ar], axis=-1)
    a2 = jnp.concatenate([-ai, ai], axis=-1)
    zeros = jnp.zeros_like(a1[0])
    av = jnp.stack([a1[0], a2[0], a1[1], a2[1], zeros, zeros, zeros, zeros], axis=1)
    return w1.astype(BF16), wo.astype(BF16), av


def _s5_kernel(x_ref, w1_ref, wo_ref, av_ref, y_ref, u_scr, y_scr, sf_ref, sb_ref, *, n_ctx_tiles, n_tiles):
    rows = x_ref.shape[0]
    row_tile = rows // S5_ROW_TILES
    nst = 2 * S5_N
    lane_blk = lax.broadcasted_iota(jnp.int32, (row_tile, LANES), 1) // S5_GROUP_CH

    def block_transpose(arrs):
        a = list(arrs)
        k = S5_GB // 2
        while k:
            bit = (lane_blk & k) != 0
            for i in range(S5_GB):
                if not i & k:
                    lo, hi = a[i], a[i + k]
                    a[i] = jnp.where(bit, pltpu.roll(hi, k * S5_GROUP_CH, 1), lo)
                    a[i + k] = jnp.where(bit, hi, pltpu.roll(lo, LANES - k * S5_GROUP_CH, 1))
            k //= 2
        return a

    def gather(r, _):
        r0 = pl.multiple_of(r * row_tile, BF16_ROWS)
        for hv in range(S5_L // S5_GB):
            per_group = block_transpose(
                [x_ref[pl.ds(r0, row_tile), hv * S5_GB + sl, :] for sl in range(S5_GB)])
            for g in range(S5_GB):
                u_scr[g, pl.ds(r0, row_tile), hv * LANES:(hv + 1) * LANES] = per_group[g].astype(u_scr.dtype)
        return 0

    lax.fori_loop(0, S5_ROW_TILES, gather, 0)

    for g in range(S5_GB):
        p = jnp.dot(u_scr[g], w1_ref[g], preferred_element_type=F32)
        y_scr[g] = p[:, :S5_CW]
        sf_ref[:, g * nst:(g + 1) * nst] = p[:, S5_CW:S5_CW + nst]
        sb_ref[:, g * nst:(g + 1) * nst] = p[:, S5_CW + nst:]

    wide = S5_GB * nst
    half = SUBLANES // 2
    coef = lambda k: jnp.broadcast_to(
        jnp.concatenate([av_ref[g][k:k + 1] for g in range(S5_GB)], axis=1), (half, wide))
    a1f, a2f, a1b, a2b = coef(0), coef(1), coef(2), coef(3)
    low = lax.broadcasted_iota(jnp.int32, (SUBLANES, wide), 1) % nst < S5_N
    swap = lambda t: jnp.where(low, pltpu.roll(t, wide - S5_N, 1), pltpu.roll(t, S5_N, 1))

    def body(j, carry):
        hf, hfs, hb, hbs = carry
        of = pl.multiple_of(j * SUBLANES, SUBLANES)
        s = sf_ref[pl.ds(of, SUBLANES), :]
        ss = swap(s)
        h1 = a1f * hf + a2f * hfs + s[:half]
        h1s = a1f * hfs - a2f * hf + ss[:half]
        sf_ref[pl.ds(of, SUBLANES), :] = jnp.concatenate([hf, h1], axis=0)
        h2 = a1f * h1 + a2f * h1s + s[half:]
        h2s = a1f * h1s - a2f * h1 + ss[half:]
        jb = jnp.where(j < n_ctx_tiles, n_ctx_tiles - 1 - j, n_tiles - 1 - (j - n_ctx_tiles))
        ob = pl.multiple_of(jb * SUBLANES, SUBLANES)
        s = sb_ref[pl.ds(ob, SUBLANES), :]
        ss = swap(s)
        g1 = a1b * hb + a2b * hbs + s[half:]
        g1s = a1b * hbs - a2b * hb + ss[half:]
        sb_ref[pl.ds(ob, SUBLANES), :] = jnp.concatenate([g1, hb], axis=0)
        g2 = a1b * g1 + a2b * g1s + s[:half]
        g2s = a1b * g1s - a2b * g1 + ss[:half]
        return h2, h2s, g2, g2s

    z = jnp.zeros((half, wide), F32)
    lax.fori_loop(0, n_tiles, body, (z, z, z, z))

    for g in range(S5_GB):
        wo = wo_ref[g]
        y_scr[g] += (
            jnp.dot(sf_ref[:, g * nst:(g + 1) * nst].astype(BF16), wo[:nst], preferred_element_type=F32)
            + jnp.dot(sb_ref[:, g * nst:(g + 1) * nst].astype(BF16), wo[nst:], preferred_element_type=F32))

    def scatter(r, _):
        r0 = pl.multiple_of(r * row_tile, BF16_ROWS)
        for hv in range(S5_L // S5_GB):
            per_lag = block_transpose(
                [y_scr[g, pl.ds(r0, row_tile), hv * LANES:(hv + 1) * LANES] for g in range(S5_GB)])
            for sl in range(S5_GB):
                y_ref[pl.ds(r0, row_tile), hv * S5_GB + sl, :] = per_lag[sl]
        return 0

    lax.fori_loop(0, S5_ROW_TILES, scatter, 0)


def _s5_chunked(lay, x, w1, wo, av):
    rows = lay.n_tok // S5_L
    n_ctx_rows = lay.batch * lay.ctx // S5_L
    assert rows % (S5_ROW_TILES * BF16_ROWS) == 0
    kern = functools.partial(_s5_kernel, n_ctx_tiles=n_ctx_rows // SUBLANES, n_tiles=rows // SUBLANES)
    blk = pl.BlockSpec((rows, None, S5_L, LANES), lambda g: (0, g, 0, 0), pipeline_mode=pl.Buffered(1))
    return pl.pallas_call(
        kern, grid=(S5_NB,),
        in_specs=[blk,
                  pl.BlockSpec((S5_GB, S5_CW, S5_CW + 4 * S5_N), lambda g: (g, 0, 0)),
                  pl.BlockSpec((S5_GB, 4 * S5_N, S5_CW), lambda g: (g, 0, 0)),
                  pl.BlockSpec((S5_GB, SUBLANES, 2 * S5_N), lambda g: (g, 0, 0))],
        out_specs=blk,
        out_shape=jax.ShapeDtypeStruct((rows, S5_NB, S5_L, LANES), F32),
        scratch_shapes=[pltpu.VMEM((S5_GB, rows, S5_CW), BF16), pltpu.VMEM((S5_GB, rows, S5_CW), F32),
                        pltpu.VMEM((rows, S5_GB * 2 * S5_N), F32), pltpu.VMEM((rows, S5_GB * 2 * S5_N), F32)],
        compiler_params=_params("parallel"), name="s5_chunked",
    )(x, w1, wo, av)


def _s5_tile_index(lay, i):
    n_lat_tiles, per_seq = lay.n_lat // S5_TM, lay.seq // S5_TM
    return jnp.where(i < n_lat_tiles, 1 + i % per_seq, 0), jnp.where(i < n_lat_tiles, i // per_seq, i - n_lat_tiles)


def _s5_chunk_spec(lay):
    return pl.BlockSpec((S5_TM // S5_L, None, S5_NB, S5_L, LANES),
                        lambda i: (*_s5_tile_index(lay, i), 0, 0, 0))


def _proj_u_kernel(x_ref, w_ref, o_ref):
    acc = jnp.dot(x_ref[...], w_ref[...], preferred_element_type=F32)
    for nb in range(S5_NB):
        o_ref[:, nb] = acc[:, nb * LANES:(nb + 1) * LANES].reshape(S5_TM // S5_L, S5_L, LANES)


def _proj_u(lay, hm, w):
    chunks = (lay.seq + lay.ctx) // S5_L
    return pl.pallas_call(
        _proj_u_kernel, grid=(lay.n_tok // S5_TM,),
        in_specs=[pl.BlockSpec((S5_TM, D_MODEL), lambda i: (i, 0)),
                  pl.BlockSpec((D_MODEL, S5_CH), lambda i: (0, 0))],
        out_specs=_s5_chunk_spec(lay),
        out_shape=jax.ShapeDtypeStruct((chunks, lay.batch, S5_NB, S5_L, LANES), F32),
        compiler_params=_params("parallel"), name="proj_u",
    )(hm, w)


def _s5_glu_kernel(ys_ref, u_ref, dsk_ref, w_ref, b_ref, o_ref):
    natural = lambda ref: jnp.concatenate([ref[:, nb].reshape(S5_TM, LANES) for nb in range(S5_NB)], axis=1)
    t = natural(ys_ref) + natural(u_ref) * dsk_ref[...]
    t = 0.5 * t * (1.0 + jnp.tanh(math.sqrt(2.0 / math.pi) * (t + 0.044715 * (t * t * t))))
    gate = jnp.dot(t.astype(BF16), w_ref[...], preferred_element_type=F32) + b_ref[...]
    o_ref[...] = (t * jax.nn.sigmoid(gate)).astype(o_ref.dtype)


def _s5_glu(lay, ys, u, dsk, w, b):
    par = pl.BlockSpec((1, S5_CH), lambda i: (0, 0))
    return pl.pallas_call(
        _s5_glu_kernel, grid=(lay.n_tok // S5_TM,),
        in_specs=[_s5_chunk_spec(lay), _s5_chunk_spec(lay), par,
                  pl.BlockSpec((S5_CH, S5_CH), lambda i: (0, 0)), par],
        out_specs=pl.BlockSpec((S5_TM, S5_CH), lambda i: (i, 0)),
        out_shape=jax.ShapeDtypeStruct((lay.n_tok, S5_CH), BF16),
        compiler_params=_params("parallel"), name="s5_glu",
    )(ys, u, dsk, w, b)


def _merge_kernel(oa_ref, os_ref, o5_ref, ga_ref, gs_ref, g5_ref, wa_ref, ws_ref, w5_ref, o_ref):
    acc = None
    for o, g, w in ((oa_ref, ga_ref, wa_ref), (os_ref, gs_ref, ws_ref), (o5_ref, g5_ref, w5_ref)):
        t = jax.nn.sigmoid(g[...]) * jnp.dot(o[...], w[...], preferred_element_type=F32)
        acc = t if acc is None else acc + t
    o_ref[...] = acc.astype(o_ref.dtype)


def _merge(rows, o_att, o_ssd, o_s5, g, w_branch):
    nt = D_MODEL // TN_MERGE
    row = pl.BlockSpec((TM, BRANCH_W), lambda i, j: (i, 0))
    gate = lambda k: pl.BlockSpec((TM, TN_MERGE), lambda i, j: (i, k * nt + j))
    wb = lambda k: pl.BlockSpec((None, BRANCH_W, TN_MERGE), lambda i, j: (k, 0, j))
    return pl.pallas_call(
        _merge_kernel, grid=(rows // TM, nt),
        in_specs=[row, row, row, gate(0), gate(1), gate(2), wb(0), wb(1), wb(2)],
        out_specs=pl.BlockSpec((TM, TN_MERGE), lambda i, j: (i, j)),
        out_shape=jax.ShapeDtypeStruct((rows, D_MODEL), BF16),
        compiler_params=_params("parallel", "parallel"), name="branch_merge",
    )(o_att, o_ssd, o_s5, g, g, g, w_branch, w_branch, w_branch)


def _out_norm_kernel(mx_ref, h_ref, w_ref, gate_ref, lng_ref, lnb_ref, nsh_ref, nsc_ref, ho_ref, hmo_ref):
    y = jnp.dot(mx_ref[...], w_ref[...], preferred_element_type=F32)
    _post_norm_emit(h_ref[...], gate_ref[...] * y, lng_ref[...], lnb_ref[...], nsh_ref[...], nsc_ref[...],
                    ho_ref, hmo_ref)


def _out_norm(lay, rows, mixed, h, w_out, gate, lng, lnb, nsh, nsc):
    vec = pl.BlockSpec((None, 1, D_MODEL), lambda i: (lay.sample_of_tile(i, TM), 0, 0))
    par = pl.BlockSpec((1, D_MODEL), lambda i: (0, 0))
    row = pl.BlockSpec((TM, D_MODEL), lambda i: (i, 0))
    return pl.pallas_call(
        _out_norm_kernel, grid=(rows // TM,),
        in_specs=[row, row, pl.BlockSpec((D_MODEL, D_MODEL), lambda i: (0, 0)), vec, par, par, vec, vec],
        out_specs=[row, row],
        out_shape=[jax.ShapeDtypeStruct((rows, D_MODEL), F32),
                   jax.ShapeDtypeStruct((rows, D_MODEL), BF16)],
        compiler_params=_params("parallel"), name="out_norm",
    )(mixed, h, w_out, gate, lng, lnb, nsh, nsc)


def _pad_cols(t, width):
    return jnp.pad(t, [(0, 0)] * (t.ndim - 1) + [(0, width - t.shape[-1])])


def _token_mixer(lay, rows_out, hm, rope, lam_init, w_in, att_lam, att_subln, conv_w, conv_b, a_log, dt_bias,
                 ssd_d, ssd_norm, s5_ops, s5_d, glu_w, glu_b, w_branch):
    cuts = [0]
    for w in (BRANCH_W, BRANCH_W, BRANCH_W, SSD_INNER, SSD_XBC_W, 2 * SSD_HEADS, S5_CH, N_BRANCH * D_MODEL):
        cuts.append(cuts[-1] + w)
    w_q, w_k, w_v, w_z, w_xbc, w_dt, w_u, w_g = (
        w_in[:, a:b].astype(BF16) for a, b in zip(cuts[:-1], cuts[1:]))
    qt = _proj_qk(lay, hm, w_q, *rope, is_q=True)
    k = _proj_qk(lay, hm, w_k, *rope, is_q=False)
    vt = _proj_v(lay, hm, w_v)
    z = _proj(lay, hm, w_z, F32, "proj_z")
    xbc = _proj(lay, hm, w_xbc, F32, "proj_xbc")
    dt = _proj(lay, hm, _pad_cols(w_dt, DT_PAD), F32, "proj_dt")
    u = _proj_u(lay, hm, w_u)
    g = _proj(lay, hm, w_g, F32, "proj_gate")

    o_att = _diff_attention(lay, qt, k, vt, att_lam, att_subln.reshape(ATT_DV, 1), lam_init)

    conv_w8 = jnp.pad(conv_w, ((0, SUBLANES - SSD_CONV), (0, 0)))
    xs, bm, cm, dts, dtst = _ssd_prep(lay, xbc, dt, conv_w8, conv_b.reshape(1, -1),
                                      _pad_cols(dt_bias.reshape(1, -1), DT_PAD))
    a = -jnp.exp(a_log.astype(F32))
    ydir = _ssd_scan(lay, xs, bm, cm, dts, dtst, _pad_cols(a, DT_PAD)[:, None, :], a[:, :, None])
    o_ssd = _ssd_gate_norm(lay, ydir, xs, z, jnp.repeat(ssd_d, SSD_P).reshape(1, -1),
                           ssd_norm.reshape(1, -1))

    ys = _s5_chunked(lay, u.reshape(lay.n_tok // S5_L, S5_NB, S5_L, LANES), *s5_ops)
    o_s5 = _s5_glu(lay, ys.reshape(u.shape), u, s5_d.reshape(1, -1), glu_w.astype(BF16),
                   glu_b.reshape(1, -1))

    return _merge(rows_out, o_att, o_ssd, o_s5, g, w_branch.astype(BF16))


def _trunk(lay, x, c, ctx, c_ctx, w_mod, b_mod, ln_g, ln_b, ffn_w1, ffn_w3, ffn_w2, w_in,
           att_lam, att_subln, ssd_conv_w, ssd_conv_b, ssd_a_log, ssd_dt_bias, ssd_d, ssd_norm,
           s5_lam_re, s5_lam_im, s5_log_step, s5_b_re, s5_b_im, s5_c_re, s5_c_im,
           s5_d, s5_glu_w, s5_glu_b, w_branch, w_out):
    depth = w_mod.shape[0]
    h = jnp.concatenate([x.reshape(lay.n_lat, D_MODEL), ctx.reshape(-1, D_MODEL)], axis=0)
    cc = jnp.concatenate([c, c_ctx[None], jnp.zeros((MOD_ROWS - lay.batch - 1, D_MODEL), F32)], axis=0)
    mod = _mod_all(cc, w_mod, b_mod).reshape(depth, MOD_ROWS, N_MOD, 1, D_MODEL)
    mvec = lambda l, k: mod[l, :, k]
    zero_vec = jnp.zeros((MOD_ROWS, 1, D_MODEL), F32)
    rope = _rope_tables(lay.seq)
    lnp = lambda l, k: (ln_g[l, k].reshape(1, -1), ln_b[l, k].reshape(1, -1))

    hm = _modulate(lay, h, mvec(0, 0), mvec(0, 1))
    for l in range(depth):
        lam_init = LAMBDA_INIT_BASE - LAMBDA_INIT_SPAN * math.exp(-LAMBDA_INIT_RATE * l)
        last = l + 1 == depth
        rows = lay.n_lat if last else lay.n_tok
        h, hm = _half_ffn(lay, lay.n_tok, hm, h, ffn_w1[l, 0].astype(BF16), ffn_w3[l, 0].astype(BF16),
                          ffn_w2[l, 0].astype(BF16), mvec(l, 2), *lnp(l, 0), mvec(l, 3), mvec(l, 4))
        s5_ops = _s5_operators(s5_lam_re[l], s5_lam_im[l], s5_log_step[l], s5_b_re[l], s5_b_im[l],
                               s5_c_re[l], s5_c_im[l])
        mixed = _token_mixer(lay, rows, hm, rope, lam_init, w_in[l], att_lam[l], att_subln[l],
                             ssd_conv_w[l], ssd_conv_b[l], ssd_a_log[l], ssd_dt_bias[l], ssd_d[l],
                             ssd_norm[l], s5_ops, s5_d[l], s5_glu_w[l], s5_glu_b[l], w_branch[l])
        h, hm = _out_norm(lay, rows, mixed, h, w_out[l].astype(BF16), mvec(l, 5), *lnp(l, 1),
                          mvec(l, 6), mvec(l, 7))
        nxt = (zero_vec, zero_vec) if last else (mvec(l + 1, 0), mvec(l + 1, 1))
        h, hm = _half_ffn(lay, rows, hm, h, ffn_w1[l, 1].astype(BF16), ffn_w3[l, 1].astype(BF16),
                          ffn_w2[l, 1].astype(BF16), mvec(l, 8), *lnp(l, 2), *nxt)
    return h.reshape(x.shape)


def kernel(x, c, ctx, c_ctx, w_mod, b_mod, ln_g, ln_b, ffn_w1, ffn_w3, ffn_w2, w_in, att_lam, att_subln, ssd_conv_w, ssd_conv_b, ssd_a_log, ssd_dt_bias, ssd_d, ssd_norm, s5_lam_re, s5_lam_im, s5_log_step, s5_b_re, s5_b_im, s5_c_re, s5_c_im, s5_d, s5_glu_w, s5_glu_b, w_branch, w_out):
    lay = Layout(x.shape[0], x.shape[1], ctx.shape[1])
    return _trunk(lay, x, c, ctx, c_ctx, w_mod, b_mod, ln_g, ln_b, ffn_w1, ffn_w3, ffn_w2, w_in,
                  att_lam, att_subln, ssd_conv_w, ssd_conv_b, ssd_a_log, ssd_dt_bias, ssd_d, ssd_norm,
                  s5_lam_re, s5_lam_im, s5_log_step, s5_b_re, s5_b_im, s5_c_re, s5_c_im,
                  s5_d, s5_glu_w, s5_glu_b, w_branch, w_out)
```

```python
import functools
import math

import jax
import jax.numpy as jnp
from jax import lax
from jax.experimental import pallas as pl
from jax.experimental.pallas import tpu as pltpu

F32 = jnp.float32
BF16 = jnp.bfloat16
LOG2_E = math.log2(math.e)

D_MODEL = 2048
DEPTH = 2
GRID_W = 64
DN_ALPHA = (2 * DEPTH) ** 0.25
N_SUB = 3
N_MOD = 3 * N_SUB
FFN_HALF = 0.5
D_FF = 5632
LN_EPS = 1e-5
RMS_EPS = 1e-6
BRANCH_W = D_MODEL // 2
N_BRANCH = 3
ATT_DH = 64
ATT_DV = 2 * ATT_DH
ATT_HEADS = BRANCH_W // ATT_DV
ROPE_BASE = 10000.0
ROPE_FREQS = ATT_DH // 4
LAMBDA_INIT_BASE = 0.8
LAMBDA_INIT_SPAN = 0.6
LAMBDA_INIT_RATE = 0.3
SSD_P = 64
SSD_HEADS = BRANCH_W // SSD_P
SSD_GROUPS = 4
SSD_HPG = SSD_HEADS // SSD_GROUPS
SSD_N = 128
SSD_CONV = 5
SSD_INNER = SSD_HEADS * SSD_P
SSD_BC_W = SSD_GROUPS * SSD_N
SSD_XBC_W = SSD_INNER + 2 * SSD_BC_W
SSD_NORM_GROUP = SSD_INNER // SSD_GROUPS
S5_CH = BRANCH_W
S5_GROUP_CH = 16
S5_GROUPS = S5_CH // S5_GROUP_CH
S5_N = 64

LANES = 128
SUBLANES = 8
VMEM_LIMIT_BYTES = 56 * 1024 * 1024
MOD_ROWS = 8
TM = 512
TN_FF = 512
TN_PROJ = 1024
TM_PROJ = 1024
TN_MERGE = 512
TN_MOD = 1024
ATT_V_ROWS = ATT_DV + 16
ATT_TQ = 1024
ATT_TK = 2048
SSD_L = 128
SSD_TM = 256
S5_L = 16
S5_CW = S5_L * S5_GROUP_CH
S5_GB = LANES // S5_GROUP_CH
S5_NB = S5_GROUPS // S5_GB
S5_TM = 256
S5_ROW_TILES = 4
BF16_ROWS = 2 * SUBLANES
DT_PAD = LANES


def _params(*sem):
    return pltpu.CompilerParams(dimension_semantics=sem, vmem_limit_bytes=VMEM_LIMIT_BYTES)


class Layout:
    def __init__(self, batch, seq, ctx):
        self.batch, self.seq, self.ctx = batch, seq, ctx
        self.n_lat = batch * seq
        self.n_tok = batch * (seq + ctx)
        for t in (TM, SSD_TM, SSD_L):
            assert seq % t == 0 and (batch * ctx) % t == 0, (seq, ctx, t)
        assert ctx == SSD_TM and ctx % SSD_L == 0 and ctx == S5_TM and seq % S5_TM == 0
        assert seq % min(ATT_TQ, seq) == 0 and seq % min(ATT_TK, seq) == 0
        assert batch * 2 == SUBLANES and seq % GRID_W == 0

    def sample_of_tile(self, i, tile):
        return jnp.where(i < self.n_lat // tile, i // (self.seq // tile), self.batch)


def _post_norm_emit(h, upd, lng, lnb, nsh, nsc, ho_ref, hmo_ref):
    t = DN_ALPHA * h + upd
    mu = jnp.mean(t, axis=-1, keepdims=True)
    tc = t - mu
    var = jnp.mean(tc * tc, axis=-1, keepdims=True)
    hn = tc * lax.rsqrt(var + LN_EPS) * lng + lnb
    ho_ref[...] = hn
    hmo_ref[...] = (hn * (1.0 + nsc) + nsh).astype(hmo_ref.dtype)


def _mod_kernel(c_ref, w_ref, b_ref, o_ref):
    c = c_ref[...]
    s = (c * jax.nn.sigmoid(c)).astype(BF16)
    o_ref[...] = jnp.dot(s, w_ref[...].astype(BF16), preferred_element_type=F32) + b_ref[...]


def _mod_all(cc, w_mod, b_mod):
    depth, d, n = w_mod.shape
    return pl.pallas_call(
        _mod_kernel,
        grid=(depth, n // TN_MOD),
        in_specs=[pl.BlockSpec((MOD_ROWS, d), lambda l, j: (0, 0)),
                  pl.BlockSpec((None, d, TN_MOD), lambda l, j: (l, 0, j)),
                  pl.BlockSpec((None, 1, TN_MOD), lambda l, j: (l, 0, j))],
        out_specs=pl.BlockSpec((None, MOD_ROWS, TN_MOD), lambda l, j: (l, 0, j)),
        out_shape=jax.ShapeDtypeStruct((depth, MOD_ROWS, n), F32),
        compiler_params=_params("parallel", "parallel"),
        name="mod_matmul",
    )(cc, w_mod, b_mod.reshape(depth, 1, n))


def _modulate_kernel(h_ref, sh_ref, sc_ref, o_ref):
    o_ref[...] = (h_ref[...] * (1.0 + sc_ref[...]) + sh_ref[...]).astype(o_ref.dtype)


def _modulate(lay, h, sh, sc):
    vec = pl.BlockSpec((None, 1, D_MODEL), lambda i: (lay.sample_of_tile(i, TM), 0, 0))
    row = pl.BlockSpec((TM, D_MODEL), lambda i: (i, 0))
    return pl.pallas_call(
        _modulate_kernel, grid=(lay.n_tok // TM,),
        in_specs=[row, vec, vec], out_specs=row,
        out_shape=jax.ShapeDtypeStruct((lay.n_tok, D_MODEL), BF16),
        compiler_params=_params("parallel"), name="modulate",
    )(h, sh, sc)


def _ffn_kernel(hm_ref, h_ref, w1_ref, w3_ref, w2_ref, gate_ref, lng_ref, lnb_ref, nsh_ref, nsc_ref,
                ho_ref, hmo_ref, acc_ref):
    j = pl.program_id(1)

    @pl.when(j == 0)
    def _():
        acc_ref[...] = jnp.zeros(acc_ref.shape, F32)

    hm = hm_ref[...]
    a = jnp.dot(hm, w1_ref[...], preferred_element_type=F32)
    b = jnp.dot(hm, w3_ref[...], preferred_element_type=F32)
    p = (a * jax.nn.sigmoid(a) * b).astype(BF16)
    acc_ref[...] += jnp.dot(p, w2_ref[...], preferred_element_type=F32)

    @pl.when(j == pl.num_programs(1) - 1)
    def _():
        upd = (FFN_HALF * gate_ref[...]) * acc_ref[...]
        _post_norm_emit(h_ref[...], upd, lng_ref[...], lnb_ref[...], nsh_ref[...], nsc_ref[...],
                        ho_ref, hmo_ref)


def _half_ffn(lay, rows, hm, h, w1, w3, w2, gate, lng, lnb, nsh, nsc):
    vec = pl.BlockSpec((None, 1, D_MODEL), lambda i, j: (lay.sample_of_tile(i, TM), 0, 0))
    par = pl.BlockSpec((1, D_MODEL), lambda i, j: (0, 0))
    row = pl.BlockSpec((TM, D_MODEL), lambda i, j: (i, 0))
    return pl.pallas_call(
        _ffn_kernel,
        grid=(rows // TM, D_FF // TN_FF),
        in_specs=[row, row,
                  pl.BlockSpec((D_MODEL, TN_FF), lambda i, j: (0, j)),
                  pl.BlockSpec((D_MODEL, TN_FF), lambda i, j: (0, j)),
                  pl.BlockSpec((TN_FF, D_MODEL), lambda i, j: (j, 0)),
                  vec, par, par, vec, vec],
        out_specs=[row, row],
        out_shape=[jax.ShapeDtypeStruct((rows, D_MODEL), F32),
                   jax.ShapeDtypeStruct((rows, D_MODEL), BF16)],
        scratch_shapes=[pltpu.VMEM((TM, D_MODEL), F32)],
        compiler_params=_params("parallel", "arbitrary"), name="half_ffn",
    )(hm, h, w1, w3, w2, gate, lng, lnb, nsh, nsc)


def _proj_kernel(x_ref, w_ref, o_ref):
    o_ref[...] = jnp.dot(x_ref[...], w_ref[...], preferred_element_type=F32).astype(o_ref.dtype)


def _proj(lay, hm, w, out_dtype, name):
    n = w.shape[1]
    tn = min(TN_PROJ, n)
    tm = TM_PROJ if lay.n_tok % TM_PROJ == 0 else TM
    return pl.pallas_call(
        _proj_kernel, grid=(lay.n_tok // tm, n // tn),
        in_specs=[pl.BlockSpec((tm, D_MODEL), lambda i, j: (i, 0)),
                  pl.BlockSpec((D_MODEL, tn), lambda i, j: (0, j))],
        out_specs=pl.BlockSpec((tm, tn), lambda i, j: (i, j)),
        out_shape=jax.ShapeDtypeStruct((lay.n_tok, n), out_dtype),
        compiler_params=_params("parallel", "parallel"), name=name,
    )(hm, w)


def _proj_qk_kernel(x_ref, w_ref, cos_ref, sin_ref, o_ref, *, n_lat_tiles, is_q):
    i = pl.program_id(0)
    acc = jnp.dot(x_ref[...], w_ref[...], preferred_element_type=F32)
    lane = lax.broadcasted_iota(jnp.int32, (acc.shape[0], LANES), 1)
    low_rows = lax.broadcasted_iota(jnp.int32, (LANES, acc.shape[0]), 0) < ATT_DH

    def emit(rotate):
        if rotate:
            cos, sin = cos_ref[...], sin_ref[...]
            first = (lane % (2 * ROPE_FREQS)) < ROPE_FREQS
        for h in range(acc.shape[1] // LANES):
            t = acc[:, h * LANES:(h + 1) * LANES]
            if rotate:
                partner = jnp.where(first, pltpu.roll(t, LANES - ROPE_FREQS, 1), pltpu.roll(t, ROPE_FREQS, 1))
                t = t * cos + partner * sin
            if is_q:
                tt = (t * (ATT_DH ** -0.5 * LOG2_E)).T
                o_ref[h, 0] = jnp.where(low_rows, tt, 0.0).astype(o_ref.dtype)
                o_ref[h, 1] = jnp.where(low_rows, 0.0, tt).astype(o_ref.dtype)
            else:
                o_ref[:, h * LANES:(h + 1) * LANES] = t.astype(o_ref.dtype)

    @pl.when(i < n_lat_tiles)
    def _():
        emit(True)

    @pl.when(i >= n_lat_tiles)
    def _():
        emit(False)


def _rope_tables(seq):
    rows = seq // GRID_W
    row = jnp.repeat(jnp.arange(rows), GRID_W)
    col = jnp.tile(jnp.arange(GRID_W), rows)
    inv = ROPE_BASE ** (-jnp.arange(ROPE_FREQS, dtype=F32) / ROPE_FREQS)
    ar, ac = row[:, None] * inv, col[:, None] * inv
    cos = jnp.concatenate([jnp.cos(ar), jnp.cos(ar), jnp.cos(ac), jnp.cos(ac)], axis=1)
    sin = jnp.concatenate([-jnp.sin(ar), jnp.sin(ar), -jnp.sin(ac), jnp.sin(ac)], axis=1)
    return jnp.tile(cos, (1, 2)), jnp.tile(sin, (1, 2))


def _proj_qk(lay, hm, w, cos, sin, is_q):
    tps = lay.seq // TM
    kern = functools.partial(_proj_qk_kernel, n_lat_tiles=lay.n_lat // TM, is_q=is_q)
    tab = pl.BlockSpec((TM, LANES), lambda i: (i % tps, 0))
    if is_q:
        out_spec = pl.BlockSpec((ATT_HEADS, 2, LANES, TM), lambda i: (0, 0, 0, i))
        out_shape = jax.ShapeDtypeStruct((ATT_HEADS, 2, LANES, lay.n_tok), BF16)
    else:
        out_spec = pl.BlockSpec((TM, BRANCH_W), lambda i: (i, 0))
        out_shape = jax.ShapeDtypeStruct((lay.n_tok, BRANCH_W), BF16)
    return pl.pallas_call(
        kern, grid=(lay.n_tok // TM,),
        in_specs=[pl.BlockSpec((TM, D_MODEL), lambda i: (i, 0)),
                  pl.BlockSpec((D_MODEL, BRANCH_W), lambda i: (0, 0)), tab, tab],
        out_specs=out_spec, out_shape=out_shape,
        compiler_params=_params("parallel"), name="proj_q" if is_q else "proj_k",
    )(hm, w, cos, sin)


def _proj_v_kernel(x_ref, w_ref, o_ref):
    acc = jnp.dot(x_ref[...], w_ref[...], preferred_element_type=F32)
    tail = ATT_V_ROWS - ATT_DV
    ones_row = jnp.where(lax.broadcasted_iota(jnp.int32, (tail, acc.shape[0]), 0) == 0, 1.0, 0.0)
    for h in range(acc.shape[1] // LANES):
        o_ref[h, :ATT_DV, :] = acc[:, h * LANES:(h + 1) * LANES].T.astype(o_ref.dtype)
        o_ref[h, ATT_DV:, :] = ones_row.astype(o_ref.dtype)


def _proj_v(lay, hm, w):
    return pl.pallas_call(
        _proj_v_kernel, grid=(lay.n_tok // TM,),
        in_specs=[pl.BlockSpec((TM, D_MODEL), lambda i: (i, 0)),
                  pl.BlockSpec((D_MODEL, BRANCH_W), lambda i: (0, 0))],
        out_specs=pl.BlockSpec((ATT_HEADS, ATT_V_ROWS, TM), lambda i: (0, 0, i)),
        out_shape=jax.ShapeDtypeStruct((ATT_HEADS, ATT_V_ROWS, lay.n_tok), BF16),
        compiler_params=_params("parallel"), name="proj_v",
    )(hm, w)


def _attn_kernel(lamv_ref, subln_ref, q0_ref, q1_ref, kc_ref, vc_ref, *rest, n_lat_chunks, tk, lam_init):
    if n_lat_chunks:
        kl_ref, vl_ref, o_ref = rest
    else:
        _, o_ref = rest
    lv = lamv_ref[...]
    lam = (jnp.exp(jnp.sum(lv[0:1] * lv[1:2], axis=-1, keepdims=True))
           - jnp.exp(jnp.sum(lv[2:3] * lv[3:4], axis=-1, keepdims=True)) + lam_init)

    def first(qt, k, vt):
        s = jnp.dot(k, qt, preferred_element_type=F32)
        m = jnp.max(s, axis=0, keepdims=True)
        return m, jnp.dot(vt, jnp.exp2(s - m).astype(BF16), preferred_element_type=F32)

    def update(state, qt, k, vt):
        m, acc = state
        s = jnp.dot(k, qt, preferred_element_type=F32)
        m_new = jnp.maximum(m, jnp.max(s, axis=0, keepdims=True))
        e = jnp.exp2(s - m_new).astype(BF16)
        return m_new, jnp.exp2(m - m_new) * acc + jnp.dot(vt, e, preferred_element_type=F32)

    qts = (q0_ref[...], q1_ref[...])
    kc, vtc = kc_ref[...], vc_ref[...]
    states = tuple(first(qt, kc, vtc) for qt in qts)
    if n_lat_chunks:
        def body(c, st):
            off = pl.multiple_of(c * tk, tk)
            k, vt = kl_ref[pl.ds(off, tk), :], vl_ref[:, pl.ds(off, tk)]
            return tuple(update(s_, qt, k, vt) for s_, qt in zip(st, qts))

        states = lax.fori_loop(0, n_lat_chunks, body, states)
    outs = [a[:ATT_DV] / a[ATT_DV:ATT_DV + 1] for _, a in states]
    o = outs[0] - lam * outs[1]
    o = o * lax.rsqrt(jnp.mean(o * o, axis=0, keepdims=True) + RMS_EPS) * subln_ref[...] * (1.0 - lam_init)
    o_ref[...] = o.T.astype(o_ref.dtype)


def _diff_attention(lay, qt, k, vt, att_lam, subln, lam_init):
    b_, s_, c_ = lay.batch, lay.seq, lay.ctx
    lamv = pl.BlockSpec((4, ATT_DH), lambda *_: (0, 0))
    sub = pl.BlockSpec((ATT_DV, 1), lambda *_: (0, 0))
    out_shape = jax.ShapeDtypeStruct((lay.n_tok, BRANCH_W), BF16)
    ctx_blk = lay.n_lat // c_

    def specs(tq, q_blk):
        return [lamv, sub,
                pl.BlockSpec((None, None, LANES, tq), lambda b, h, i: (h, 0, 0, q_blk(b, i))),
                pl.BlockSpec((None, None, LANES, tq), lambda b, h, i: (h, 1, 0, q_blk(b, i))),
                pl.BlockSpec((c_, LANES), lambda b, h, i: (ctx_blk + b, h)),
                pl.BlockSpec((None, ATT_V_ROWS, c_), lambda b, h, i: (h, 0, ctx_blk + b))]

    tq, tk = min(ATT_TQ, s_), min(ATT_TK, s_)
    n_q = s_ // tq
    lat_blk = lambda b, i: b * n_q + i
    lat = pl.pallas_call(
        functools.partial(_attn_kernel, n_lat_chunks=s_ // tk, tk=tk, lam_init=lam_init),
        grid=(b_, ATT_HEADS, n_q),
        in_specs=specs(tq, lat_blk) + [
            pl.BlockSpec((s_, LANES), lambda b, h, i: (b, h)),
            pl.BlockSpec((None, ATT_V_ROWS, s_), lambda b, h, i: (h, 0, b))],
        out_specs=pl.BlockSpec((tq, LANES), lambda b, h, i: (lat_blk(b, i), h)),
        out_shape=out_shape,
        compiler_params=_params("parallel", "parallel", "arbitrary"), name="attn_latent",
    )(att_lam, subln, qt, qt, k, vt, k, vt)
    ctx_q = lambda b, i: ctx_blk + b
    return pl.pallas_call(
        functools.partial(_attn_kernel, n_lat_chunks=0, tk=0, lam_init=lam_init),
        grid=(b_, ATT_HEADS, 1),
        in_specs=specs(c_, ctx_q) + [pl.BlockSpec(memory_space=pl.ANY)],
        out_specs=pl.BlockSpec((c_, LANES), lambda b, h, i: (ctx_q(b, i), h)),
        out_shape=out_shape, input_output_aliases={6: 0},
        compiler_params=_params("parallel", "parallel", "arbitrary"), name="attn_context",
    )(att_lam, subln, qt, qt, k, vt, lat)


def _ssd_prep_kernel(prev_ref, cur_ref, next_ref, dt_ref, cw_ref, cb_ref, dtb_ref,
                     xs_ref, bm_ref, cm_ref, dts_ref, dtst_ref, ext_ref, *, tiles_per_seq, n_lat_tiles):
    i = pl.program_id(0)
    is_ctx = i >= n_lat_tiles
    first = jnp.logical_or(is_ctx, i % tiles_per_seq == 0)
    last = jnp.logical_or(is_ctx, i % tiles_per_seq == tiles_per_seq - 1)
    tm = cur_ref.shape[0]
    ext_ref[0:SUBLANES, :] = jnp.where(first, 0.0, prev_ref[...])
    ext_ref[SUBLANES:SUBLANES + tm, :] = cur_ref[...]
    ext_ref[SUBLANES + tm:, :] = jnp.where(last, 0.0, next_ref[...])
    acc = jnp.zeros(cur_ref.shape, F32) + cb_ref[...]
    for k in range(SSD_CONV):
        start = SUBLANES + k - SSD_CONV // 2
        acc = acc + ext_ref[start:start + tm, :] * cw_ref[k:k + 1, :]
    act = acc * jax.nn.sigmoid(acc)
    xs_ref[...] = act[:, :SSD_INNER]
    bm_ref[...] = act[:, SSD_INNER:SSD_INNER + SSD_BC_W].astype(bm_ref.dtype)
    cm_ref[...] = act[:, SSD_INNER + SSD_BC_W:].astype(cm_ref.dtype)
    x = dt_ref[...] + dtb_ref[...]
    sp = jnp.maximum(x, 0.0) + jnp.log1p(jnp.exp(-jnp.abs(x)))
    dts_ref[0] = sp
    dts_ref[1] = pltpu.roll(sp, DT_PAD - SSD_HEADS, 1)
    spt = sp.T
    dtst_ref[0] = spt[0:SSD_HEADS]
    dtst_ref[1] = spt[SSD_HEADS:2 * SSD_HEADS]


def _ssd_prep(lay, xbc, dt, conv_w, conv_b, dt_bias):
    n, tm = lay.n_tok, SSD_TM
    sub_per_tile = tm // SUBLANES
    n_sub = n // SUBLANES
    kern = functools.partial(_ssd_prep_kernel, tiles_per_seq=lay.seq // tm, n_lat_tiles=lay.n_lat // tm)
    row = lambda w: pl.BlockSpec((tm, w), lambda i: (i, 0))
    return pl.pallas_call(
        kern, grid=(n // tm,),
        in_specs=[pl.BlockSpec((SUBLANES, SSD_XBC_W), lambda i: (jnp.maximum(i * sub_per_tile - 1, 0), 0)),
                  row(SSD_XBC_W),
                  pl.BlockSpec((SUBLANES, SSD_XBC_W),
                               lambda i: (jnp.minimum((i + 1) * sub_per_tile, n_sub - 1), 0)),
                  row(DT_PAD),
                  pl.BlockSpec((SUBLANES, SSD_XBC_W), lambda i: (0, 0)),
                  pl.BlockSpec((1, SSD_XBC_W), lambda i: (0, 0)),
                  pl.BlockSpec((1, DT_PAD), lambda i: (0, 0))],
        out_specs=[row(SSD_INNER), row(SSD_BC_W), row(SSD_BC_W),
                   pl.BlockSpec((2, tm, DT_PAD), lambda i: (0, i, 0)),
                   pl.BlockSpec((2, SSD_HEADS, tm), lambda i: (0, 0, i))],
        out_shape=[jax.ShapeDtypeStruct((n, SSD_INNER), F32),
                   jax.ShapeDtypeStruct((n, SSD_BC_W), BF16),
                   jax.ShapeDtypeStruct((n, SSD_BC_W), BF16),
                   jax.ShapeDtypeStruct((2, n, DT_PAD), F32),
                   jax.ShapeDtypeStruct((2, SSD_HEADS, n), F32)],
        scratch_shapes=[pltpu.VMEM((tm + 2 * SUBLANES, SSD_XBC_W), F32)],
        compiler_params=_params("parallel"), name="ssd_prep",
    )(xbc, xbc, xbc, dt, conv_w, conv_b, dt_bias)


def _split_dot(a, x, x_is_lhs=False):
    out = None
    r = x
    for _ in range(3):
        t = r.astype(BF16)
        r = r - t.astype(F32)
        d = (jnp.dot(t, a, preferred_element_type=F32) if x_is_lhs
             else jnp.dot(a, t, preferred_element_type=F32))
        out = d if out is None else out + d
    return out


def _ssd_kernel(xs_ref, bm_ref, cm_ref, dts_ref, dtst_ref, arow_ref, acol_ref, y_ref, h_ref):
    sign = 1 - 2 * pl.program_id(1)

    @pl.when(pl.program_id(2) == 0)
    def _():
        h_ref[...] = jnp.zeros(h_ref.shape, F32)

    ll = xs_ref.shape[0]
    r = lax.broadcasted_iota(jnp.int32, (ll, ll), 0)
    c = lax.broadcasted_iota(jnp.int32, (ll, ll), 1)
    mask = (r - c) * sign >= 0
    mask_t = (r - c) * sign <= 0
    one_hot = lambda m: jnp.where(m, 1.0, 0.0).astype(BF16)
    dts_t = dtst_ref[...]
    da = dts_ref[...] * arow_ref[...]
    da_t = dts_t * acol_ref[...]
    cs = _split_dot(one_hot(mask), da)
    cs_t = _split_dot(one_hot(mask_t), da_t, x_is_lhs=True)
    tot = jnp.sum(da, axis=0, keepdims=True)
    from_start = jnp.exp(cs)
    chunk_decay = jnp.exp(tot)
    src_w_t = dts_t * jnp.exp(jnp.sum(da_t, axis=1, keepdims=True) - cs_t)
    lane = lax.broadcasted_iota(jnp.int32, (ll, LANES), 1)
    lo_half = lane < SSD_P

    def pair_expand(t, hd):
        rows = t.shape[0]
        return jnp.where(lo_half[:rows], jnp.broadcast_to(t[:, hd:hd + 1], (rows, LANES)),
                         jnp.broadcast_to(t[:, hd + 1:hd + 2], (rows, LANES)))

    gw = SSD_HPG * SSD_P
    for g in range(SSD_GROUPS):
        bg = bm_ref[:, g * SSD_N:(g + 1) * SSD_N]
        cg = cm_ref[:, g * SSD_N:(g + 1) * SSD_N]
        cb = lax.dot_general(cg, bg, (((1,), (1,)), ((), ())), preferred_element_type=F32)
        bg_t = bg.astype(F32).T
        y_parts, st_parts, fs_parts, cd_parts = [], [], [], []
        for pr in range(SSD_HPG // 2):
            hd = g * SSD_HPG + 2 * pr
            xp = xs_ref[:, hd * SSD_P:(hd + 2) * SSD_P]
            fs_parts.append(pair_expand(from_start, hd))
            cd_parts.append(pair_expand(chunk_decay, hd))
            yp = sp = None
            for k in range(2):
                e = hd + k
                seg = cs[:, e:e + 1] - cs_t[e:e + 1, :]
                dec = jnp.where(mask, jnp.exp(jnp.where(mask, seg, 0.0)), 0.0)
                w = (cb * dec * dts_t[e:e + 1, :]).astype(BF16)
                xk = jnp.where(lo_half if k == 0 else jnp.logical_not(lo_half), xp, 0.0).astype(BF16)
                d = jnp.dot(w, xk, preferred_element_type=F32)
                yp = d if yp is None else yp + d
                s = jnp.dot((bg_t * src_w_t[e:e + 1, :]).astype(BF16), xk, preferred_element_type=F32)
                sp = s if sp is None else sp + s
            y_parts.append(yp)
            st_parts.append(sp)
        h_prev = h_ref[g]
        y_off = jnp.dot(cg, h_prev.astype(BF16), preferred_element_type=F32) * jnp.concatenate(fs_parts, axis=1)
        y_ref[:, g * gw:(g + 1) * gw] = jnp.concatenate(y_parts, axis=1) + y_off
        h_ref[g] = h_prev * jnp.concatenate(cd_parts, axis=1) + jnp.concatenate(st_parts, axis=1)


def _ssd_scan(lay, xs, bm, cm, dts, dtst, a_row, a_col):
    ll = SSD_L
    n_ctx, n_lat = lay.ctx // ll, lay.seq // ll
    ctx_base = lay.n_lat // ll

    def rb(b, d, s):
        cstep = jnp.where(d == 0, s, n_ctx - 1 - s)
        lstep = jnp.where(d == 0, s - n_ctx, n_lat - 1 - (s - n_ctx))
        return jnp.where(s < n_ctx, ctx_base + b * n_ctx + cstep, b * n_lat + lstep)

    row = lambda w: pl.BlockSpec((ll, w), lambda b, d, s: (rb(b, d, s), 0))
    return pl.pallas_call(
        _ssd_kernel, grid=(lay.batch, 2, n_ctx + n_lat),
        in_specs=[row(SSD_INNER), row(SSD_BC_W), row(SSD_BC_W),
                  pl.BlockSpec((None, ll, DT_PAD), lambda b, d, s: (d, rb(b, d, s), 0)),
                  pl.BlockSpec((None, SSD_HEADS, ll), lambda b, d, s: (d, 0, rb(b, d, s))),
                  pl.BlockSpec((None, 1, DT_PAD), lambda b, d, s: (d, 0, 0)),
                  pl.BlockSpec((None, SSD_HEADS, 1), lambda b, d, s: (d, 0, 0))],
        out_specs=pl.BlockSpec((None, ll, SSD_INNER), lambda b, d, s: (d, rb(b, d, s), 0)),
        out_shape=jax.ShapeDtypeStruct((2, lay.n_tok, SSD_INNER), F32),
        scratch_shapes=[pltpu.VMEM((SSD_GROUPS, SSD_N, SSD_HPG * SSD_P), F32)],
        compiler_params=_params("parallel", "parallel", "arbitrary"), name="ssd_scan",
    )(xs, bm, cm, dts, dtst, a_row, a_col)


def _ssd_gate_norm_kernel(y0_ref, y1_ref, xs_ref, z_ref, dsk_ref, nw_ref, o_ref):
    z = z_ref[...]
    gated = (y0_ref[...] + y1_ref[...] + xs_ref[...] * dsk_ref[...]) * (z * jax.nn.sigmoid(z))
    nw = nw_ref[...]
    for g in range(SSD_GROUPS):
        sl = slice(g * SSD_NORM_GROUP, (g + 1) * SSD_NORM_GROUP)
        t = gated[:, sl]
        o_ref[:, sl] = (t * lax.rsqrt(jnp.mean(t * t, axis=-1, keepdims=True) + RMS_EPS)
                        * nw[:, sl]).astype(o_ref.dtype)


def _ssd_gate_norm(lay, ydir, xs, z, dsk, nw):
    tm = SSD_TM
    row = pl.BlockSpec((tm, SSD_INNER), lambda i: (i, 0))
    par = pl.BlockSpec((1, SSD_INNER), lambda i: (0, 0))
    return pl.pallas_call(
        _ssd_gate_norm_kernel, grid=(lay.n_tok // tm,),
        in_specs=[pl.BlockSpec((None, tm, SSD_INNER), lambda i: (0, i, 0)),
                  pl.BlockSpec((None, tm, SSD_INNER), lambda i: (1, i, 0)), row, row, par, par],
        out_specs=row, out_shape=jax.ShapeDtypeStruct((lay.n_tok, SSD_INNER), BF16),
        compiler_params=_params("parallel"), name="ssd_gate_norm",
    )(ydir, ydir, xs, z, dsk, nw)


def _s5_operators(lam_re, lam_im, log_step, b_re, b_im, c_re, c_im):
    hp = lax.Precision.HIGHEST
    ll, hh = S5_L, S5_GROUP_CH
    step = jnp.exp(log_step)[..., None, None]
    d = jnp.arange(ll + 1, dtype=F32)
    p_mag = jnp.exp(lam_re[..., None] * step * d)
    p_ang = lam_im[..., None] * step * d
    p_re, p_im = p_mag * jnp.cos(p_ang), p_mag * jnp.sin(p_ang)
    ab_re, ab_im = p_re[..., 1], p_im[..., 1]
    den = lam_re * lam_re + lam_im * lam_im
    k_re = ((ab_re - 1.0) * lam_re + ab_im * lam_im) / den
    k_im = (ab_im * lam_re - (ab_re - 1.0) * lam_im) / den
    bb_re = k_re[..., None] * b_re - k_im[..., None] * b_im
    bb_im = k_re[..., None] * b_im + k_im[..., None] * b_re
    cp_re = c_re[..., None] * p_re[:, :, None] - c_im[..., None] * p_im[:, :, None]
    cp_im = c_re[..., None] * p_im[:, :, None] + c_im[..., None] * p_re[:, :, None]
    kern = (jnp.einsum('zghnd,zgnk->zgdhk', cp_re, bb_re, precision=hp)
            - jnp.einsum('zghnd,zgnk->zgdhk', cp_im, bb_im, precision=hp))
    s_idx = jnp.arange(ll)[:, None]
    l_idx = jnp.arange(ll)[None, :]
    t_f = jnp.where((l_idx >= s_idx)[None, :, :, None, None], kern[0][:, jnp.clip(l_idx - s_idx, 0, ll)], 0.0)
    t_b = jnp.where((s_idx >= l_idx)[None, :, :, None, None], kern[1][:, jnp.clip(s_idx - l_idx, 0, ll)], 0.0)
    toep = (t_f + t_b).transpose(0, 1, 4, 2, 3).reshape(S5_GROUPS, S5_CW, S5_CW)

    def state_in(z, powers):
        pr, pi = p_re[z][..., powers], p_im[z][..., powers]
        re = pr[..., None] * bb_re[z][:, :, None] - pi[..., None] * bb_im[z][:, :, None]
        im = pr[..., None] * bb_im[z][:, :, None] + pi[..., None] * bb_re[z][:, :, None]
        re = re.transpose(0, 2, 3, 1).reshape(S5_GROUPS, S5_CW, S5_N)
        im = im.transpose(0, 2, 3, 1).reshape(S5_GROUPS, S5_CW, S5_N)
        return jnp.concatenate([re, im], axis=-1)

    ws_f = state_in(0, ll - 1 - jnp.arange(ll))
    ws_b = state_in(1, jnp.arange(ll))
    w1 = jnp.concatenate([toep, ws_f, ws_b], axis=-1)

    def state_out(z, powers):
        re = cp_re[z][..., powers].transpose(0, 2, 3, 1).reshape(S5_GROUPS, S5_N, S5_CW)
        im = cp_im[z][..., powers].transpose(0, 2, 3, 1).reshape(S5_GROUPS, S5_N, S5_CW)
        return jnp.concatenate([re, -im], axis=1)

    wo = jnp.concatenate([state_out(0, jnp.arange(ll) + 1), state_out(1, ll - jnp.arange(ll))], axis=1)
    ar, ai = p_re[..., ll], p_im[..., ll]
    a1 = jnp.concatenate([ar, ar], axis=-1)
    a2 = jnp.concatenate([-ai, ai], axis=-1)
    zeros = jnp.zeros_like(a1[0])
    av = jnp.stack([a1[0], a2[0], a1[1], a2[1], zeros, zeros, zeros, zeros], axis=1)
    return w1.astype(BF16), wo.astype(BF16), av


def _s5_kernel(x_ref, w1_ref, wo_ref, av_ref, y_ref, u_scr, y_scr, sf_ref, sb_ref, *, n_ctx_tiles, n_tiles):
    rows = x_ref.shape[0]
    row_tile = rows // S5_ROW_TILES
    nst = 2 * S5_N
    lane_blk = lax.broadcasted_iota(jnp.int32, (row_tile, LANES), 1) // S5_GROUP_CH

    def block_transpose(arrs):
        a = list(arrs)
        k = S5_GB // 2
        while k:
            bit = (lane_blk & k) != 0
            for i in range(S5_GB):
                if not i & k:
                    lo, hi = a[i], a[i + k]
                    a[i] = jnp.where(bit, pltpu.roll(hi, k * S5_GROUP_CH, 1), lo)
                    a[i + k] = jnp.where(bit, hi, pltpu.roll(lo, LANES - k * S5_GROUP_CH, 1))
            k //= 2
        return a

    def gather(r, _):
        r0 = pl.multiple_of(r * row_tile, BF16_ROWS)
        for hv in range(S5_L // S5_GB):
            per_group = block_transpose(
                [x_ref[pl.ds(r0, row_tile), hv * S5_GB + sl, :] for sl in range(S5_GB)])
            for g in range(S5_GB):
                u_scr[g, pl.ds(r0, row_tile), hv * LANES:(hv + 1) * LANES] = per_group[g].astype(u_scr.dtype)
        return 0

    lax.fori_loop(0, S5_ROW_TILES, gather, 0)

    for g in range(S5_GB):
        p = jnp.dot(u_scr[g], w1_ref[g], preferred_element_type=F32)
        y_scr[g] = p[:, :S5_CW]
        sf_ref[:, g * nst:(g + 1) * nst] = p[:, S5_CW:S5_CW + nst]
        sb_ref[:, g * nst:(g + 1) * nst] = p[:, S5_CW + nst:]

    wide = S5_GB * nst
    half = SUBLANES // 2
    coef = lambda k: jnp.broadcast_to(
        jnp.concatenate([av_ref[g][k:k + 1] for g in range(S5_GB)], axis=1), (half, wide))
    a1f, a2f, a1b, a2b = coef(0), coef(1), coef(2), coef(3)
    low = lax.broadcasted_iota(jnp.int32, (SUBLANES, wide), 1) % nst < S5_N
    swap = lambda t: jnp.where(low, pltpu.roll(t, wide - S5_N, 1), pltpu.roll(t, S5_N, 1))

    def body(j, carry):
        hf, hfs, hb, hbs = carry
        of = pl.multiple_of(j * SUBLANES, SUBLANES)
        s = sf_ref[pl.ds(of, SUBLANES), :]
        ss = swap(s)
        h1 = a1f * hf + a2f * hfs + s[:half]
        h1s = a1f * hfs - a2f * hf + ss[:half]
        sf_ref[pl.ds(of, SUBLANES), :] = jnp.concatenate([hf, h1], axis=0)
        h2 = a1f * h1 + a2f * h1s + s[half:]
        h2s = a1f * h1s - a2f * h1 + ss[half:]
        jb = jnp.where(j < n_ctx_tiles, n_ctx_tiles - 1 - j, n_tiles - 1 - (j - n_ctx_tiles))
        ob = pl.multiple_of(jb * SUBLANES, SUBLANES)
        s = sb_ref[pl.ds(ob, SUBLANES), :]
        ss = swap(s)
        g1 = a1b * hb + a2b * hbs + s[half:]
        g1s = a1b * hbs - a2b * hb + ss[half:]
        sb_ref[pl.ds(ob, SUBLANES), :] = jnp.concatenate([g1, hb], axis=0)
        g2 = a1b * g1 + a2b * g1s + s[:half]
        g2s = a1b * g1s - a2b * g1 + ss[:half]
        return h2, h2s, g2, g2s

    z = jnp.zeros((half, wide), F32)
    lax.fori_loop(0, n_tiles, body, (z, z, z, z))

    for g in range(S5_GB):
        wo = wo_ref[g]
        y_scr[g] += (
            jnp.dot(sf_ref[:, g * nst:(g + 1) * nst].astype(BF16), wo[:nst], preferred_element_type=F32)
            + jnp.dot(sb_ref[:, g * nst:(g + 1) * nst].astype(BF16), wo[nst:], preferred_element_type=F32))

    def scatter(r, _):
        r0 = pl.multiple_of(r * row_tile, BF16_ROWS)
        for hv in range(S5_L // S5_GB):
            per_lag = block_transpose(
                [y_scr[g, pl.ds(r0, row_tile), hv * LANES:(hv + 1) * LANES] for g in range(S5_GB)])
            for sl in range(S5_GB):
                y_ref[pl.ds(r0, row_tile), hv * S5_GB + sl, :] = per_lag[sl]
        return 0

    lax.fori_loop(0, S5_ROW_TILES, scatter, 0)


def _s5_chunked(lay, x, w1, wo, av):
    rows = lay.n_tok // S5_L
    n_ctx_rows = lay.batch * lay.ctx // S5_L
    assert rows % (S5_ROW_TILES * BF16_ROWS) == 0
    kern = functools.partial(_s5_kernel, n_ctx_tiles=n_ctx_rows // SUBLANES, n_tiles=rows // SUBLANES)
    blk = pl.BlockSpec((rows, None, S5_L, LANES), lambda g: (0, g, 0, 0), pipeline_mode=pl.Buffered(1))
    return pl.pallas_call(
        kern, grid=(S5_NB,),
        in_specs=[blk,
                  pl.BlockSpec((S5_GB, S5_CW, S5_CW + 4 * S5_N), lambda g: (g, 0, 0)),
                  pl.BlockSpec((S5_GB, 4 * S5_N, S5_CW), lambda g: (g, 0, 0)),
                  pl.BlockSpec((S5_GB, SUBLANES, 2 * S5_N), lambda g: (g, 0, 0))],
        out_specs=blk,
        out_shape=jax.ShapeDtypeStruct((rows, S5_NB, S5_L, LANES), F32),
        scratch_shapes=[pltpu.VMEM((S5_GB, rows, S5_CW), BF16), pltpu.VMEM((S5_GB, rows, S5_CW), F32),
                        pltpu.VMEM((rows, S5_GB * 2 * S5_N), F32), pltpu.VMEM((rows, S5_GB * 2 * S5_N), F32)],
        compiler_params=_params("parallel"), name="s5_chunked",
    )(x, w1, wo, av)


def _s5_tile_index(lay, i):
    n_lat_tiles, per_seq = lay.n_lat // S5_TM, lay.seq // S5_TM
    return jnp.where(i < n_lat_tiles, 1 + i % per_seq, 0), jnp.where(i < n_lat_tiles, i // per_seq, i - n_lat_tiles)


def _s5_chunk_spec(lay):
    return pl.BlockSpec((S5_TM // S5_L, None, S5_NB, S5_L, LANES),
                        lambda i: (*_s5_tile_index(lay, i), 0, 0, 0))


def _proj_u_kernel(x_ref, w_ref, o_ref):
    acc = jnp.dot(x_ref[...], w_ref[...], preferred_element_type=F32)
    for nb in range(S5_NB):
        o_ref[:, nb] = acc[:, nb * LANES:(nb + 1) * LANES].reshape(S5_TM // S5_L, S5_L, LANES)


def _proj_u(lay, hm, w):
    chunks = (lay.seq + lay.ctx) // S5_L
    return pl.pallas_call(
        _proj_u_kernel, grid=(lay.n_tok // S5_TM,),
        in_specs=[pl.BlockSpec((S5_TM, D_MODEL), lambda i: (i, 0)),
                  pl.BlockSpec((D_MODEL, S5_CH), lambda i: (0, 0))],
        out_specs=_s5_chunk_spec(lay),
        out_shape=jax.ShapeDtypeStruct((chunks, lay.batch, S5_NB, S5_L, LANES), F32),
        compiler_params=_params("parallel"), name="proj_u",
    )(hm, w)


def _s5_glu_kernel(ys_ref, u_ref, dsk_ref, w_ref, b_ref, o_ref):
    natural = lambda ref: jnp.concatenate([ref[:, nb].reshape(S5_TM, LANES) for nb in range(S5_NB)], axis=1)
    t = natural(ys_ref) + natural(u_ref) * dsk_ref[...]
    t = 0.5 * t * (1.0 + jnp.tanh(math.sqrt(2.0 / math.pi) * (t + 0.044715 * (t * t * t))))
    gate = jnp.dot(t.astype(BF16), w_ref[...], preferred_element_type=F32) + b_ref[...]
    o_ref[...] = (t * jax.nn.sigmoid(gate)).astype(o_ref.dtype)


def _s5_glu(lay, ys, u, dsk, w, b):
    par = pl.BlockSpec((1, S5_CH), lambda i: (0, 0))
    return pl.pallas_call(
        _s5_glu_kernel, grid=(lay.n_tok // S5_TM,),
        in_specs=[_s5_chunk_spec(lay), _s5_chunk_spec(lay), par,
                  pl.BlockSpec((S5_CH, S5_CH), lambda i: (0, 0)), par],
        out_specs=pl.BlockSpec((S5_TM, S5_CH), lambda i: (i, 0)),
        out_shape=jax.ShapeDtypeStruct((lay.n_tok, S5_CH), BF16),
        compiler_params=_params("parallel"), name="s5_glu",
    )(ys, u, dsk, w, b)


def _merge_kernel(oa_ref, os_ref, o5_ref, ga_ref, gs_ref, g5_ref, wa_ref, ws_ref, w5_ref, o_ref):
    acc = None
    for o, g, w in ((oa_ref, ga_ref, wa_ref), (os_ref, gs_ref, ws_ref), (o5_ref, g5_ref, w5_ref)):
        t = jax.nn.sigmoid(g[...]) * jnp.dot(o[...], w[...], preferred_element_type=F32)
        acc = t if acc is None else acc + t
    o_ref[...] = acc.astype(o_ref.dtype)


def _merge(rows, o_att, o_ssd, o_s5, g, w_branch):
    nt = D_MODEL // TN_MERGE
    row = pl.BlockSpec((TM, BRANCH_W), lambda i, j: (i, 0))
    gate = lambda k: pl.BlockSpec((TM, TN_MERGE), lambda i, j: (i, k * nt + j))
    wb = lambda k: pl.BlockSpec((None, BRANCH_W, TN_MERGE), lambda i, j: (k, 0, j))
    return pl.pallas_call(
        _merge_kernel, grid=(rows // TM, nt),
        in_specs=[row, row, row, gate(0), gate(1), gate(2), wb(0), wb(1), wb(2)],
        out_specs=pl.BlockSpec((TM, TN_MERGE), lambda i, j: (i, j)),
        out_shape=jax.ShapeDtypeStruct((rows, D_MODEL), BF16),
        compiler_params=_params("parallel", "parallel"), name="branch_merge",
    )(o_att, o_ssd, o_s5, g, g, g, w_branch, w_branch, w_branch)


def _out_norm_kernel(mx_ref, h_ref, w_ref, gate_ref, lng_ref, lnb_ref, nsh_ref, nsc_ref, ho_ref, hmo_ref):
    y = jnp.dot(mx_ref[...], w_ref[...], preferred_element_type=F32)
    _post_norm_emit(h_ref[...], gate_ref[...] * y, lng_ref[...], lnb_ref[...], nsh_ref[...], nsc_ref[...],
                    ho_ref, hmo_ref)


def _out_norm(lay, rows, mixed, h, w_out, gate, lng, lnb, nsh, nsc):
    vec = pl.BlockSpec((None, 1, D_MODEL), lambda i: (lay.sample_of_tile(i, TM), 0, 0))
    par = pl.BlockSpec((1, D_MODEL), lambda i: (0, 0))
    row = pl.BlockSpec((TM, D_MODEL), lambda i: (i, 0))
    return pl.pallas_call(
        _out_norm_kernel, grid=(rows // TM,),
        in_specs=[row, row, pl.BlockSpec((D_MODEL, D_MODEL), lambda i: (0, 0)), vec, par, par, vec, vec],
        out_specs=[row, row],
        out_shape=[jax.ShapeDtypeStruct((rows, D_MODEL), F32),
                   jax.ShapeDtypeStruct((rows, D_MODEL), BF16)],
        compiler_params=_params("parallel"), name="out_norm",
    )(mixed, h, w_out, gate, lng, lnb, nsh, nsc)


def _pad_cols(t, width):
    return jnp.pad(t, [(0, 0)] * (t.ndim - 1) + [(0, width - t.shape[-1])])


def _token_mixer(lay, rows_out, hm, rope, lam_init, w_in, att_lam, att_subln, conv_w, conv_b, a_log, dt_bias,
                 ssd_d, ssd_norm, s5_ops, s5_d, glu_w, glu_b, w_branch):
    cuts = [0]
    for w in (BRANCH_W, BRANCH_W, BRANCH_W, SSD_INNER, SSD_XBC_W, 2 * SSD_HEADS, S5_CH, N_BRANCH * D_MODEL):
        cuts.append(cuts[-1] + w)
    w_q, w_k, w_v, w_z, w_xbc, w_dt, w_u, w_g = (
        w_in[:, a:b].astype(BF16) for a, b in zip(cuts[:-1], cuts[1:]))
    qt = _proj_qk(lay, hm, w_q, *rope, is_q=True)
    k = _proj_qk(lay, hm, w_k, *rope, is_q=False)
    vt = _proj_v(lay, hm, w_v)
    z = _proj(lay, hm, w_z, F32, "proj_z")
    xbc = _proj(lay, hm, w_xbc, F32, "proj_xbc")
    dt = _proj(lay, hm, _pad_cols(w_dt, DT_PAD), F32, "proj_dt")
    u = _proj_u(lay, hm, w_u)
    g = _proj(lay, hm, w_g, F32, "proj_gate")

    o_att = _diff_attention(lay, qt, k, vt, att_lam, att_subln.reshape(ATT_DV, 1), lam_init)

    conv_w8 = jnp.pad(conv_w, ((0, SUBLANES - SSD_CONV), (0, 0)))
    xs, bm, cm, dts, dtst = _ssd_prep(lay, xbc, dt, conv_w8, conv_b.reshape(1, -1),
                                      _pad_cols(dt_bias.reshape(1, -1), DT_PAD))
    a = -jnp.exp(a_log.astype(F32))
    ydir = _ssd_scan(lay, xs, bm, cm, dts, dtst, _pad_cols(a, DT_PAD)[:, None, :], a[:, :, None])
    o_ssd = _ssd_gate_norm(lay, ydir, xs, z, jnp.repeat(ssd_d, SSD_P).reshape(1, -1),
                           ssd_norm.reshape(1, -1))

    ys = _s5_chunked(lay, u.reshape(lay.n_tok // S5_L, S5_NB, S5_L, LANES), *s5_ops)
    o_s5 = _s5_glu(lay, ys.reshape(u.shape), u, s5_d.reshape(1, -1), glu_w.astype(BF16),
                   glu_b.reshape(1, -1))

    return _merge(rows_out, o_att, o_ssd, o_s5, g, w_branch.astype(BF16))


def _trunk(lay, x, c, ctx, c_ctx, w_mod, b_mod, ln_g, ln_b, ffn_w1, ffn_w3, ffn_w2, w_in,
           att_lam, att_subln, ssd_conv_w, ssd_conv_b, ssd_a_log, ssd_dt_bias, ssd_d, ssd_norm,
           s5_lam_re, s5_lam_im, s5_log_step, s5_b_re, s5_b_im, s5_c_re, s5_c_im,
           s5_d, s5_glu_w, s5_glu_b, w_branch, w_out):
    depth = w_mod.shape[0]
    h = jnp.concatenate([x.reshape(lay.n_lat, D_MODEL), ctx.reshape(-1, D_MODEL)], axis=0)
    cc = jnp.concatenate([c, c_ctx[None], jnp.zeros((MOD_ROWS - lay.batch - 1, D_MODEL), F32)], axis=0)
    mod = _mod_all(cc, w_mod, b_mod).reshape(depth, MOD_ROWS, N_MOD, 1, D_MODEL)
    mvec = lambda l, k: mod[l, :, k]
    zero_vec = jnp.zeros((MOD_ROWS, 1, D_MODEL), F32)
    rope = _rope_tables(lay.seq)
    lnp = lambda l, k: (ln_g[l, k].reshape(1, -1), ln_b[l, k].reshape(1, -1))

    hm = _modulate(lay, h, mvec(0, 0), mvec(0, 1))
    for l in range(depth):
        lam_init = LAMBDA_INIT_BASE - LAMBDA_INIT_SPAN * math.exp(-LAMBDA_INIT_RATE * l)
        last = l + 1 == depth
        rows = lay.n_lat if last else lay.n_tok
        h, hm = _half_ffn(lay, lay.n_tok, hm, h, ffn_w1[l, 0].astype(BF16), ffn_w3[l, 0].astype(BF16),
                          ffn_w2[l, 0].astype(BF16), mvec(l, 2), *lnp(l, 0), mvec(l, 3), mvec(l, 4))
        s5_ops = _s5_operators(s5_lam_re[l], s5_lam_im[l], s5_log_step[l], s5_b_re[l], s5_b_im[l],
                               s5_c_re[l], s5_c_im[l])
        mixed = _token_mixer(lay, rows, hm, rope, lam_init, w_in[l], att_lam[l], att_subln[l],
                             ssd_conv_w[l], ssd_conv_b[l], ssd_a_log[l], ssd_dt_bias[l], ssd_d[l],
                             ssd_norm[l], s5_ops, s5_d[l], s5_glu_w[l], s5_glu_b[l], w_branch[l])
        h, hm = _out_norm(lay, rows, mixed, h, w_out[l].astype(BF16), mvec(l, 5), *lnp(l, 1),
                          mvec(l, 6), mvec(l, 7))
        nxt = (zero_vec, zero_vec) if last else (mvec(l + 1, 0), mvec(l + 1, 1))
        h, hm = _half_ffn(lay, rows, hm, h, ffn_w1[l, 1].astype(BF16), ffn_w3[l, 1].astype(BF16),
                          ffn_w2[l, 1].astype(BF16), mvec(l, 8), *lnp(l, 2), *nxt)
    return h.reshape(x.shape)


def kernel(x, c, ctx, c_ctx, w_mod, b_mod, ln_g, ln_b, ffn_w1, ffn_w3, ffn_w2, w_in, att_lam, att_subln, ssd_conv_w, ssd_conv_b, ssd_a_log, ssd_dt_bias, ssd_d, ssd_norm, s5_lam_re, s5_lam_im, s5_log_step, s5_b_re, s5_b_im, s5_c_re, s5_c_im, s5_d, s5_glu_w, s5_glu_b, w_branch, w_out):
    lay = Layout(x.shape[0], x.shape[1], ctx.shape[1])
    return _trunk(lay, x, c, ctx, c_ctx, w_mod, b_mod, ln_g, ln_b, ffn_w1, ffn_w3, ffn_w2, w_in,
                  att_lam, att_subln, ssd_conv_w, ssd_conv_b, ssd_a_log, ssd_dt_bias, ssd_d, ssd_norm,
                  s5_lam_re, s5_lam_im, s5_log_step, s5_b_re, s5_b_im, s5_c_re, s5_c_im,
                  s5_d, s5_glu_w, s5_glu_b, w_branch, w_out)
```

```python
import functools
import math

import jax
import jax.numpy as jnp
from jax import lax
from jax.experimental import pallas as pl
from jax.experimental.pallas import tpu as pltpu

F32 = jnp.float32
BF16 = jnp.bfloat16
LOG2_E = math.log2(math.e)

D_MODEL = 2048
DEPTH = 2
GRID_W = 64
DN_ALPHA = (2 * DEPTH) ** 0.25
N_SUB = 3
N_MOD = 3 * N_SUB
FFN_HALF = 0.5
D_FF = 5632
LN_EPS = 1e-5
RMS_EPS = 1e-6
BRANCH_W = D_MODEL // 2
N_BRANCH = 3
ATT_DH = 64
ATT_DV = 2 * ATT_DH
ATT_HEADS = BRANCH_W // ATT_DV
ROPE_BASE = 10000.0
ROPE_FREQS = ATT_DH // 4
LAMBDA_INIT_BASE = 0.8
LAMBDA_INIT_SPAN = 0.6
LAMBDA_INIT_RATE = 0.3
SSD_P = 64
SSD_HEADS = BRANCH_W // SSD_P
SSD_GROUPS = 4
SSD_HPG = SSD_HEADS // SSD_GROUPS
SSD_N = 128
SSD_CONV = 5
SSD_INNER = SSD_HEADS * SSD_P
SSD_BC_W = SSD_GROUPS * SSD_N
SSD_XBC_W = SSD_INNER + 2 * SSD_BC_W
SSD_NORM_GROUP = SSD_INNER // SSD_GROUPS
S5_CH = BRANCH_W
S5_GROUP_CH = 16
S5_GROUPS = S5_CH // S5_GROUP_CH
S5_N = 64

LANES = 128
SUBLANES = 8
VMEM_LIMIT_BYTES = 56 * 1024 * 1024
MOD_ROWS = 8
TM = 512
TN_FF = 512
TN_PROJ = 1024
TM_PROJ = 1024
TN_MERGE = 1024
TN_MOD = 1024
ATT_V_ROWS = ATT_DV + 16
ATT_TQ = 1024
ATT_TK = 2048
SSD_L = 128
SSD_TM = 256
S5_L = 16
S5_CW = S5_L * S5_GROUP_CH
S5_GB = LANES // S5_GROUP_CH
S5_NB = S5_GROUPS // S5_GB
S5_TM = 256
S5_ROW_TILES = 4
BF16_ROWS = 2 * SUBLANES
DT_PAD = LANES


def _params(*sem):
    return pltpu.CompilerParams(dimension_semantics=sem, vmem_limit_bytes=VMEM_LIMIT_BYTES)


class Layout:
    def __init__(self, batch, seq, ctx):
        self.batch, self.seq, self.ctx = batch, seq, ctx
        self.n_lat = batch * seq
        self.n_tok = batch * (seq + ctx)
        for t in (TM, SSD_TM, SSD_L):
            assert seq % t == 0 and (batch * ctx) % t == 0, (seq, ctx, t)
        assert ctx == SSD_TM and ctx % SSD_L == 0 and ctx == S5_TM and seq % S5_TM == 0
        assert seq % min(ATT_TQ, seq) == 0 and seq % min(ATT_TK, seq) == 0
        assert batch * 2 == SUBLANES and seq % GRID_W == 0

    def sample_of_tile(self, i, tile):
        return jnp.where(i < self.n_lat // tile, i // (self.seq // tile), self.batch)


def _post_norm_emit(h, upd, lng, lnb, nsh, nsc, ho_ref, hmo_ref):
    t = DN_ALPHA * h + upd
    mu = jnp.mean(t, axis=-1, keepdims=True)
    tc = t - mu
    var = jnp.mean(tc * tc, axis=-1, keepdims=True)
    hn = tc * lax.rsqrt(var + LN_EPS) * lng + lnb
    ho_ref[...] = hn
    hmo_ref[...] = (hn * (1.0 + nsc) + nsh).astype(hmo_ref.dtype)


def _mod_kernel(c_ref, w_ref, b_ref, o_ref):
    c = c_ref[...]
    s = (c * jax.nn.sigmoid(c)).astype(BF16)
    o_ref[...] = jnp.dot(s, w_ref[...].astype(BF16), preferred_element_type=F32) + b_ref[...]


def _mod_all(cc, w_mod, b_mod):
    depth, d, n = w_mod.shape
    return pl.pallas_call(
        _mod_kernel,
        grid=(depth, n // TN_MOD),
        in_specs=[pl.BlockSpec((MOD_ROWS, d), lambda l, j: (0, 0)),
                  pl.BlockSpec((None, d, TN_MOD), lambda l, j: (l, 0, j)),
                  pl.BlockSpec((None, 1, TN_MOD), lambda l, j: (l, 0, j))],
        out_specs=pl.BlockSpec((None, MOD_ROWS, TN_MOD), lambda l, j: (l, 0, j)),
        out_shape=jax.ShapeDtypeStruct((depth, MOD_ROWS, n), F32),
        compiler_params=_params("parallel", "parallel"),
        name="mod_matmul",
    )(cc, w_mod, b_mod.reshape(depth, 1, n))


def _modulate_kernel(h_ref, sh_ref, sc_ref, o_ref):
    o_ref[...] = (h_ref[...] * (1.0 + sc_ref[...]) + sh_ref[...]).astype(o_ref.dtype)


def _modulate(lay, h, sh, sc):
    vec = pl.BlockSpec((None, 1, D_MODEL), lambda i: (lay.sample_of_tile(i, TM), 0, 0))
    row = pl.BlockSpec((TM, D_MODEL), lambda i: (i, 0))
    return pl.pallas_call(
        _modulate_kernel, grid=(lay.n_tok // TM,),
        in_specs=[row, vec, vec], out_specs=row,
        out_shape=jax.ShapeDtypeStruct((lay.n_tok, D_MODEL), BF16),
        compiler_params=_params("parallel"), name="modulate",
    )(h, sh, sc)


def _ffn_kernel(hm_ref, h_ref, w1_ref, w3_ref, w2_ref, gate_ref, lng_ref, lnb_ref, nsh_ref, nsc_ref,
                ho_ref, hmo_ref, acc_ref):
    j = pl.program_id(1)

    @pl.when(j == 0)
    def _():
        acc_ref[...] = jnp.zeros(acc_ref.shape, F32)

    hm = hm_ref[...]
    a = jnp.dot(hm, w1_ref[...], preferred_element_type=F32)
    b = jnp.dot(hm, w3_ref[...], preferred_element_type=F32)
    p = (a * jax.nn.sigmoid(a) * b).astype(BF16)
    acc_ref[...] += jnp.dot(p, w2_ref[...], preferred_element_type=F32)

    @pl.when(j == pl.num_programs(1) - 1)
    def _():
        upd = (FFN_HALF * gate_ref[...]) * acc_ref[...]
        _post_norm_emit(h_ref[...], upd, lng_ref[...], lnb_ref[...], nsh_ref[...], nsc_ref[...],
                        ho_ref, hmo_ref)


def _half_ffn(lay, rows, hm, h, w1, w3, w2, gate, lng, lnb, nsh, nsc):
    vec = pl.BlockSpec((None, 1, D_MODEL), lambda i, j: (lay.sample_of_tile(i, TM), 0, 0))
    par = pl.BlockSpec((1, D_MODEL), lambda i, j: (0, 0))
    row = pl.BlockSpec((TM, D_MODEL), lambda i, j: (i, 0))
    return pl.pallas_call(
        _ffn_kernel,
        grid=(rows // TM, D_FF // TN_FF),
        in_specs=[row, row,
                  pl.BlockSpec((D_MODEL, TN_FF), lambda i, j: (0, j)),
                  pl.BlockSpec((D_MODEL, TN_FF), lambda i, j: (0, j)),
                  pl.BlockSpec((TN_FF, D_MODEL), lambda i, j: (j, 0)),
                  vec, par, par, vec, vec],
        out_specs=[row, row],
        out_shape=[jax.ShapeDtypeStruct((rows, D_MODEL), F32),
                   jax.ShapeDtypeStruct((rows, D_MODEL), BF16)],
        scratch_shapes=[pltpu.VMEM((TM, D_MODEL), F32)],
        compiler_params=_params("parallel", "arbitrary"), name="half_ffn",
    )(hm, h, w1, w3, w2, gate, lng, lnb, nsh, nsc)


def _proj_kernel(x_ref, w_ref, o_ref):
    o_ref[...] = jnp.dot(x_ref[...], w_ref[...], preferred_element_type=F32).astype(o_ref.dtype)


def _proj(lay, hm, w, out_dtype, name):
    n = w.shape[1]
    tn = min(TN_PROJ, n)
    tm = TM_PROJ if lay.n_tok % TM_PROJ == 0 else TM
    return pl.pallas_call(
        _proj_kernel, grid=(lay.n_tok // tm, n // tn),
        in_specs=[pl.BlockSpec((tm, D_MODEL), lambda i, j: (i, 0)),
                  pl.BlockSpec((D_MODEL, tn), lambda i, j: (0, j))],
        out_specs=pl.BlockSpec((tm, tn), lambda i, j: (i, j)),
        out_shape=jax.ShapeDtypeStruct((lay.n_tok, n), out_dtype),
        compiler_params=_params("parallel", "parallel"), name=name,
    )(hm, w)


def _proj_qk_kernel(x_ref, w_ref, cos_ref, sin_ref, o_ref, *, n_lat_tiles, is_q):
    i = pl.program_id(0)
    acc = jnp.dot(x_ref[...], w_ref[...], preferred_element_type=F32)
    lane = lax.broadcasted_iota(jnp.int32, (acc.shape[0], LANES), 1)
    low_rows = lax.broadcasted_iota(jnp.int32, (LANES, acc.shape[0]), 0) < ATT_DH

    def emit(rotate):
        if rotate:
            cos, sin = cos_ref[...], sin_ref[...]
            first = (lane % (2 * ROPE_FREQS)) < ROPE_FREQS
        for h in range(acc.shape[1] // LANES):
            t = acc[:, h * LANES:(h + 1) * LANES]
            if rotate:
                partner = jnp.where(first, pltpu.roll(t, LANES - ROPE_FREQS, 1), pltpu.roll(t, ROPE_FREQS, 1))
                t = t * cos + partner * sin
            if is_q:
                tt = (t * (ATT_DH ** -0.5 * LOG2_E)).T
                o_ref[h, 0] = jnp.where(low_rows, tt, 0.0).astype(o_ref.dtype)
                o_ref[h, 1] = jnp.where(low_rows, 0.0, tt).astype(o_ref.dtype)
            else:
                o_ref[:, h * LANES:(h + 1) * LANES] = t.astype(o_ref.dtype)

    @pl.when(i < n_lat_tiles)
    def _():
        emit(True)

    @pl.when(i >= n_lat_tiles)
    def _():
        emit(False)


def _rope_tables(seq):
    rows = seq // GRID_W
    row = jnp.repeat(jnp.arange(rows), GRID_W)
    col = jnp.tile(jnp.arange(GRID_W), rows)
    inv = ROPE_BASE ** (-jnp.arange(ROPE_FREQS, dtype=F32) / ROPE_FREQS)
    ar, ac = row[:, None] * inv, col[:, None] * inv
    cos = jnp.concatenate([jnp.cos(ar), jnp.cos(ar), jnp.cos(ac), jnp.cos(ac)], axis=1)
    sin = jnp.concatenate([-jnp.sin(ar), jnp.sin(ar), -jnp.sin(ac), jnp.sin(ac)], axis=1)
    return jnp.tile(cos, (1, 2)), jnp.tile(sin, (1, 2))


def _proj_qk(lay, hm, w, cos, sin, is_q):
    tps = lay.seq // TM
    kern = functools.partial(_proj_qk_kernel, n_lat_tiles=lay.n_lat // TM, is_q=is_q)
    tab = pl.BlockSpec((TM, LANES), lambda i: (i % tps, 0))
    if is_q:
        out_spec = pl.BlockSpec((ATT_HEADS, 2, LANES, TM), lambda i: (0, 0, 0, i))
        out_shape = jax.ShapeDtypeStruct((ATT_HEADS, 2, LANES, lay.n_tok), BF16)
    else:
        out_spec = pl.BlockSpec((TM, BRANCH_W), lambda i: (i, 0))
        out_shape = jax.ShapeDtypeStruct((lay.n_tok, BRANCH_W), BF16)
    return pl.pallas_call(
        kern, grid=(lay.n_tok // TM,),
        in_specs=[pl.BlockSpec((TM, D_MODEL), lambda i: (i, 0)),
                  pl.BlockSpec((D_MODEL, BRANCH_W), lambda i: (0, 0)), tab, tab],
        out_specs=out_spec, out_shape=out_shape,
        compiler_params=_params("parallel"), name="proj_q" if is_q else "proj_k",
    )(hm, w, cos, sin)


def _proj_v_kernel(x_ref, w_ref, o_ref):
    acc = jnp.dot(x_ref[...], w_ref[...], preferred_element_type=F32)
    tail = ATT_V_ROWS - ATT_DV
    ones_row = jnp.where(lax.broadcasted_iota(jnp.int32, (tail, acc.shape[0]), 0) == 0, 1.0, 0.0)
    for h in range(acc.shape[1] // LANES):
        o_ref[h, :ATT_DV, :] = acc[:, h * LANES:(h + 1) * LANES].T.astype(o_ref.dtype)
        o_ref[h, ATT_DV:, :] = ones_row.astype(o_ref.dtype)


def _proj_v(lay, hm, w):
    return pl.pallas_call(
        _proj_v_kernel, grid=(lay.n_tok // TM,),
        in_specs=[pl.BlockSpec((TM, D_MODEL), lambda i: (i, 0)),
                  pl.BlockSpec((D_MODEL, BRANCH_W), lambda i: (0, 0))],
        out_specs=pl.BlockSpec((ATT_HEADS, ATT_V_ROWS, TM), lambda i: (0, 0, i)),
        out_shape=jax.ShapeDtypeStruct((ATT_HEADS, ATT_V_ROWS, lay.n_tok), BF16),
        compiler_params=_params("parallel"), name="proj_v",
    )(hm, w)


def _attn_kernel(lamv_ref, subln_ref, q0_ref, q1_ref, kc_ref, vc_ref, *rest, n_lat_chunks, tk, lam_init):
    if n_lat_chunks:
        kl_ref, vl_ref, o_ref = rest
    else:
        _, o_ref = rest
    lv = lamv_ref[...]
    lam = (jnp.exp(jnp.sum(lv[0:1] * lv[1:2], axis=-1, keepdims=True))
           - jnp.exp(jnp.sum(lv[2:3] * lv[3:4], axis=-1, keepdims=True)) + lam_init)

    def first(qt, k, vt):
        s = jnp.dot(k, qt, preferred_element_type=F32)
        m = jnp.max(s, axis=0, keepdims=True)
        return m, jnp.dot(vt, jnp.exp2(s - m).astype(BF16), preferred_element_type=F32)

    def update(state, qt, k, vt):
        m, acc = state
        s = jnp.dot(k, qt, preferred_element_type=F32)
        m_new = jnp.maximum(m, jnp.max(s, axis=0, keepdims=True))
        e = jnp.exp2(s - m_new).astype(BF16)
        return m_new, jnp.exp2(m - m_new) * acc + jnp.dot(vt, e, preferred_element_type=F32)

    n_q = q0_ref.shape[1]
    qt = jnp.concatenate([q0_ref[...], q1_ref[...]], axis=1)
    state = first(qt, kc_ref[...], vc_ref[...])
    if n_lat_chunks:
        def body(c, st):
            off = pl.multiple_of(c * tk, tk)
            return update(st, qt, kl_ref[pl.ds(off, tk), :], vl_ref[:, pl.ds(off, tk)])

        state = lax.fori_loop(0, n_lat_chunks, body, state)
    acc = state[1]
    out = acc[:ATT_DV] / acc[ATT_DV:ATT_DV + 1]
    o = out[:, :n_q] - lam * out[:, n_q:]
    o = o * lax.rsqrt(jnp.mean(o * o, axis=0, keepdims=True) + RMS_EPS) * subln_ref[...] * (1.0 - lam_init)
    o_ref[...] = o.T.astype(o_ref.dtype)


def _diff_attention(lay, qt, k, vt, att_lam, subln, lam_init):
    b_, s_, c_ = lay.batch, lay.seq, lay.ctx
    lamv = pl.BlockSpec((4, ATT_DH), lambda *_: (0, 0))
    sub = pl.BlockSpec((ATT_DV, 1), lambda *_: (0, 0))
    out_shape = jax.ShapeDtypeStruct((lay.n_tok, BRANCH_W), BF16)
    ctx_blk = lay.n_lat // c_

    def specs(tq, q_blk):
        return [lamv, sub,
                pl.BlockSpec((None, None, LANES, tq), lambda b, h, i: (h, 0, 0, q_blk(b, i))),
                pl.BlockSpec((None, None, LANES, tq), lambda b, h, i: (h, 1, 0, q_blk(b, i))),
                pl.BlockSpec((c_, LANES), lambda b, h, i: (ctx_blk + b, h)),
                pl.BlockSpec((None, ATT_V_ROWS, c_), lambda b, h, i: (h, 0, ctx_blk + b))]

    tq, tk = min(ATT_TQ, s_), min(ATT_TK, s_)
    n_q = s_ // tq
    lat_blk = lambda b, i: b * n_q + i
    lat = pl.pallas_call(
        functools.partial(_attn_kernel, n_lat_chunks=s_ // tk, tk=tk, lam_init=lam_init),
        grid=(b_, ATT_HEADS, n_q),
        in_specs=specs(tq, lat_blk) + [
            pl.BlockSpec((s_, LANES), lambda b, h, i: (b, h)),
            pl.BlockSpec((None, ATT_V_ROWS, s_), lambda b, h, i: (h, 0, b))],
        out_specs=pl.BlockSpec((tq, LANES), lambda b, h, i: (lat_blk(b, i), h)),
        out_shape=out_shape,
        compiler_params=_params("parallel", "parallel", "arbitrary"), name="attn_latent",
    )(att_lam, subln, qt, qt, k, vt, k, vt)
    ctx_q = lambda b, i: ctx_blk + b
    return pl.pallas_call(
        functools.partial(_attn_kernel, n_lat_chunks=0, tk=0, lam_init=lam_init),
        grid=(b_, ATT_HEADS, 1),
        in_specs=specs(c_, ctx_q) + [pl.BlockSpec(memory_space=pl.ANY)],
        out_specs=pl.BlockSpec((c_, LANES), lambda b, h, i: (ctx_q(b, i), h)),
        out_shape=out_shape, input_output_aliases={6: 0},
        compiler_params=_params("parallel", "parallel", "arbitrary"), name="attn_context",
    )(att_lam, subln, qt, qt, k, vt, lat)


def _ssd_prep_kernel(prev_ref, cur_ref, next_ref, dt_ref, cw_ref, cb_ref, dtb_ref,
                     xs_ref, bm_ref, cm_ref, dts_ref, dtst_ref, ext_ref, *, tiles_per_seq, n_lat_tiles):
    i = pl.program_id(0)
    is_ctx = i >= n_lat_tiles
    first = jnp.logical_or(is_ctx, i % tiles_per_seq == 0)
    last = jnp.logical_or(is_ctx, i % tiles_per_seq == tiles_per_seq - 1)
    tm = cur_ref.shape[0]
    ext_ref[0:SUBLANES, :] = jnp.where(first, 0.0, prev_ref[...])
    ext_ref[SUBLANES:SUBLANES + tm, :] = cur_ref[...]
    ext_ref[SUBLANES + tm:, :] = jnp.where(last, 0.0, next_ref[...])
    acc = jnp.zeros(cur_ref.shape, F32) + cb_ref[...]
    for k in range(SSD_CONV):
        start = SUBLANES + k - SSD_CONV // 2
        acc = acc + ext_ref[start:start + tm, :] * cw_ref[k:k + 1, :]
    act = acc * jax.nn.sigmoid(acc)
    xs_ref[...] = act[:, :SSD_INNER]
    bm_ref[...] = act[:, SSD_INNER:SSD_INNER + SSD_BC_W].astype(bm_ref.dtype)
    cm_ref[...] = act[:, SSD_INNER + SSD_BC_W:].astype(cm_ref.dtype)
    x = dt_ref[...] + dtb_ref[...]
    sp = jnp.maximum(x, 0.0) + jnp.log1p(jnp.exp(-jnp.abs(x)))
    dts_ref[0] = sp
    dts_ref[1] = pltpu.roll(sp, DT_PAD - SSD_HEADS, 1)
    spt = sp.T
    dtst_ref[0] = spt[0:SSD_HEADS]
    dtst_ref[1] = spt[SSD_HEADS:2 * SSD_HEADS]


def _ssd_prep(lay, xbc, dt, conv_w, conv_b, dt_bias):
    n, tm = lay.n_tok, SSD_TM
    sub_per_tile = tm // SUBLANES
    n_sub = n // SUBLANES
    kern = functools.partial(_ssd_prep_kernel, tiles_per_seq=lay.seq // tm, n_lat_tiles=lay.n_lat // tm)
    row = lambda w: pl.BlockSpec((tm, w), lambda i: (i, 0))
    return pl.pallas_call(
        kern, grid=(n // tm,),
        in_specs=[pl.BlockSpec((SUBLANES, SSD_XBC_W), lambda i: (jnp.maximum(i * sub_per_tile - 1, 0), 0)),
                  row(SSD_XBC_W),
                  pl.BlockSpec((SUBLANES, SSD_XBC_W),
                               lambda i: (jnp.minimum((i + 1) * sub_per_tile, n_sub - 1), 0)),
                  row(DT_PAD),
                  pl.BlockSpec((SUBLANES, SSD_XBC_W), lambda i: (0, 0)),
                  pl.BlockSpec((1, SSD_XBC_W), lambda i: (0, 0)),
                  pl.BlockSpec((1, DT_PAD), lambda i: (0, 0))],
        out_specs=[row(SSD_INNER), row(SSD_BC_W), row(SSD_BC_W),
                   pl.BlockSpec((2, tm, DT_PAD), lambda i: (0, i, 0)),
                   pl.BlockSpec((2, SSD_HEADS, tm), lambda i: (0, 0, i))],
        out_shape=[jax.ShapeDtypeStruct((n, SSD_INNER), F32),
                   jax.ShapeDtypeStruct((n, SSD_BC_W), BF16),
                   jax.ShapeDtypeStruct((n, SSD_BC_W), BF16),
                   jax.ShapeDtypeStruct((2, n, DT_PAD), F32),
                   jax.ShapeDtypeStruct((2, SSD_HEADS, n), F32)],
        scratch_shapes=[pltpu.VMEM((tm + 2 * SUBLANES, SSD_XBC_W), F32)],
        compiler_params=_params("parallel"), name="ssd_prep",
    )(xbc, xbc, xbc, dt, conv_w, conv_b, dt_bias)


def _split_dot(a, x, x_is_lhs=False):
    out = None
    r = x
    for _ in range(3):
        t = r.astype(BF16)
        r = r - t.astype(F32)
        d = (jnp.dot(t, a, preferred_element_type=F32) if x_is_lhs
             else jnp.dot(a, t, preferred_element_type=F32))
        out = d if out is None else out + d
    return out


def _ssd_kernel(xs_ref, bm_ref, cm_ref, dts_ref, dtst_ref, arow_ref, acol_ref, y_ref, h_ref):
    sign = 1 - 2 * pl.program_id(1)

    @pl.when(pl.program_id(2) == 0)
    def _():
        h_ref[...] = jnp.zeros(h_ref.shape, F32)

    ll = xs_ref.shape[0]
    r = lax.broadcasted_iota(jnp.int32, (ll, ll), 0)
    c = lax.broadcasted_iota(jnp.int32, (ll, ll), 1)
    mask = (r - c) * sign >= 0
    mask_t = (r - c) * sign <= 0
    one_hot = lambda m: jnp.where(m, 1.0, 0.0).astype(BF16)
    dts_t = dtst_ref[...]
    da = dts_ref[...] * arow_ref[...]
    da_t = dts_t * acol_ref[...]
    cs = _split_dot(one_hot(mask), da)
    cs_t = _split_dot(one_hot(mask_t), da_t, x_is_lhs=True)
    tot = jnp.sum(da, axis=0, keepdims=True)
    from_start = jnp.exp(cs)
    chunk_decay = jnp.exp(tot)
    src_w_t = dts_t * jnp.exp(jnp.sum(da_t, axis=1, keepdims=True) - cs_t)
    lane = lax.broadcasted_iota(jnp.int32, (ll, LANES), 1)
    lo_half = lane < SSD_P

    def pair_expand(t, hd):
        rows = t.shape[0]
        return jnp.where(lo_half[:rows], jnp.broadcast_to(t[:, hd:hd + 1], (rows, LANES)),
                         jnp.broadcast_to(t[:, hd + 1:hd + 2], (rows, LANES)))

    gw = SSD_HPG * SSD_P
    for g in range(SSD_GROUPS):
        bg = bm_ref[:, g * SSD_N:(g + 1) * SSD_N]
        cg = cm_ref[:, g * SSD_N:(g + 1) * SSD_N]
        cb = lax.dot_general(cg, bg, (((1,), (1,)), ((), ())), preferred_element_type=F32)
        bg_t = bg.astype(F32).T
        y_parts, st_parts, fs_parts, cd_parts = [], [], [], []
        for pr in range(SSD_HPG // 2):
            hd = g * SSD_HPG + 2 * pr
            xp = xs_ref[:, hd * SSD_P:(hd + 2) * SSD_P]
            fs_parts.append(pair_expand(from_start, hd))
            cd_parts.append(pair_expand(chunk_decay, hd))
            yp = sp = None
            for k in range(2):
                e = hd + k
                seg = cs[:, e:e + 1] - cs_t[e:e + 1, :]
                dec = jnp.where(mask, jnp.exp(jnp.where(mask, seg, 0.0)), 0.0)
                w = (cb * dec * dts_t[e:e + 1, :]).astype(BF16)
                xk = jnp.where(lo_half if k == 0 else jnp.logical_not(lo_half), xp, 0.0).astype(BF16)
                d = jnp.dot(w, xk, preferred_element_type=F32)
                yp = d if yp is None else yp + d
                s = jnp.dot((bg_t * src_w_t[e:e + 1, :]).astype(BF16), xk, preferred_element_type=F32)
                sp = s if sp is None else sp + s
            y_parts.append(yp)
            st_parts.append(sp)
        h_prev = h_ref[g]
        y_off = jnp.dot(cg, h_prev.astype(BF16), preferred_element_type=F32) * jnp.concatenate(fs_parts, axis=1)
        y_ref[:, g * gw:(g + 1) * gw] = jnp.concatenate(y_parts, axis=1) + y_off
        h_ref[g] = h_prev * jnp.concatenate(cd_parts, axis=1) + jnp.concatenate(st_parts, axis=1)


def _ssd_scan(lay, xs, bm, cm, dts, dtst, a_row, a_col):
    ll = SSD_L
    n_ctx, n_lat = lay.ctx // ll, lay.seq // ll
    ctx_base = lay.n_lat // ll

    def rb(b, d, s):
        cstep = jnp.where(d == 0, s, n_ctx - 1 - s)
        lstep = jnp.where(d == 0, s - n_ctx, n_lat - 1 - (s - n_ctx))
        return jnp.where(s < n_ctx, ctx_base + b * n_ctx + cstep, b * n_lat + lstep)

    row = lambda w: pl.BlockSpec((ll, w), lambda b, d, s: (rb(b, d, s), 0))
    return pl.pallas_call(
        _ssd_kernel, grid=(lay.batch, 2, n_ctx + n_lat),
        in_specs=[row(SSD_INNER), row(SSD_BC_W), row(SSD_BC_W),
                  pl.BlockSpec((None, ll, DT_PAD), lambda b, d, s: (d, rb(b, d, s), 0)),
                  pl.BlockSpec((None, SSD_HEADS, ll), lambda b, d, s: (d, 0, rb(b, d, s))),
                  pl.BlockSpec((None, 1, DT_PAD), lambda b, d, s: (d, 0, 0)),
                  pl.BlockSpec((None, SSD_HEADS, 1), lambda b, d, s: (d, 0, 0))],
        out_specs=pl.BlockSpec((None, ll, SSD_INNER), lambda b, d, s: (d, rb(b, d, s), 0)),
        out_shape=jax.ShapeDtypeStruct((2, lay.n_tok, SSD_INNER), F32),
        scratch_shapes=[pltpu.VMEM((SSD_GROUPS, SSD_N, SSD_HPG * SSD_P), F32)],
        compiler_params=_params("parallel", "parallel", "arbitrary"), name="ssd_scan",
    )(xs, bm, cm, dts, dtst, a_row, a_col)


def _ssd_gate_norm_kernel(y0_ref, y1_ref, xs_ref, z_ref, dsk_ref, nw_ref, o_ref):
    z = z_ref[...]
    gated = (y0_ref[...] + y1_ref[...] + xs_ref[...] * dsk_ref[...]) * (z * jax.nn.sigmoid(z))
    nw = nw_ref[...]
    for g in range(SSD_GROUPS):
        sl = slice(g * SSD_NORM_GROUP, (g + 1) * SSD_NORM_GROUP)
        t = gated[:, sl]
        o_ref[:, sl] = (t * lax.rsqrt(jnp.mean(t * t, axis=-1, keepdims=True) + RMS_EPS)
                        * nw[:, sl]).astype(o_ref.dtype)


def _ssd_gate_norm(lay, ydir, xs, z, dsk, nw):
    tm = SSD_TM
    row = pl.BlockSpec((tm, SSD_INNER), lambda i: (i, 0))
    par = pl.BlockSpec((1, SSD_INNER), lambda i: (0, 0))
    return pl.pallas_call(
        _ssd_gate_norm_kernel, grid=(lay.n_tok // tm,),
        in_specs=[pl.BlockSpec((None, tm, SSD_INNER), lambda i: (0, i, 0)),
                  pl.BlockSpec((None, tm, SSD_INNER), lambda i: (1, i, 0)), row, row, par, par],
        out_specs=row, out_shape=jax.ShapeDtypeStruct((lay.n_tok, SSD_INNER), BF16),
        compiler_params=_params("parallel"), name="ssd_gate_norm",
    )(ydir, ydir, xs, z, dsk, nw)


def _s5_operators(lam_re, lam_im, log_step, b_re, b_im, c_re, c_im):
    hp = lax.Precision.HIGHEST
    ll, hh = S5_L, S5_GROUP_CH
    step = jnp.exp(log_step)[..., None, None]
    d = jnp.arange(ll + 1, dtype=F32)
    p_mag = jnp.exp(lam_re[..., None] * step * d)
    p_ang = lam_im[..., None] * step * d
    p_re, p_im = p_mag * jnp.cos(p_ang), p_mag * jnp.sin(p_ang)
    ab_re, ab_im = p_re[..., 1], p_im[..., 1]
    den = lam_re * lam_re + lam_im * lam_im
    k_re = ((ab_re - 1.0) * lam_re + ab_im * lam_im) / den
    k_im = (ab_im * lam_re - (ab_re - 1.0) * lam_im) / den
    bb_re = k_re[..., None] * b_re - k_im[..., None] * b_im
    bb_im = k_re[..., None] * b_im + k_im[..., None] * b_re
    cp_re = c_re[..., None] * p_re[:, :, None] - c_im[..., None] * p_im[:, :, None]
    cp_im = c_re[..., None] * p_im[:, :, None] + c_im[..., None] * p_re[:, :, None]
    kern = (jnp.einsum('zghnd,zgnk->zgdhk', cp_re, bb_re, precision=hp)
            - jnp.einsum('zghnd,zgnk->zgdhk', cp_im, bb_im, precision=hp))
    s_idx = jnp.arange(ll)[:, None]
    l_idx = jnp.arange(ll)[None, :]
    t_f = jnp.where((l_idx >= s_idx)[None, :, :, None, None], kern[0][:, jnp.clip(l_idx - s_idx, 0, ll)], 0.0)
    t_b = jnp.where((s_idx >= l_idx)[None, :, :, None, None], kern[1][:, jnp.clip(s_idx - l_idx, 0, ll)], 0.0)
    toep = (t_f + t_b).transpose(0, 1, 4, 2, 3).reshape(S5_GROUPS, S5_CW, S5_CW)

    def state_in(z, powers):
        pr, pi = p_re[z][..., powers], p_im[z][..., powers]
        re = pr[..., None] * bb_re[z][:, :, None] - pi[..., None] * bb_im[z][:, :, None]
        im = pr[..., None] * bb_im[z][:, :, None] + pi[..., None] * bb_re[z][:, :, None]
        re = re.transpose(0, 2, 3, 1).reshape(S5_GROUPS, S5_CW, S5_N)
        im = im.transpose(0, 2, 3, 1).reshape(S5_GROUPS, S5_CW, S5_N)
        return jnp.concatenate([re, im], axis=-1)

    ws_f = state_in(0, ll - 1 - jnp.arange(ll))
    ws_b = state_in(1, jnp.arange(ll))
    w1 = jnp.concatenate([toep, ws_f, ws_b], axis=-1)

    def state_out(z, powers):
        re = cp_re[z][..., powers].transpose(0, 2, 3, 1).reshape(S5_GROUPS, S5_N, S5_CW)
        im = cp_im[z][..., powers].transpose(0, 2, 3, 1).reshape(S5_GROUPS, S5_N, S5_CW)
        return jnp.concatenate([re, -im], axis=1)

    wo = jnp.concatenate([state_out(0, jnp.arange(ll) + 1), state_out(1, ll - jnp.arange(ll))], axis=1)
    ar, ai = p_re[..., ll], p_im[..., ll]
    a1 = jnp.concatenate([ar, ar], axis=-1)
    a2 = jnp.concatenate([-ai, ai], axis=-1)
    zeros = jnp.zeros_like(a1[0])
    av = jnp.stack([a1[0], a2[0], a1[1], a2[1], zeros, zeros, zeros, zeros], axis=1)
    return w1.astype(BF16), wo.astype(BF16), av


def _s5_kernel(x_ref, w1_ref, wo_ref, av_ref, y_ref, u_scr, y_scr, sf_ref, sb_ref, *, n_ctx_tiles, n_tiles):
    rows = x_ref.shape[0]
    row_tile = rows // S5_ROW_TILES
    nst = 2 * S5_N
    lane_blk = lax.broadcasted_iota(jnp.int32, (row_tile, LANES), 1) // S5_GROUP_CH

    def block_transpose(arrs):
        a = list(arrs)
        k = S5_GB // 2
        while k:
            bit = (lane_blk & k) != 0
            for i in range(S5_GB):
                if not i & k:
                    lo, hi = a[i], a[i + k]
                    a[i] = jnp.where(bit, pltpu.roll(hi, k * S5_GROUP_CH, 1), lo)
                    a[i + k] = jnp.where(bit, hi, pltpu.roll(lo, LANES - k * S5_GROUP_CH, 1))
            k //= 2
        return a

    def gather(r, _):
        r0 = pl.multiple_of(r * row_tile, BF16_ROWS)
        for hv in range(S5_L // S5_GB):
            per_group = block_transpose(
                [x_ref[pl.ds(r0, row_tile), hv * S5_GB + sl, :] for sl in range(S5_GB)])
            for g in range(S5_GB):
                u_scr[g, pl.ds(r0, row_tile), hv * LANES:(hv + 1) * LANES] = per_group[g].astype(u_scr.dtype)
        return 0

    lax.fori_loop(0, S5_ROW_TILES, gather, 0)

    for g in range(S5_GB):
        p = jnp.dot(u_scr[g], w1_ref[g], preferred_element_type=F32)
        y_scr[g] = p[:, :S5_CW]
        sf_ref[:, g * nst:(g + 1) * nst] = p[:, S5_CW:S5_CW + nst]
        sb_ref[:, g * nst:(g + 1) * nst] = p[:, S5_CW + nst:]

    wide = S5_GB * nst
    half = SUBLANES // 2
    coef = lambda k: jnp.broadcast_to(
        jnp.concatenate([av_ref[g][k:k + 1] for g in range(S5_GB)], axis=1), (half, wide))
    a1f, a2f, a1b, a2b = coef(0), coef(1), coef(2), coef(3)
    low = lax.broadcasted_iota(jnp.int32, (SUBLANES, wide), 1) % nst < S5_N
    swap = lambda t: jnp.where(low, pltpu.roll(t, wide - S5_N, 1), pltpu.roll(t, S5_N, 1))

    def body(j, carry):
        hf, hfs, hb, hbs = carry
        of = pl.multiple_of(j * SUBLANES, SUBLANES)
        s = sf_ref[pl.ds(of, SUBLANES), :]
        ss = swap(s)
        h1 = a1f * hf + a2f * hfs + s[:half]
        h1s = a1f * hfs - a2f * hf + ss[:half]
        sf_ref[pl.ds(of, SUBLANES), :] = jnp.concatenate([hf, h1], axis=0)
        h2 = a1f * h1 + a2f * h1s + s[half:]
        h2s = a1f * h1s - a2f * h1 + ss[half:]
        jb = jnp.where(j < n_ctx_tiles, n_ctx_tiles - 1 - j, n_tiles - 1 - (j - n_ctx_tiles))
        ob = pl.multiple_of(jb * SUBLANES, SUBLANES)
        s = sb_ref[pl.ds(ob, SUBLANES), :]
        ss = swap(s)
        g1 = a1b * hb + a2b * hbs + s[half:]
        g1s = a1b * hbs - a2b * hb + ss[half:]
        sb_ref[pl.ds(ob, SUBLANES), :] = jnp.concatenate([g1, hb], axis=0)
        g2 = a1b * g1 + a2b * g1s + s[:half]
        g2s = a1b * g1s - a2b * g1 + ss[:half]
        return h2, h2s, g2, g2s

    z = jnp.zeros((half, wide), F32)
    lax.fori_loop(0, n_tiles, body, (z, z, z, z))

    for g in range(S5_GB):
        wo = wo_ref[g]
        y_scr[g] += (
            jnp.dot(sf_ref[:, g * nst:(g + 1) * nst].astype(BF16), wo[:nst], preferred_element_type=F32)
            + jnp.dot(sb_ref[:, g * nst:(g + 1) * nst].astype(BF16), wo[nst:], preferred_element_type=F32))

    def scatter(r, _):
        r0 = pl.multiple_of(r * row_tile, BF16_ROWS)
        for hv in range(S5_L // S5_GB):
            per_lag = block_transpose(
                [y_scr[g, pl.ds(r0, row_tile), hv * LANES:(hv + 1) * LANES] for g in range(S5_GB)])
            for sl in range(S5_GB):
                y_ref[pl.ds(r0, row_tile), hv * S5_GB + sl, :] = per_lag[sl]
        return 0

    lax.fori_loop(0, S5_ROW_TILES, scatter, 0)


def _s5_chunked(lay, x, w1, wo, av):
    rows = lay.n_tok // S5_L
    n_ctx_rows = lay.batch * lay.ctx // S5_L
    assert rows % (S5_ROW_TILES * BF16_ROWS) == 0
    kern = functools.partial(_s5_kernel, n_ctx_tiles=n_ctx_rows // SUBLANES, n_tiles=rows // SUBLANES)
    blk = pl.BlockSpec((rows, None, S5_L, LANES), lambda g: (0, g, 0, 0), pipeline_mode=pl.Buffered(1))
    return pl.pallas_call(
        kern, grid=(S5_NB,),
        in_specs=[blk,
                  pl.BlockSpec((S5_GB, S5_CW, S5_CW + 4 * S5_N), lambda g: (g, 0, 0)),
                  pl.BlockSpec((S5_GB, 4 * S5_N, S5_CW), lambda g: (g, 0, 0)),
                  pl.BlockSpec((S5_GB, SUBLANES, 2 * S5_N), lambda g: (g, 0, 0))],
        out_specs=blk,
        out_shape=jax.ShapeDtypeStruct((rows, S5_NB, S5_L, LANES), F32),
        scratch_shapes=[pltpu.VMEM((S5_GB, rows, S5_CW), BF16), pltpu.VMEM((S5_GB, rows, S5_CW), F32),
                        pltpu.VMEM((rows, S5_GB * 2 * S5_N), F32), pltpu.VMEM((rows, S5_GB * 2 * S5_N), F32)],
        compiler_params=_params("parallel"), name="s5_chunked",
    )(x, w1, wo, av)


def _s5_tile_index(lay, i):
    n_lat_tiles, per_seq = lay.n_lat // S5_TM, lay.seq // S5_TM
    return jnp.where(i < n_lat_tiles, 1 + i % per_seq, 0), jnp.where(i < n_lat_tiles, i // per_seq, i - n_lat_tiles)


def _s5_chunk_spec(lay):
    return pl.BlockSpec((S5_TM // S5_L, None, S5_NB, S5_L, LANES),
                        lambda i: (*_s5_tile_index(lay, i), 0, 0, 0))


def _proj_u_kernel(x_ref, w_ref, o_ref):
    acc = jnp.dot(x_ref[...], w_ref[...], preferred_element_type=F32)
    for nb in range(S5_NB):
        o_ref[:, nb] = acc[:, nb * LANES:(nb + 1) * LANES].reshape(S5_TM // S5_L, S5_L, LANES)


def _proj_u(lay, hm, w):
    chunks = (lay.seq + lay.ctx) // S5_L
    return pl.pallas_call(
        _proj_u_kernel, grid=(lay.n_tok // S5_TM,),
        in_specs=[pl.BlockSpec((S5_TM, D_MODEL), lambda i: (i, 0)),
                  pl.BlockSpec((D_MODEL, S5_CH), lambda i: (0, 0))],
        out_specs=_s5_chunk_spec(lay),
        out_shape=jax.ShapeDtypeStruct((chunks, lay.batch, S5_NB, S5_L, LANES), F32),
        compiler_params=_params("parallel"), name="proj_u",
    )(hm, w)


def _s5_glu_kernel(ys_ref, u_ref, dsk_ref, w_ref, b_ref, o_ref):
    natural = lambda ref: jnp.concatenate([ref[:, nb].reshape(S5_TM, LANES) for nb in range(S5_NB)], axis=1)
    t = natural(ys_ref) + natural(u_ref) * dsk_ref[...]
    t = 0.5 * t * (1.0 + jnp.tanh(math.sqrt(2.0 / math.pi) * (t + 0.044715 * (t * t * t))))
    gate = jnp.dot(t.astype(BF16), w_ref[...], preferred_element_type=F32) + b_ref[...]
    o_ref[...] = (t * jax.nn.sigmoid(gate)).astype(o_ref.dtype)


def _s5_glu(lay, ys, u, dsk, w, b):
    par = pl.BlockSpec((1, S5_CH), lambda i: (0, 0))
    return pl.pallas_call(
        _s5_glu_kernel, grid=(lay.n_tok // S5_TM,),
        in_specs=[_s5_chunk_spec(lay), _s5_chunk_spec(lay), par,
                  pl.BlockSpec((S5_CH, S5_CH), lambda i: (0, 0)), par],
        out_specs=pl.BlockSpec((S5_TM, S5_CH), lambda i: (i, 0)),
        out_shape=jax.ShapeDtypeStruct((lay.n_tok, S5_CH), BF16),
        compiler_params=_params("parallel"), name="s5_glu",
    )(ys, u, dsk, w, b)


def _merge_kernel(oa_ref, os_ref, o5_ref, ga_ref, gs_ref, g5_ref, wa_ref, ws_ref, w5_ref, o_ref):
    acc = None
    for o, g, w in ((oa_ref, ga_ref, wa_ref), (os_ref, gs_ref, ws_ref), (o5_ref, g5_ref, w5_ref)):
        t = jax.nn.sigmoid(g[...]) * jnp.dot(o[...], w[...], preferred_element_type=F32)
        acc = t if acc is None else acc + t
    o_ref[...] = acc.astype(o_ref.dtype)


def _merge(rows, o_att, o_ssd, o_s5, g, w_branch):
    nt = D_MODEL // TN_MERGE
    row = pl.BlockSpec((TM, BRANCH_W), lambda i, j: (i, 0))
    gate = lambda k: pl.BlockSpec((TM, TN_MERGE), lambda i, j: (i, k * nt + j))
    wb = lambda k: pl.BlockSpec((None, BRANCH_W, TN_MERGE), lambda i, j: (k, 0, j))
    return pl.pallas_call(
        _merge_kernel, grid=(rows // TM, nt),
        in_specs=[row, row, row, gate(0), gate(1), gate(2), wb(0), wb(1), wb(2)],
        out_specs=pl.BlockSpec((TM, TN_MERGE), lambda i, j: (i, j)),
        out_shape=jax.ShapeDtypeStruct((rows, D_MODEL), BF16),
        compiler_params=_params("parallel", "parallel"), name="branch_merge",
    )(o_att, o_ssd, o_s5, g, g, g, w_branch, w_branch, w_branch)


def _out_norm_kernel(mx_ref, h_ref, w_ref, gate_ref, lng_ref, lnb_ref, nsh_ref, nsc_ref, ho_ref, hmo_ref):
    y = jnp.dot(mx_ref[...], w_ref[...], preferred_element_type=F32)
    _post_norm_emit(h_ref[...], gate_ref[...] * y, lng_ref[...], lnb_ref[...], nsh_ref[...], nsc_ref[...],
                    ho_ref, hmo_ref)


def _out_norm(lay, rows, mixed, h, w_out, gate, lng, lnb, nsh, nsc):
    vec = pl.BlockSpec((None, 1, D_MODEL), lambda i: (lay.sample_of_tile(i, TM), 0, 0))
    par = pl.BlockSpec((1, D_MODEL), lambda i: (0, 0))
    row = pl.BlockSpec((TM, D_MODEL), lambda i: (i, 0))
    return pl.pallas_call(
        _out_norm_kernel, grid=(rows // TM,),
        in_specs=[row, row, pl.BlockSpec((D_MODEL, D_MODEL), lambda i: (0, 0)), vec, par, par, vec, vec],
        out_specs=[row, row],
        out_shape=[jax.ShapeDtypeStruct((rows, D_MODEL), F32),
                   jax.ShapeDtypeStruct((rows, D_MODEL), BF16)],
        compiler_params=_params("parallel"), name="out_norm",
    )(mixed, h, w_out, gate, lng, lnb, nsh, nsc)


def _pad_cols(t, width):
    return jnp.pad(t, [(0, 0)] * (t.ndim - 1) + [(0, width - t.shape[-1])])


def _token_mixer(lay, rows_out, hm, rope, lam_init, w_in, att_lam, att_subln, conv_w, conv_b, a_log, dt_bias,
                 ssd_d, ssd_norm, s5_ops, s5_d, glu_w, glu_b, w_branch):
    cuts = [0]
    for w in (BRANCH_W, BRANCH_W, BRANCH_W, SSD_INNER, SSD_XBC_W, 2 * SSD_HEADS, S5_CH, N_BRANCH * D_MODEL):
        cuts.append(cuts[-1] + w)
    w_q, w_k, w_v, w_z, w_xbc, w_dt, w_u, w_g = (
        w_in[:, a:b].astype(BF16) for a, b in zip(cuts[:-1], cuts[1:]))
    qt = _proj_qk(lay, hm, w_q, *rope, is_q=True)
    k = _proj_qk(lay, hm, w_k, *rope, is_q=False)
    vt = _proj_v(lay, hm, w_v)
    z = _proj(lay, hm, w_z, F32, "proj_z")
    xbc = _proj(lay, hm, w_xbc, F32, "proj_xbc")
    dt = _proj(lay, hm, _pad_cols(w_dt, DT_PAD), F32, "proj_dt")
    u = _proj_u(lay, hm, w_u)
    g = _proj(lay, hm, w_g, F32, "proj_gate")

    o_att = _diff_attention(lay, qt, k, vt, att_lam, att_subln.reshape(ATT_DV, 1), lam_init)

    conv_w8 = jnp.pad(conv_w, ((0, SUBLANES - SSD_CONV), (0, 0)))
    xs, bm, cm, dts, dtst = _ssd_prep(lay, xbc, dt, conv_w8, conv_b.reshape(1, -1),
                                      _pad_cols(dt_bias.reshape(1, -1), DT_PAD))
    a = -jnp.exp(a_log.astype(F32))
    ydir = _ssd_scan(lay, xs, bm, cm, dts, dtst, _pad_cols(a, DT_PAD)[:, None, :], a[:, :, None])
    o_ssd = _ssd_gate_norm(lay, ydir, xs, z, jnp.repeat(ssd_d, SSD_P).reshape(1, -1),
                           ssd_norm.reshape(1, -1))

    ys = _s5_chunked(lay, u.reshape(lay.n_tok // S5_L, S5_NB, S5_L, LANES), *s5_ops)
    o_s5 = _s5_glu(lay, ys.reshape(u.shape), u, s5_d.reshape(1, -1), glu_w.astype(BF16),
                   glu_b.reshape(1, -1))

    return _merge(rows_out, o_att, o_ssd, o_s5, g, w_branch.astype(BF16))


def _trunk(lay, x, c, ctx, c_ctx, w_mod, b_mod, ln_g, ln_b, ffn_w1, ffn_w3, ffn_w2, w_in,
           att_lam, att_subln, ssd_conv_w, ssd_conv_b, ssd_a_log, ssd_dt_bias, ssd_d, ssd_norm,
           s5_lam_re, s5_lam_im, s5_log_step, s5_b_re, s5_b_im, s5_c_re, s5_c_im,
           s5_d, s5_glu_w, s5_glu_b, w_branch, w_out):
    depth = w_mod.shape[0]
    h = jnp.concatenate([x.reshape(lay.n_lat, D_MODEL), ctx.reshape(-1, D_MODEL)], axis=0)
    cc = jnp.concatenate([c, c_ctx[None], jnp.zeros((MOD_ROWS - lay.batch - 1, D_MODEL), F32)], axis=0)
    mod = _mod_all(cc, w_mod, b_mod).reshape(depth, MOD_ROWS, N_MOD, 1, D_MODEL)
    mvec = lambda l, k: mod[l, :, k]
    zero_vec = jnp.zeros((MOD_ROWS, 1, D_MODEL), F32)
    rope = _rope_tables(lay.seq)
    lnp = lambda l, k: (ln_g[l, k].reshape(1, -1), ln_b[l, k].reshape(1, -1))

    hm = _modulate(lay, h, mvec(0, 0), mvec(0, 1))
    for l in range(depth):
        lam_init = LAMBDA_INIT_BASE - LAMBDA_INIT_SPAN * math.exp(-LAMBDA_INIT_RATE * l)
        last = l + 1 == depth
        rows = lay.n_lat if last else lay.n_tok
        h, hm = _half_ffn(lay, lay.n_tok, hm, h, ffn_w1[l, 0].astype(BF16), ffn_w3[l, 0].astype(BF16),
                          ffn_w2[l, 0].astype(BF16), mvec(l, 2), *lnp(l, 0), mvec(l, 3), mvec(l, 4))
        s5_ops = _s5_operators(s5_lam_re[l], s5_lam_im[l], s5_log_step[l], s5_b_re[l], s5_b_im[l],
                               s5_c_re[l], s5_c_im[l])
        mixed = _token_mixer(lay, rows, hm, rope, lam_init, w_in[l], att_lam[l], att_subln[l],
                             ssd_conv_w[l], ssd_conv_b[l], ssd_a_log[l], ssd_dt_bias[l], ssd_d[l],
                             ssd_norm[l], s5_ops, s5_d[l], s5_glu_w[l], s5_glu_b[l], w_branch[l])
        h, hm = _out_norm(lay, rows, mixed, h, w_out[l].astype(BF16), mvec(l, 5), *lnp(l, 1),
                          mvec(l, 6), mvec(l, 7))
        nxt = (zero_vec, zero_vec) if last else (mvec(l + 1, 0), mvec(l + 1, 1))
        h, hm = _half_ffn(lay, rows, hm, h, ffn_w1[l, 1].astype(BF16), ffn_w3[l, 1].astype(BF16),
                          ffn_w2[l, 1].astype(BF16), mvec(l, 8), *lnp(l, 2), *nxt)
    return h.reshape(x.shape)


def kernel(x, c, ctx, c_ctx, w_mod, b_mod, ln_g, ln_b, ffn_w1, ffn_w3, ffn_w2, w_in, att_lam, att_subln, ssd_conv_w, ssd_conv_b, ssd_a_log, ssd_dt_bias, ssd_d, ssd_norm, s5_lam_re, s5_lam_im, s5_log_step, s5_b_re, s5_b_im, s5_c_re, s5_c_im, s5_d, s5_glu_w, s5_glu_b, w_branch, w_out):
    lay = Layout(x.shape[0], x.shape[1], ctx.shape[1])
    return _trunk(lay, x, c, ctx, c_ctx, w_mod, b_mod, ln_g, ln_b, ffn_w1, ffn_w3, ffn_w2, w_in,
                  att_lam, att_subln, ssd_conv_w, ssd_conv_b, ssd_a_log, ssd_dt_bias, ssd_d, ssd_norm,
                  s5_lam_re, s5_lam_im, s5_log_step, s5_b_re, s5_b_im, s5_c_re, s5_c_im,
                  s5_d, s5_glu_w, s5_glu_b, w_branch, w_out)
```

```python
import functools
import math

import jax
import jax.numpy as jnp
from jax import lax
from jax.experimental import pallas as pl
from jax.experimental.pallas import tpu as pltpu

F32 = jnp.float32
BF16 = jnp.bfloat16
LOG2_E = math.log2(math.e)

D_MODEL = 2048
DEPTH = 2
GRID_W = 64
DN_ALPHA = (2 * DEPTH) ** 0.25
N_SUB = 3
N_MOD = 3 * N_SUB
FFN_HALF = 0.5
D_FF = 5632
LN_EPS = 1e-5
RMS_EPS = 1e-6
BRANCH_W = D_MODEL // 2
N_BRANCH = 3
ATT_DH = 64
ATT_DV = 2 * ATT_DH
ATT_HEADS = BRANCH_W // ATT_DV
ROPE_BASE = 10000.0
ROPE_FREQS = ATT_DH // 4
LAMBDA_INIT_BASE = 0.8
LAMBDA_INIT_SPAN = 0.6
LAMBDA_INIT_RATE = 0.3
SSD_P = 64
SSD_HEADS = BRANCH_W // SSD_P
SSD_GROUPS = 4
SSD_HPG = SSD_HEADS // SSD_GROUPS
SSD_N = 128
SSD_CONV = 5
SSD_INNER = SSD_HEADS * SSD_P
SSD_BC_W = SSD_GROUPS * SSD_N
SSD_XBC_W = SSD_INNER + 2 * SSD_BC_W
SSD_NORM_GROUP = SSD_INNER // SSD_GROUPS
S5_CH = BRANCH_W
S5_GROUP_CH = 16
S5_GROUPS = S5_CH // S5_GROUP_CH
S5_N = 64

LANES = 128
SUBLANES = 8
VMEM_LIMIT_BYTES = 56 * 1024 * 1024
MOD_ROWS = 8
TM = 512
TN_FF = 512
TN_PROJ = 1024
TM_PROJ = 1024
TN_MERGE = 1024
TN_MOD = 1024
ATT_V_ROWS = ATT_DV + 16
ATT_TQ = 1024
ATT_TK = 2048
SSD_L = 128
SSD_TM = 256
S5_L = 16
S5_CW = S5_L * S5_GROUP_CH
S5_GB = LANES // S5_GROUP_CH
S5_NB = S5_GROUPS // S5_GB
S5_TM = 256
S5_ROW_TILES = 4
BF16_ROWS = 2 * SUBLANES
DT_PAD = LANES


def _params(*sem):
    return pltpu.CompilerParams(dimension_semantics=sem, vmem_limit_bytes=VMEM_LIMIT_BYTES)


class Layout:
    def __init__(self, batch, seq, ctx):
        self.batch, self.seq, self.ctx = batch, seq, ctx
        self.n_lat = batch * seq
        self.n_tok = batch * (seq + ctx)
        for t in (TM, SSD_TM, SSD_L):
            assert seq % t == 0 and (batch * ctx) % t == 0, (seq, ctx, t)
        assert ctx == SSD_TM and ctx % SSD_L == 0 and ctx == S5_TM and seq % S5_TM == 0
        assert seq % min(ATT_TQ, seq) == 0 and seq % min(ATT_TK, seq) == 0
        assert batch * 2 == SUBLANES and seq % GRID_W == 0

    def sample_of_tile(self, i, tile):
        return jnp.where(i < self.n_lat // tile, i // (self.seq // tile), self.batch)


def _post_norm_emit(h, upd, lng, lnb, nsh, nsc, ho_ref, hmo_ref):
    t = DN_ALPHA * h + upd
    mu = jnp.mean(t, axis=-1, keepdims=True)
    tc = t - mu
    var = jnp.mean(tc * tc, axis=-1, keepdims=True)
    hn = tc * lax.rsqrt(var + LN_EPS) * lng + lnb
    ho_ref[...] = hn
    hmo_ref[...] = (hn * (1.0 + nsc) + nsh).astype(hmo_ref.dtype)


def _mod_kernel(c_ref, w_ref, b_ref, o_ref):
    c = c_ref[...]
    s = (c * jax.nn.sigmoid(c)).astype(BF16)
    o_ref[...] = jnp.dot(s, w_ref[...].astype(BF16), preferred_element_type=F32) + b_ref[...]


def _mod_all(cc, w_mod, b_mod):
    depth, d, n = w_mod.shape
    return pl.pallas_call(
        _mod_kernel,
        grid=(depth, n // TN_MOD),
        in_specs=[pl.BlockSpec((MOD_ROWS, d), lambda l, j: (0, 0)),
                  pl.BlockSpec((None, d, TN_MOD), lambda l, j: (l, 0, j)),
                  pl.BlockSpec((None, 1, TN_MOD), lambda l, j: (l, 0, j))],
        out_specs=pl.BlockSpec((None, MOD_ROWS, TN_MOD), lambda l, j: (l, 0, j)),
        out_shape=jax.ShapeDtypeStruct((depth, MOD_ROWS, n), F32),
        compiler_params=_params("parallel", "parallel"),
        name="mod_matmul",
    )(cc, w_mod, b_mod.reshape(depth, 1, n))


def _modulate_kernel(h_ref, sh_ref, sc_ref, o_ref):
    o_ref[...] = (h_ref[...] * (1.0 + sc_ref[...]) + sh_ref[...]).astype(o_ref.dtype)


def _modulate(lay, h, sh, sc):
    vec = pl.BlockSpec((None, 1, D_MODEL), lambda i: (lay.sample_of_tile(i, TM), 0, 0))
    row = pl.BlockSpec((TM, D_MODEL), lambda i: (i, 0))
    return pl.pallas_call(
        _modulate_kernel, grid=(lay.n_tok // TM,),
        in_specs=[row, vec, vec], out_specs=row,
        out_shape=jax.ShapeDtypeStruct((lay.n_tok, D_MODEL), BF16),
        compiler_params=_params("parallel"), name="modulate",
    )(h, sh, sc)


def _ffn_kernel(hm_ref, h_ref, w1_ref, w3_ref, w2_ref, gate_ref, lng_ref, lnb_ref, nsh_ref, nsc_ref,
                ho_ref, hmo_ref, acc_ref):
    j = pl.program_id(1)

    @pl.when(j == 0)
    def _():
        acc_ref[...] = jnp.zeros(acc_ref.shape, F32)

    hm = hm_ref[...]
    a = jnp.dot(hm, w1_ref[...], preferred_element_type=F32)
    b = jnp.dot(hm, w3_ref[...], preferred_element_type=F32)
    p = (a * jax.nn.sigmoid(a) * b).astype(BF16)
    acc_ref[...] += jnp.dot(p, w2_ref[...], preferred_element_type=F32)

    @pl.when(j == pl.num_programs(1) - 1)
    def _():
        upd = (FFN_HALF * gate_ref[...]) * acc_ref[...]
        _post_norm_emit(h_ref[...], upd, lng_ref[...], lnb_ref[...], nsh_ref[...], nsc_ref[...],
                        ho_ref, hmo_ref)


def _half_ffn(lay, rows, hm, h, w1, w3, w2, layer, which, gate, lng, lnb, nsh, nsc):
    vec = pl.BlockSpec((None, 1, D_MODEL), lambda i, j: (lay.sample_of_tile(i, TM), 0, 0))
    par = pl.BlockSpec((1, D_MODEL), lambda i, j: (0, 0))
    row = pl.BlockSpec((TM, D_MODEL), lambda i, j: (i, 0))
    return pl.pallas_call(
        _ffn_kernel,
        grid=(rows // TM, D_FF // TN_FF),
        in_specs=[row, row,
                  pl.BlockSpec((None, None, D_MODEL, TN_FF), lambda i, j: (layer, which, 0, j)),
                  pl.BlockSpec((None, None, D_MODEL, TN_FF), lambda i, j: (layer, which, 0, j)),
                  pl.BlockSpec((None, None, TN_FF, D_MODEL), lambda i, j: (layer, which, j, 0)),
                  vec, par, par, vec, vec],
        out_specs=[row, row],
        out_shape=[jax.ShapeDtypeStruct((rows, D_MODEL), F32),
                   jax.ShapeDtypeStruct((rows, D_MODEL), BF16)],
        scratch_shapes=[pltpu.VMEM((TM, D_MODEL), F32)],
        compiler_params=_params("parallel", "arbitrary"), name="half_ffn",
    )(hm, h, w1, w3, w2, gate, lng, lnb, nsh, nsc)


def _proj_kernel(x_ref, w_ref, o_ref):
    o_ref[...] = jnp.dot(x_ref[...], w_ref[...], preferred_element_type=F32).astype(o_ref.dtype)


def _proj(lay, hm, w, out_dtype, name):
    n = w.shape[1]
    tn = min(TN_PROJ, n)
    tm = TM_PROJ if lay.n_tok % TM_PROJ == 0 else TM
    return pl.pallas_call(
        _proj_kernel, grid=(lay.n_tok // tm, n // tn),
        in_specs=[pl.BlockSpec((tm, D_MODEL), lambda i, j: (i, 0)),
                  pl.BlockSpec((D_MODEL, tn), lambda i, j: (0, j))],
        out_specs=pl.BlockSpec((tm, tn), lambda i, j: (i, j)),
        out_shape=jax.ShapeDtypeStruct((lay.n_tok, n), out_dtype),
        compiler_params=_params("parallel", "parallel"), name=name,
    )(hm, w)


def _proj_qk_kernel(x_ref, w_ref, cos_ref, sin_ref, o_ref, *, n_lat_tiles, is_q):
    i = pl.program_id(0)
    acc = jnp.dot(x_ref[...], w_ref[...], preferred_element_type=F32)
    lane = lax.broadcasted_iota(jnp.int32, (acc.shape[0], LANES), 1)
    low_rows = lax.broadcasted_iota(jnp.int32, (LANES, acc.shape[0]), 0) < ATT_DH

    def emit(rotate):
        if rotate:
            cos, sin = cos_ref[...], sin_ref[...]
            first = (lane % (2 * ROPE_FREQS)) < ROPE_FREQS
        for h in range(acc.shape[1] // LANES):
            t = acc[:, h * LANES:(h + 1) * LANES]
            if rotate:
                partner = jnp.where(first, pltpu.roll(t, LANES - ROPE_FREQS, 1), pltpu.roll(t, ROPE_FREQS, 1))
                t = t * cos + partner * sin
            if is_q:
                tt = (t * (ATT_DH ** -0.5 * LOG2_E)).T
                o_ref[h, 0] = jnp.where(low_rows, tt, 0.0).astype(o_ref.dtype)
                o_ref[h, 1] = jnp.where(low_rows, 0.0, tt).astype(o_ref.dtype)
            else:
                o_ref[:, h * LANES:(h + 1) * LANES] = t.astype(o_ref.dtype)

    @pl.when(i < n_lat_tiles)
    def _():
        emit(True)

    @pl.when(i >= n_lat_tiles)
    def _():
        emit(False)


def _rope_tables(seq):
    rows = seq // GRID_W
    row = jnp.repeat(jnp.arange(rows), GRID_W)
    col = jnp.tile(jnp.arange(GRID_W), rows)
    inv = ROPE_BASE ** (-jnp.arange(ROPE_FREQS, dtype=F32) / ROPE_FREQS)
    ar, ac = row[:, None] * inv, col[:, None] * inv
    cos = jnp.concatenate([jnp.cos(ar), jnp.cos(ar), jnp.cos(ac), jnp.cos(ac)], axis=1)
    sin = jnp.concatenate([-jnp.sin(ar), jnp.sin(ar), -jnp.sin(ac), jnp.sin(ac)], axis=1)
    return jnp.tile(cos, (1, 2)), jnp.tile(sin, (1, 2))


def _proj_qk(lay, hm, w, cos, sin, is_q):
    tps = lay.seq // TM
    kern = functools.partial(_proj_qk_kernel, n_lat_tiles=lay.n_lat // TM, is_q=is_q)
    tab = pl.BlockSpec((TM, LANES), lambda i: (i % tps, 0))
    if is_q:
        out_spec = pl.BlockSpec((ATT_HEADS, 2, LANES, TM), lambda i: (0, 0, 0, i))
        out_shape = jax.ShapeDtypeStruct((ATT_HEADS, 2, LANES, lay.n_tok), BF16)
    else:
        out_spec = pl.BlockSpec((TM, BRANCH_W), lambda i: (i, 0))
        out_shape = jax.ShapeDtypeStruct((lay.n_tok, BRANCH_W), BF16)
    return pl.pallas_call(
        kern, grid=(lay.n_tok // TM,),
        in_specs=[pl.BlockSpec((TM, D_MODEL), lambda i: (i, 0)),
                  pl.BlockSpec((D_MODEL, BRANCH_W), lambda i: (0, 0)), tab, tab],
        out_specs=out_spec, out_shape=out_shape,
        compiler_params=_params("parallel"), name="proj_q" if is_q else "proj_k",
    )(hm, w, cos, sin)


def _proj_v_kernel(x_ref, w_ref, o_ref):
    acc = jnp.dot(x_ref[...], w_ref[...], preferred_element_type=F32)
    tail = ATT_V_ROWS - ATT_DV
    ones_row = jnp.where(lax.broadcasted_iota(jnp.int32, (tail, acc.shape[0]), 0) == 0, 1.0, 0.0)
    for h in range(acc.shape[1] // LANES):
        o_ref[h, :ATT_DV, :] = acc[:, h * LANES:(h + 1) * LANES].T.astype(o_ref.dtype)
        o_ref[h, ATT_DV:, :] = ones_row.astype(o_ref.dtype)


def _proj_v(lay, hm, w):
    return pl.pallas_call(
        _proj_v_kernel, grid=(lay.n_tok // TM,),
        in_specs=[pl.BlockSpec((TM, D_MODEL), lambda i: (i, 0)),
                  pl.BlockSpec((D_MODEL, BRANCH_W), lambda i: (0, 0))],
        out_specs=pl.BlockSpec((ATT_HEADS, ATT_V_ROWS, TM), lambda i: (0, 0, i)),
        out_shape=jax.ShapeDtypeStruct((ATT_HEADS, ATT_V_ROWS, lay.n_tok), BF16),
        compiler_params=_params("parallel"), name="proj_v",
    )(hm, w)


def _attn_kernel(lamv_ref, subln_ref, q0_ref, q1_ref, kc_ref, vc_ref, *rest, n_lat_chunks, tk, lam_init):
    if n_lat_chunks:
        kl_ref, vl_ref, o_ref = rest
    else:
        _, o_ref = rest
    lv = lamv_ref[...]
    lam = (jnp.exp(jnp.sum(lv[0:1] * lv[1:2], axis=-1, keepdims=True))
           - jnp.exp(jnp.sum(lv[2:3] * lv[3:4], axis=-1, keepdims=True)) + lam_init)

    def first(qt, k, vt):
        s = jnp.dot(k, qt, preferred_element_type=F32)
        m = jnp.max(s, axis=0, keepdims=True)
        return m, jnp.dot(vt, jnp.exp2(s - m).astype(BF16), preferred_element_type=F32)

    def update(state, qt, k, vt):
        m, acc = state
        s = jnp.dot(k, qt, preferred_element_type=F32)
        m_new = jnp.maximum(m, jnp.max(s, axis=0, keepdims=True))
        e = jnp.exp2(s - m_new).astype(BF16)
        return m_new, jnp.exp2(m - m_new) * acc + jnp.dot(vt, e, preferred_element_type=F32)

    n_q = q0_ref.shape[1]
    qt = jnp.concatenate([q0_ref[...], q1_ref[...]], axis=1)
    state = first(qt, kc_ref[...], vc_ref[...])
    if n_lat_chunks:
        def body(c, st):
            off = pl.multiple_of(c * tk, tk)
            return update(st, qt, kl_ref[pl.ds(off, tk), :], vl_ref[:, pl.ds(off, tk)])

        state = lax.fori_loop(0, n_lat_chunks, body, state)
    acc = state[1]
    out = acc[:ATT_DV] / acc[ATT_DV:ATT_DV + 1]
    o = out[:, :n_q] - lam * out[:, n_q:]
    o = o * lax.rsqrt(jnp.mean(o * o, axis=0, keepdims=True) + RMS_EPS) * subln_ref[...] * (1.0 - lam_init)
    o_ref[...] = o.T.astype(o_ref.dtype)


def _diff_attention(lay, qt, k, vt, att_lam, subln, lam_init):
    b_, s_, c_ = lay.batch, lay.seq, lay.ctx
    lamv = pl.BlockSpec((4, ATT_DH), lambda *_: (0, 0))
    sub = pl.BlockSpec((ATT_DV, 1), lambda *_: (0, 0))
    out_shape = jax.ShapeDtypeStruct((lay.n_tok, BRANCH_W), BF16)
    ctx_blk = lay.n_lat // c_

    def specs(tq, q_blk):
        return [lamv, sub,
                pl.BlockSpec((None, None, LANES, tq), lambda b, h, i: (h, 0, 0, q_blk(b, i))),
                pl.BlockSpec((None, None, LANES, tq), lambda b, h, i: (h, 1, 0, q_blk(b, i))),
                pl.BlockSpec((c_, LANES), lambda b, h, i: (ctx_blk + b, h)),
                pl.BlockSpec((None, ATT_V_ROWS, c_), lambda b, h, i: (h, 0, ctx_blk + b))]

    tq, tk = min(ATT_TQ, s_), min(ATT_TK, s_)
    n_q = s_ // tq
    lat_blk = lambda b, i: b * n_q + i
    lat = pl.pallas_call(
        functools.partial(_attn_kernel, n_lat_chunks=s_ // tk, tk=tk, lam_init=lam_init),
        grid=(b_, ATT_HEADS, n_q),
        in_specs=specs(tq, lat_blk) + [
            pl.BlockSpec((s_, LANES), lambda b, h, i: (b, h)),
            pl.BlockSpec((None, ATT_V_ROWS, s_), lambda b, h, i: (h, 0, b))],
        out_specs=pl.BlockSpec((tq, LANES), lambda b, h, i: (lat_blk(b, i), h)),
        out_shape=out_shape,
        compiler_params=_params("parallel", "parallel", "arbitrary"), name="attn_latent",
    )(att_lam, subln, qt, qt, k, vt, k, vt)
    ctx_q = lambda b, i: ctx_blk + b
    return pl.pallas_call(
        functools.partial(_attn_kernel, n_lat_chunks=0, tk=0, lam_init=lam_init),
        grid=(b_, ATT_HEADS, 1),
        in_specs=specs(c_, ctx_q) + [pl.BlockSpec(memory_space=pl.ANY)],
        out_specs=pl.BlockSpec((c_, LANES), lambda b, h, i: (ctx_q(b, i), h)),
        out_shape=out_shape, input_output_aliases={6: 0},
        compiler_params=_params("parallel", "parallel", "arbitrary"), name="attn_context",
    )(att_lam, subln, qt, qt, k, vt, lat)


def _ssd_prep_kernel(prev_ref, cur_ref, next_ref, dt_ref, cw_ref, cb_ref, dtb_ref,
                     xs_ref, bm_ref, cm_ref, dts_ref, dtst_ref, ext_ref, *, tiles_per_seq, n_lat_tiles):
    i = pl.program_id(0)
    is_ctx = i >= n_lat_tiles
    first = jnp.logical_or(is_ctx, i % tiles_per_seq == 0)
    last = jnp.logical_or(is_ctx, i % tiles_per_seq == tiles_per_seq - 1)
    tm = cur_ref.shape[0]
    ext_ref[0:SUBLANES, :] = jnp.where(first, 0.0, prev_ref[...])
    ext_ref[SUBLANES:SUBLANES + tm, :] = cur_ref[...]
    ext_ref[SUBLANES + tm:, :] = jnp.where(last, 0.0, next_ref[...])
    acc = jnp.zeros(cur_ref.shape, F32) + cb_ref[...]
    for k in range(SSD_CONV):
        start = SUBLANES + k - SSD_CONV // 2
        acc = acc + ext_ref[start:start + tm, :] * cw_ref[k:k + 1, :]
    act = acc * jax.nn.sigmoid(acc)
    xs_ref[...] = act[:, :SSD_INNER]
    bm_ref[...] = act[:, SSD_INNER:SSD_INNER + SSD_BC_W].astype(bm_ref.dtype)
    cm_ref[...] = act[:, SSD_INNER + SSD_BC_W:].astype(cm_ref.dtype)
    x = dt_ref[...] + dtb_ref[...]
    sp = jnp.maximum(x, 0.0) + jnp.log1p(jnp.exp(-jnp.abs(x)))
    dts_ref[0] = sp
    dts_ref[1] = pltpu.roll(sp, DT_PAD - SSD_HEADS, 1)
    spt = sp.T
    dtst_ref[0] = spt[0:SSD_HEADS]
    dtst_ref[1] = spt[SSD_HEADS:2 * SSD_HEADS]


def _ssd_prep(lay, xbc, dt, conv_w, conv_b, dt_bias):
    n, tm = lay.n_tok, SSD_TM
    sub_per_tile = tm // SUBLANES
    n_sub = n // SUBLANES
    kern = functools.partial(_ssd_prep_kernel, tiles_per_seq=lay.seq // tm, n_lat_tiles=lay.n_lat // tm)
    row = lambda w: pl.BlockSpec((tm, w), lambda i: (i, 0))
    return pl.pallas_call(
        kern, grid=(n // tm,),
        in_specs=[pl.BlockSpec((SUBLANES, SSD_XBC_W), lambda i: (jnp.maximum(i * sub_per_tile - 1, 0), 0)),
                  row(SSD_XBC_W),
                  pl.BlockSpec((SUBLANES, SSD_XBC_W),
                               lambda i: (jnp.minimum((i + 1) * sub_per_tile, n_sub - 1), 0)),
                  row(DT_PAD),
                  pl.BlockSpec((SUBLANES, SSD_XBC_W), lambda i: (0, 0)),
                  pl.BlockSpec((1, SSD_XBC_W), lambda i: (0, 0)),
                  pl.BlockSpec((1, DT_PAD), lambda i: (0, 0))],
        out_specs=[row(SSD_INNER), row(SSD_BC_W), row(SSD_BC_W),
                   pl.BlockSpec((2, tm, DT_PAD), lambda i: (0, i, 0)),
                   pl.BlockSpec((2, SSD_HEADS, tm), lambda i: (0, 0, i))],
        out_shape=[jax.ShapeDtypeStruct((n, SSD_INNER), F32),
                   jax.ShapeDtypeStruct((n, SSD_BC_W), BF16),
                   jax.ShapeDtypeStruct((n, SSD_BC_W), BF16),
                   jax.ShapeDtypeStruct((2, n, DT_PAD), F32),
                   jax.ShapeDtypeStruct((2, SSD_HEADS, n), F32)],
        scratch_shapes=[pltpu.VMEM((tm + 2 * SUBLANES, SSD_XBC_W), F32)],
        compiler_params=_params("parallel"), name="ssd_prep",
    )(xbc, xbc, xbc, dt, conv_w, conv_b, dt_bias)


def _split_dot(a, x, x_is_lhs=False):
    out = None
    r = x
    for _ in range(3):
        t = r.astype(BF16)
        r = r - t.astype(F32)
        d = (jnp.dot(t, a, preferred_element_type=F32) if x_is_lhs
             else jnp.dot(a, t, preferred_element_type=F32))
        out = d if out is None else out + d
    return out


def _ssd_kernel(xs_ref, bm_ref, cm_ref, dts_ref, dtst_ref, arow_ref, acol_ref, y_ref, h_ref):
    sign = 1 - 2 * pl.program_id(1)

    @pl.when(pl.program_id(2) == 0)
    def _():
        h_ref[...] = jnp.zeros(h_ref.shape, F32)

    ll = xs_ref.shape[0]
    r = lax.broadcasted_iota(jnp.int32, (ll, ll), 0)
    c = lax.broadcasted_iota(jnp.int32, (ll, ll), 1)
    mask = (r - c) * sign >= 0
    mask_t = (r - c) * sign <= 0
    one_hot = lambda m: jnp.where(m, 1.0, 0.0).astype(BF16)
    dts_t = dtst_ref[...]
    da = dts_ref[...] * arow_ref[...]
    da_t = dts_t * acol_ref[...]
    cs = _split_dot(one_hot(mask), da)
    cs_t = _split_dot(one_hot(mask_t), da_t, x_is_lhs=True)
    tot = jnp.sum(da, axis=0, keepdims=True)
    from_start = jnp.exp(cs)
    chunk_decay = jnp.exp(tot)
    src_w_t = dts_t * jnp.exp(jnp.sum(da_t, axis=1, keepdims=True) - cs_t)
    lane = lax.broadcasted_iota(jnp.int32, (ll, LANES), 1)
    lo_half = lane < SSD_P

    def pair_expand(t, hd):
        rows = t.shape[0]
        return jnp.where(lo_half[:rows], jnp.broadcast_to(t[:, hd:hd + 1], (rows, LANES)),
                         jnp.broadcast_to(t[:, hd + 1:hd + 2], (rows, LANES)))

    gw = SSD_HPG * SSD_P
    for g in range(SSD_GROUPS):
        bg = bm_ref[:, g * SSD_N:(g + 1) * SSD_N]
        cg = cm_ref[:, g * SSD_N:(g + 1) * SSD_N]
        cb = lax.dot_general(cg, bg, (((1,), (1,)), ((), ())), preferred_element_type=F32)
        bg_t = bg.astype(F32).T
        y_parts, st_parts, fs_parts, cd_parts = [], [], [], []
        for pr in range(SSD_HPG // 2):
            hd = g * SSD_HPG + 2 * pr
            xp = xs_ref[:, hd * SSD_P:(hd + 2) * SSD_P]
            fs_parts.append(pair_expand(from_start, hd))
            cd_parts.append(pair_expand(chunk_decay, hd))
            yp = sp = None
            for k in range(2):
                e = hd + k
                seg = cs[:, e:e + 1] - cs_t[e:e + 1, :]
                dec = jnp.where(mask, jnp.exp(jnp.where(mask, seg, 0.0)), 0.0)
                w = (cb * dec * dts_t[e:e + 1, :]).astype(BF16)
                xk = jnp.where(lo_half if k == 0 else jnp.logical_not(lo_half), xp, 0.0).astype(BF16)
                d = jnp.dot(w, xk, preferred_element_type=F32)
                yp = d if yp is None else yp + d
                s = jnp.dot((bg_t * src_w_t[e:e + 1, :]).astype(BF16), xk, preferred_element_type=F32)
                sp = s if sp is None else sp + s
            y_parts.append(yp)
            st_parts.append(sp)
        h_prev = h_ref[g]
        y_off = jnp.dot(cg, h_prev.astype(BF16), preferred_element_type=F32) * jnp.concatenate(fs_parts, axis=1)
        y_ref[:, g * gw:(g + 1) * gw] = jnp.concatenate(y_parts, axis=1) + y_off
        h_ref[g] = h_prev * jnp.concatenate(cd_parts, axis=1) + jnp.concatenate(st_parts, axis=1)


def _ssd_scan(lay, xs, bm, cm, dts, dtst, a_row, a_col):
    ll = SSD_L
    n_ctx, n_lat = lay.ctx // ll, lay.seq // ll
    ctx_base = lay.n_lat // ll

    def rb(b, d, s):
        cstep = jnp.where(d == 0, s, n_ctx - 1 - s)
        lstep = jnp.where(d == 0, s - n_ctx, n_lat - 1 - (s - n_ctx))
        return jnp.where(s < n_ctx, ctx_base + b * n_ctx + cstep, b * n_lat + lstep)

    row = lambda w: pl.BlockSpec((ll, w), lambda b, d, s: (rb(b, d, s), 0))
    return pl.pallas_call(
        _ssd_kernel, grid=(lay.batch, 2, n_ctx + n_lat),
        in_specs=[row(SSD_INNER), row(SSD_BC_W), row(SSD_BC_W),
                  pl.BlockSpec((None, ll, DT_PAD), lambda b, d, s: (d, rb(b, d, s), 0)),
                  pl.BlockSpec((None, SSD_HEADS, ll), lambda b, d, s: (d, 0, rb(b, d, s))),
                  pl.BlockSpec((None, 1, DT_PAD), lambda b, d, s: (d, 0, 0)),
                  pl.BlockSpec((None, SSD_HEADS, 1), lambda b, d, s: (d, 0, 0))],
        out_specs=pl.BlockSpec((None, ll, SSD_INNER), lambda b, d, s: (d, rb(b, d, s), 0)),
        out_shape=jax.ShapeDtypeStruct((2, lay.n_tok, SSD_INNER), F32),
        scratch_shapes=[pltpu.VMEM((SSD_GROUPS, SSD_N, SSD_HPG * SSD_P), F32)],
        compiler_params=_params("parallel", "parallel", "arbitrary"), name="ssd_scan",
    )(xs, bm, cm, dts, dtst, a_row, a_col)


def _ssd_gate_norm_kernel(y0_ref, y1_ref, xs_ref, z_ref, dsk_ref, nw_ref, o_ref):
    z = z_ref[...]
    gated = (y0_ref[...] + y1_ref[...] + xs_ref[...] * dsk_ref[...]) * (z * jax.nn.sigmoid(z))
    nw = nw_ref[...]
    for g in range(SSD_GROUPS):
        sl = slice(g * SSD_NORM_GROUP, (g + 1) * SSD_NORM_GROUP)
        t = gated[:, sl]
        o_ref[:, sl] = (t * lax.rsqrt(jnp.mean(t * t, axis=-1, keepdims=True) + RMS_EPS)
                        * nw[:, sl]).astype(o_ref.dtype)


def _ssd_gate_norm(lay, ydir, xs, z, dsk, nw):
    tm = SSD_TM
    row = pl.BlockSpec((tm, SSD_INNER), lambda i: (i, 0))
    par = pl.BlockSpec((1, SSD_INNER), lambda i: (0, 0))
    return pl.pallas_call(
        _ssd_gate_norm_kernel, grid=(lay.n_tok // tm,),
        in_specs=[pl.BlockSpec((None, tm, SSD_INNER), lambda i: (0, i, 0)),
                  pl.BlockSpec((None, tm, SSD_INNER), lambda i: (1, i, 0)), row, row, par, par],
        out_specs=row, out_shape=jax.ShapeDtypeStruct((lay.n_tok, SSD_INNER), BF16),
        compiler_params=_params("parallel"), name="ssd_gate_norm",
    )(ydir, ydir, xs, z, dsk, nw)


def _s5_operators(lam_re, lam_im, log_step, b_re, b_im, c_re, c_im):
    hp = lax.Precision.HIGHEST
    ll, hh = S5_L, S5_GROUP_CH
    step = jnp.exp(log_step)[..., None, None]
    d = jnp.arange(ll + 1, dtype=F32)
    p_mag = jnp.exp(lam_re[..., None] * step * d)
    p_ang = lam_im[..., None] * step * d
    p_re, p_im = p_mag * jnp.cos(p_ang), p_mag * jnp.sin(p_ang)
    ab_re, ab_im = p_re[..., 1], p_im[..., 1]
    den = lam_re * lam_re + lam_im * lam_im
    k_re = ((ab_re - 1.0) * lam_re + ab_im * lam_im) / den
    k_im = (ab_im * lam_re - (ab_re - 1.0) * lam_im) / den
    bb_re = k_re[..., None] * b_re - k_im[..., None] * b_im
    bb_im = k_re[..., None] * b_im + k_im[..., None] * b_re
    cp_re = c_re[..., None] * p_re[:, :, None] - c_im[..., None] * p_im[:, :, None]
    cp_im = c_re[..., None] * p_im[:, :, None] + c_im[..., None] * p_re[:, :, None]
    kern = (jnp.einsum('zghnd,zgnk->zgdhk', cp_re, bb_re, precision=hp)
            - jnp.einsum('zghnd,zgnk->zgdhk', cp_im, bb_im, precision=hp))
    s_idx = jnp.arange(ll)[:, None]
    l_idx = jnp.arange(ll)[None, :]
    t_f = jnp.where((l_idx >= s_idx)[None, :, :, None, None], kern[0][:, jnp.clip(l_idx - s_idx, 0, ll)], 0.0)
    t_b = jnp.where((s_idx >= l_idx)[None, :, :, None, None], kern[1][:, jnp.clip(s_idx - l_idx, 0, ll)], 0.0)
    toep = (t_f + t_b).transpose(0, 1, 4, 2, 3).reshape(S5_GROUPS, S5_CW, S5_CW)

    def state_in(z, powers):
        pr, pi = p_re[z][..., powers], p_im[z][..., powers]
        re = pr[..., None] * bb_re[z][:, :, None] - pi[..., None] * bb_im[z][:, :, None]
        im = pr[..., None] * bb_im[z][:, :, None] + pi[..., None] * bb_re[z][:, :, None]
        re = re.transpose(0, 2, 3, 1).reshape(S5_GROUPS, S5_CW, S5_N)
        im = im.transpose(0, 2, 3, 1).reshape(S5_GROUPS, S5_CW, S5_N)
        return jnp.concatenate([re, im], axis=-1)

    ws_f = state_in(0, ll - 1 - jnp.arange(ll))
    ws_b = state_in(1, jnp.arange(ll))
    w1 = jnp.concatenate([toep, ws_f, ws_b], axis=-1)

    def state_out(z, powers):
        re = cp_re[z][..., powers].transpose(0, 2, 3, 1).reshape(S5_GROUPS, S5_N, S5_CW)
        im = cp_im[z][..., powers].transpose(0, 2, 3, 1).reshape(S5_GROUPS, S5_N, S5_CW)
        return jnp.concatenate([re, -im], axis=1)

    wo = jnp.concatenate([state_out(0, jnp.arange(ll) + 1), state_out(1, ll - jnp.arange(ll))], axis=1)
    ar, ai = p_re[..., ll], p_im[..., ll]
    a1 = jnp.concatenate([ar, ar], axis=-1)
    a2 = jnp.concatenate([-ai, ai], axis=-1)
    zeros = jnp.zeros_like(a1[0])
    av = jnp.stack([a1[0], a2[0], a1[1], a2[1], zeros, zeros, zeros, zeros], axis=1)
    return w1.astype(BF16), wo.astype(BF16), av


def _s5_kernel(x_ref, w1_ref, wo_ref, av_ref, y_ref, u_scr, y_scr, sf_ref, sb_ref, *, n_ctx_tiles, n_tiles):
    rows = x_ref.shape[0]
    row_tile = rows // S5_ROW_TILES
    nst = 2 * S5_N
    lane_blk = lax.broadcasted_iota(jnp.int32, (row_tile, LANES), 1) // S5_GROUP_CH

    def block_transpose(arrs):
        a = list(arrs)
        k = S5_GB // 2
        while k:
            bit = (lane_blk & k) != 0
            for i in range(S5_GB):
                if not i & k:
                    lo, hi = a[i], a[i + k]
                    a[i] = jnp.where(bit, pltpu.roll(hi, k * S5_GROUP_CH, 1), lo)
                    a[i + k] = jnp.where(bit, hi, pltpu.roll(lo, LANES - k * S5_GROUP_CH, 1))
            k //= 2
        return a

    def gather(r, _):
        r0 = pl.multiple_of(r * row_tile, BF16_ROWS)
        for hv in range(S5_L // S5_GB):
            per_group = block_transpose(
                [x_ref[pl.ds(r0, row_tile), hv * S5_GB + sl, :] for sl in range(S5_GB)])
            for g in range(S5_GB):
                u_scr[g, pl.ds(r0, row_tile), hv * LANES:(hv + 1) * LANES] = per_group[g].astype(u_scr.dtype)
        return 0

    lax.fori_loop(0, S5_ROW_TILES, gather, 0)

    for g in range(S5_GB):
        p = jnp.dot(u_scr[g], w1_ref[g], preferred_element_type=F32)
        y_scr[g] = p[:, :S5_CW]
        sf_ref[:, g * nst:(g + 1) * nst] = p[:, S5_CW:S5_CW + nst]
        sb_ref[:, g * nst:(g + 1) * nst] = p[:, S5_CW + nst:]

    wide = S5_GB * nst
    half = SUBLANES // 2
    coef = lambda k: jnp.broadcast_to(
        jnp.concatenate([av_ref[g][k:k + 1] for g in range(S5_GB)], axis=1), (half, wide))
    a1f, a2f, a1b, a2b = coef(0), coef(1), coef(2), coef(3)
    low = lax.broadcasted_iota(jnp.int32, (SUBLANES, wide), 1) % nst < S5_N
    swap = lambda t: jnp.where(low, pltpu.roll(t, wide - S5_N, 1), pltpu.roll(t, S5_N, 1))

    def body(j, carry):
        hf, hfs, hb, hbs = carry
        of = pl.multiple_of(j * SUBLANES, SUBLANES)
        s = sf_ref[pl.ds(of, SUBLANES), :]
        ss = swap(s)
        h1 = a1f * hf + a2f * hfs + s[:half]
        h1s = a1f * hfs - a2f * hf + ss[:half]
        sf_ref[pl.ds(of, SUBLANES), :] = jnp.concatenate([hf, h1], axis=0)
        h2 = a1f * h1 + a2f * h1s + s[half:]
        h2s = a1f * h1s - a2f * h1 + ss[half:]
        jb = jnp.where(j < n_ctx_tiles, n_ctx_tiles - 1 - j, n_tiles - 1 - (j - n_ctx_tiles))
        ob = pl.multiple_of(jb * SUBLANES, SUBLANES)
        s = sb_ref[pl.ds(ob, SUBLANES), :]
        ss = swap(s)
        g1 = a1b * hb + a2b * hbs + s[half:]
        g1s = a1b * hbs - a2b * hb + ss[half:]
        sb_ref[pl.ds(ob, SUBLANES), :] = jnp.concatenate([g1, hb], axis=0)
        g2 = a1b * g1 + a2b * g1s + s[:half]
        g2s = a1b * g1s - a2b * g1 + ss[:half]
        return h2, h2s, g2, g2s

    z = jnp.zeros((half, wide), F32)
    lax.fori_loop(0, n_tiles, body, (z, z, z, z))

    for g in range(S5_GB):
        wo = wo_ref[g]
        y_scr[g] += (
            jnp.dot(sf_ref[:, g * nst:(g + 1) * nst].astype(BF16), wo[:nst], preferred_element_type=F32)
            + jnp.dot(sb_ref[:, g * nst:(g + 1) * nst].astype(BF16), wo[nst:], preferred_element_type=F32))

    def scatter(r, _):
        r0 = pl.multiple_of(r * row_tile, BF16_ROWS)
        for hv in range(S5_L // S5_GB):
            per_lag = block_transpose(
                [y_scr[g, pl.ds(r0, row_tile), hv * LANES:(hv + 1) * LANES] for g in range(S5_GB)])
            for sl in range(S5_GB):
                y_ref[pl.ds(r0, row_tile), hv * S5_GB + sl, :] = per_lag[sl]
        return 0

    lax.fori_loop(0, S5_ROW_TILES, scatter, 0)


def _s5_chunked(lay, x, w1, wo, av):
    rows = lay.n_tok // S5_L
    n_ctx_rows = lay.batch * lay.ctx // S5_L
    assert rows % (S5_ROW_TILES * BF16_ROWS) == 0
    kern = functools.partial(_s5_kernel, n_ctx_tiles=n_ctx_rows // SUBLANES, n_tiles=rows // SUBLANES)
    blk = pl.BlockSpec((rows, None, S5_L, LANES), lambda g: (0, g, 0, 0), pipeline_mode=pl.Buffered(1))
    return pl.pallas_call(
        kern, grid=(S5_NB,),
        in_specs=[blk,
                  pl.BlockSpec((S5_GB, S5_CW, S5_CW + 4 * S5_N), lambda g: (g, 0, 0)),
                  pl.BlockSpec((S5_GB, 4 * S5_N, S5_CW), lambda g: (g, 0, 0)),
                  pl.BlockSpec((S5_GB, SUBLANES, 2 * S5_N), lambda g: (g, 0, 0))],
        out_specs=blk,
        out_shape=jax.ShapeDtypeStruct((rows, S5_NB, S5_L, LANES), F32),
        scratch_shapes=[pltpu.VMEM((S5_GB, rows, S5_CW), BF16), pltpu.VMEM((S5_GB, rows, S5_CW), F32),
                        pltpu.VMEM((rows, S5_GB * 2 * S5_N), F32), pltpu.VMEM((rows, S5_GB * 2 * S5_N), F32)],
        compiler_params=_params("parallel"), name="s5_chunked",
    )(x, w1, wo, av)


def _s5_tile_index(lay, i):
    n_lat_tiles, per_seq = lay.n_lat // S5_TM, lay.seq // S5_TM
    return jnp.where(i < n_lat_tiles, 1 + i % per_seq, 0), jnp.where(i < n_lat_tiles, i // per_seq, i - n_lat_tiles)


def _s5_chunk_spec(lay):
    return pl.BlockSpec((S5_TM // S5_L, None, S5_NB, S5_L, LANES),
                        lambda i: (*_s5_tile_index(lay, i), 0, 0, 0))


def _proj_u_kernel(x_ref, w_ref, o_ref):
    acc = jnp.dot(x_ref[...], w_ref[...], preferred_element_type=F32)
    for nb in range(S5_NB):
        o_ref[:, nb] = acc[:, nb * LANES:(nb + 1) * LANES].reshape(S5_TM // S5_L, S5_L, LANES)


def _proj_u(lay, hm, w):
    chunks = (lay.seq + lay.ctx) // S5_L
    return pl.pallas_call(
        _proj_u_kernel, grid=(lay.n_tok // S5_TM,),
        in_specs=[pl.BlockSpec((S5_TM, D_MODEL), lambda i: (i, 0)),
                  pl.BlockSpec((D_MODEL, S5_CH), lambda i: (0, 0))],
        out_specs=_s5_chunk_spec(lay),
        out_shape=jax.ShapeDtypeStruct((chunks, lay.batch, S5_NB, S5_L, LANES), F32),
        compiler_params=_params("parallel"), name="proj_u",
    )(hm, w)


def _s5_glu_kernel(ys_ref, u_ref, dsk_ref, w_ref, b_ref, o_ref):
    natural = lambda ref: jnp.concatenate([ref[:, nb].reshape(S5_TM, LANES) for nb in range(S5_NB)], axis=1)
    t = natural(ys_ref) + natural(u_ref) * dsk_ref[...]
    t = 0.5 * t * (1.0 + jnp.tanh(math.sqrt(2.0 / math.pi) * (t + 0.044715 * (t * t * t))))
    gate = jnp.dot(t.astype(BF16), w_ref[...], preferred_element_type=F32) + b_ref[...]
    o_ref[...] = (t * jax.nn.sigmoid(gate)).astype(o_ref.dtype)


def _s5_glu(lay, ys, u, dsk, w, b):
    par = pl.BlockSpec((1, S5_CH), lambda i: (0, 0))
    return pl.pallas_call(
        _s5_glu_kernel, grid=(lay.n_tok // S5_TM,),
        in_specs=[_s5_chunk_spec(lay), _s5_chunk_spec(lay), par,
                  pl.BlockSpec((S5_CH, S5_CH), lambda i: (0, 0)), par],
        out_specs=pl.BlockSpec((S5_TM, S5_CH), lambda i: (i, 0)),
        out_shape=jax.ShapeDtypeStruct((lay.n_tok, S5_CH), BF16),
        compiler_params=_params("parallel"), name="s5_glu",
    )(ys, u, dsk, w, b)


def _merge_kernel(oa_ref, os_ref, o5_ref, ga_ref, gs_ref, g5_ref, wa_ref, ws_ref, w5_ref, o_ref):
    acc = None
    for o, g, w in ((oa_ref, ga_ref, wa_ref), (os_ref, gs_ref, ws_ref), (o5_ref, g5_ref, w5_ref)):
        t = jax.nn.sigmoid(g[...]) * jnp.dot(o[...], w[...], preferred_element_type=F32)
        acc = t if acc is None else acc + t
    o_ref[...] = acc.astype(o_ref.dtype)


def _merge(rows, o_att, o_ssd, o_s5, g, w_branch):
    nt = D_MODEL // TN_MERGE
    row = pl.BlockSpec((TM, BRANCH_W), lambda i, j: (i, 0))
    gate = lambda k: pl.BlockSpec((TM, TN_MERGE), lambda i, j: (i, k * nt + j))
    wb = lambda k: pl.BlockSpec((None, BRANCH_W, TN_MERGE), lambda i, j: (k, 0, j))
    return pl.pallas_call(
        _merge_kernel, grid=(rows // TM, nt),
        in_specs=[row, row, row, gate(0), gate(1), gate(2), wb(0), wb(1), wb(2)],
        out_specs=pl.BlockSpec((TM, TN_MERGE), lambda i, j: (i, j)),
        out_shape=jax.ShapeDtypeStruct((rows, D_MODEL), BF16),
        compiler_params=_params("parallel", "parallel"), name="branch_merge",
    )(o_att, o_ssd, o_s5, g, g, g, w_branch, w_branch, w_branch)


def _out_norm_kernel(mx_ref, h_ref, w_ref, gate_ref, lng_ref, lnb_ref, nsh_ref, nsc_ref, ho_ref, hmo_ref):
    y = jnp.dot(mx_ref[...], w_ref[...], preferred_element_type=F32)
    _post_norm_emit(h_ref[...], gate_ref[...] * y, lng_ref[...], lnb_ref[...], nsh_ref[...], nsc_ref[...],
                    ho_ref, hmo_ref)


def _out_norm(lay, rows, mixed, h, w_out, gate, lng, lnb, nsh, nsc):
    vec = pl.BlockSpec((None, 1, D_MODEL), lambda i: (lay.sample_of_tile(i, TM), 0, 0))
    par = pl.BlockSpec((1, D_MODEL), lambda i: (0, 0))
    row = pl.BlockSpec((TM, D_MODEL), lambda i: (i, 0))
    return pl.pallas_call(
        _out_norm_kernel, grid=(rows // TM,),
        in_specs=[row, row, pl.BlockSpec((D_MODEL, D_MODEL), lambda i: (0, 0)), vec, par, par, vec, vec],
        out_specs=[row, row],
        out_shape=[jax.ShapeDtypeStruct((rows, D_MODEL), F32),
                   jax.ShapeDtypeStruct((rows, D_MODEL), BF16)],
        compiler_params=_params("parallel"), name="out_norm",
    )(mixed, h, w_out, gate, lng, lnb, nsh, nsc)


def _pad_cols(t, width):
    return jnp.pad(t, [(0, 0)] * (t.ndim - 1) + [(0, width - t.shape[-1])])


def _token_mixer(lay, rows_out, hm, rope, lam_init, w_in, att_lam, att_subln, conv_w, conv_b, a_log, dt_bias,
                 ssd_d, ssd_norm, s5_ops, s5_d, glu_w, glu_b, w_branch):
    cuts = [0]
    for w in (BRANCH_W, BRANCH_W, BRANCH_W, SSD_INNER, SSD_XBC_W, 2 * SSD_HEADS, S5_CH, N_BRANCH * D_MODEL):
        cuts.append(cuts[-1] + w)
    w_q, w_k, w_v, w_z, w_xbc, w_dt, w_u, w_g = (
        w_in[:, a:b].astype(BF16) for a, b in zip(cuts[:-1], cuts[1:]))
    qt = _proj_qk(lay, hm, w_q, *rope, is_q=True)
    k = _proj_qk(lay, hm, w_k, *rope, is_q=False)
    vt = _proj_v(lay, hm, w_v)
    z = _proj(lay, hm, w_z, F32, "proj_z")
    xbc = _proj(lay, hm, w_xbc, F32, "proj_xbc")
    dt = _proj(lay, hm, _pad_cols(w_dt, DT_PAD), F32, "proj_dt")
    u = _proj_u(lay, hm, w_u)
    g = _proj(lay, hm, w_g, F32, "proj_gate")

    o_att = _diff_attention(lay, qt, k, vt, att_lam, att_subln.reshape(ATT_DV, 1), lam_init)

    conv_w8 = jnp.pad(conv_w, ((0, SUBLANES - SSD_CONV), (0, 0)))
    xs, bm, cm, dts, dtst = _ssd_prep(lay, xbc, dt, conv_w8, conv_b.reshape(1, -1),
                                      _pad_cols(dt_bias.reshape(1, -1), DT_PAD))
    a = -jnp.exp(a_log.astype(F32))
    ydir = _ssd_scan(lay, xs, bm, cm, dts, dtst, _pad_cols(a, DT_PAD)[:, None, :], a[:, :, None])
    o_ssd = _ssd_gate_norm(lay, ydir, xs, z, jnp.repeat(ssd_d, SSD_P).reshape(1, -1),
                           ssd_norm.reshape(1, -1))

    ys = _s5_chunked(lay, u.reshape(lay.n_tok // S5_L, S5_NB, S5_L, LANES), *s5_ops)
    o_s5 = _s5_glu(lay, ys.reshape(u.shape), u, s5_d.reshape(1, -1), glu_w.astype(BF16),
                   glu_b.reshape(1, -1))

    return _merge(rows_out, o_att, o_ssd, o_s5, g, w_branch.astype(BF16))


def _trunk(lay, x, c, ctx, c_ctx, w_mod, b_mod, ln_g, ln_b, ffn_w1, ffn_w3, ffn_w2, w_in,
           att_lam, att_subln, ssd_conv_w, ssd_conv_b, ssd_a_log, ssd_dt_bias, ssd_d, ssd_norm,
           s5_lam_re, s5_lam_im, s5_log_step, s5_b_re, s5_b_im, s5_c_re, s5_c_im,
           s5_d, s5_glu_w, s5_glu_b, w_branch, w_out):
    depth = w_mod.shape[0]
    h = jnp.concatenate([x.reshape(lay.n_lat, D_MODEL), ctx.reshape(-1, D_MODEL)], axis=0)
    cc = jnp.concatenate([c, c_ctx[None], jnp.zeros((MOD_ROWS - lay.batch - 1, D_MODEL), F32)], axis=0)
    mod = _mod_all(cc, w_mod, b_mod).reshape(depth, MOD_ROWS, N_MOD, 1, D_MODEL)
    mvec = lambda l, k: mod[l, :, k]
    zero_vec = jnp.zeros((MOD_ROWS, 1, D_MODEL), F32)
    rope = _rope_tables(lay.seq)
    lnp = lambda l, k: (ln_g[l, k].reshape(1, -1), ln_b[l, k].reshape(1, -1))

    ffn_w = (ffn_w1.astype(BF16), ffn_w3.astype(BF16), ffn_w2.astype(BF16))
    hm = _modulate(lay, h, mvec(0, 0), mvec(0, 1))
    for l in range(depth):
        lam_init = LAMBDA_INIT_BASE - LAMBDA_INIT_SPAN * math.exp(-LAMBDA_INIT_RATE * l)
        last = l + 1 == depth
        rows = lay.n_lat if last else lay.n_tok
        h, hm = _half_ffn(lay, lay.n_tok, hm, h, *ffn_w, l, 0, mvec(l, 2), *lnp(l, 0), mvec(l, 3), mvec(l, 4))
        s5_ops = _s5_operators(s5_lam_re[l], s5_lam_im[l], s5_log_step[l], s5_b_re[l], s5_b_im[l],
                               s5_c_re[l], s5_c_im[l])
        mixed = _token_mixer(lay, rows, hm, rope, lam_init, w_in[l], att_lam[l], att_subln[l],
                             ssd_conv_w[l], ssd_conv_b[l], ssd_a_log[l], ssd_dt_bias[l], ssd_d[l],
                             ssd_norm[l], s5_ops, s5_d[l], s5_glu_w[l], s5_glu_b[l], w_branch[l])
        h, hm = _out_norm(lay, rows, mixed, h, w_out[l].astype(BF16), mvec(l, 5), *lnp(l, 1),
                          mvec(l, 6), mvec(l, 7))
        nxt = (zero_vec, zero_vec) if last else (mvec(l + 1, 0), mvec(l + 1, 1))
        h, hm = _half_ffn(lay, rows, hm, h, *ffn_w, l, 1, mvec(l, 8), *lnp(l, 2), *nxt)
    return h.reshape(x.shape)


def kernel(x, c, ctx, c_ctx, w_mod, b_mod, ln_g, ln_b, ffn_w1, ffn_w3, ffn_w2, w_in, att_lam, att_subln, ssd_conv_w, ssd_conv_b, ssd_a_log, ssd_dt_bias, ssd_d, ssd_norm, s5_lam_re, s5_lam_im, s5_log_step, s5_b_re, s5_b_im, s5_c_re, s5_c_im, s5_d, s5_glu_w, s5_glu_b, w_branch, w_out):
    lay = Layout(x.shape[0], x.shape[1], ctx.shape[1])
    return _trunk(lay, x, c, ctx, c_ctx, w_mod, b_mod, ln_g, ln_b, ffn_w1, ffn_w3, ffn_w2, w_in,
                  att_lam, att_subln, ssd_conv_w, ssd_conv_b, ssd_a_log, ssd_dt_bias, ssd_d, ssd_norm,
                  s5_lam_re, s5_lam_im, s5_log_step, s5_b_re, s5_b_im, s5_c_re, s5_c_im,
                  s5_d, s5_glu_w, s5_glu_b, w_branch, w_out)
```

```python
import functools
import math

import jax
import jax.numpy as jnp
from jax import lax
from jax.experimental import pallas as pl
from jax.experimental.pallas import tpu as pltpu

F32 = jnp.float32
BF16 = jnp.bfloat16
LOG2_E = math.log2(math.e)

D_MODEL = 2048
DEPTH = 2
GRID_W = 64
DN_ALPHA = (2 * DEPTH) ** 0.25
N_SUB = 3
N_MOD = 3 * N_SUB
FFN_HALF = 0.5
D_FF = 5632
LN_EPS = 1e-5
RMS_EPS = 1e-6
BRANCH_W = D_MODEL // 2
N_BRANCH = 3
ATT_DH = 64
ATT_DV = 2 * ATT_DH
ATT_HEADS = BRANCH_W // ATT_DV
ROPE_BASE = 10000.0
ROPE_FREQS = ATT_DH // 4
LAMBDA_INIT_BASE = 0.8
LAMBDA_INIT_SPAN = 0.6
LAMBDA_INIT_RATE = 0.3
SSD_P = 64
SSD_HEADS = BRANCH_W // SSD_P
SSD_GROUPS = 4
SSD_HPG = SSD_HEADS // SSD_GROUPS
SSD_N = 128
SSD_CONV = 5
SSD_INNER = SSD_HEADS * SSD_P
SSD_BC_W = SSD_GROUPS * SSD_N
SSD_XBC_W = SSD_INNER + 2 * SSD_BC_W
SSD_NORM_GROUP = SSD_INNER // SSD_GROUPS
S5_CH = BRANCH_W
S5_GROUP_CH = 16
S5_GROUPS = S5_CH // S5_GROUP_CH
S5_N = 64

LANES = 128
SUBLANES = 8
VMEM_LIMIT_BYTES = 56 * 1024 * 1024
MOD_ROWS = 8
TM = 512
TN_FF = 512
TN_PROJ = 1024
TM_PROJ = 1024
TN_MERGE = 512
TN_MOD = 1024
ATT_V_ROWS = ATT_DV + 16
ATT_TQ = 1024
ATT_TK = 2048
SSD_L = 128
SSD_TM = 256
S5_L = 16
S5_CW = S5_L * S5_GROUP_CH
S5_GB = LANES // S5_GROUP_CH
S5_NB = S5_GROUPS // S5_GB
S5_TM = 256
S5_ROW_TILES = 4
BF16_ROWS = 2 * SUBLANES
DT_PAD = LANES


def _params(*sem):
    return pltpu.CompilerParams(dimension_semantics=sem, vmem_limit_bytes=VMEM_LIMIT_BYTES)


class Layout:
    def __init__(self, batch, seq, ctx):
        self.batch, self.seq, self.ctx = batch, seq, ctx
        self.n_lat = batch * seq
        self.n_tok = batch * (seq + ctx)
        for t in (TM, SSD_TM, SSD_L):
            assert seq % t == 0 and (batch * ctx) % t == 0, (seq, ctx, t)
        assert ctx == SSD_TM and ctx % SSD_L == 0 and ctx == S5_TM and seq % S5_TM == 0
        assert seq % min(ATT_TQ, seq) == 0 and seq % min(ATT_TK, seq) == 0
        assert batch * 2 == SUBLANES and seq % GRID_W == 0

    def sample_of_tile(self, i, tile):
        return jnp.where(i < self.n_lat // tile, i // (self.seq // tile), self.batch)


def _post_norm_emit(h, upd, lng, lnb, nsh, nsc, ho_ref, hmo_ref):
    t = DN_ALPHA * h + upd
    mu = jnp.mean(t, axis=-1, keepdims=True)
    tc = t - mu
    var = jnp.mean(tc * tc, axis=-1, keepdims=True)
    hn = tc * lax.rsqrt(var + LN_EPS) * lng + lnb
    ho_ref[...] = hn
    hmo_ref[...] = (hn * (1.0 + nsc) + nsh).astype(hmo_ref.dtype)


def _mod_kernel(c_ref, w_ref, b_ref, o_ref):
    c = c_ref[...]
    s = (c * jax.nn.sigmoid(c)).astype(BF16)
    o_ref[...] = jnp.dot(s, w_ref[...].astype(BF16), preferred_element_type=F32) + b_ref[...]


def _mod_all(cc, w_mod, b_mod):
    depth, d, n = w_mod.shape
    return pl.pallas_call(
        _mod_kernel,
        grid=(depth, n // TN_MOD),
        in_specs=[pl.BlockSpec((MOD_ROWS, d), lambda l, j: (0, 0)),
                  pl.BlockSpec((None, d, TN_MOD), lambda l, j: (l, 0, j)),
                  pl.BlockSpec((None, 1, TN_MOD), lambda l, j: (l, 0, j))],
        out_specs=pl.BlockSpec((None, MOD_ROWS, TN_MOD), lambda l, j: (l, 0, j)),
        out_shape=jax.ShapeDtypeStruct((depth, MOD_ROWS, n), F32),
        compiler_params=_params("parallel", "parallel"),
        name="mod_matmul",
    )(cc, w_mod, b_mod.reshape(depth, 1, n))


def _modulate_kernel(h_ref, sh_ref, sc_ref, o_ref):
    o_ref[...] = (h_ref[...] * (1.0 + sc_ref[...]) + sh_ref[...]).astype(o_ref.dtype)


def _modulate(lay, h, sh, sc):
    vec = pl.BlockSpec((None, 1, D_MODEL), lambda i: (lay.sample_of_tile(i, TM), 0, 0))
    row = pl.BlockSpec((TM, D_MODEL), lambda i: (i, 0))
    return pl.pallas_call(
        _modulate_kernel, grid=(lay.n_tok // TM,),
        in_specs=[row, vec, vec], out_specs=row,
        out_shape=jax.ShapeDtypeStruct((lay.n_tok, D_MODEL), BF16),
        compiler_params=_params("parallel"), name="modulate",
    )(h, sh, sc)


def _ffn_kernel(hm_ref, h_ref, w1_ref, w3_ref, w2_ref, gate_ref, lng_ref, lnb_ref, nsh_ref, nsc_ref,
                ho_ref, hmo_ref, acc_ref):
    j = pl.program_id(1)

    @pl.when(j == 0)
    def _():
        acc_ref[...] = jnp.zeros(acc_ref.shape, F32)

    hm = hm_ref[...]
    a = jnp.dot(hm, w1_ref[...], preferred_element_type=F32)
    b = jnp.dot(hm, w3_ref[...], preferred_element_type=F32)
    p = (a * jax.nn.sigmoid(a) * b).astype(BF16)
    acc_ref[...] += jnp.dot(p, w2_ref[...], preferred_element_type=F32)

    @pl.when(j == pl.num_programs(1) - 1)
    def _():
        upd = (FFN_HALF * gate_ref[...]) * acc_ref[...]
        _post_norm_emit(h_ref[...], upd, lng_ref[...], lnb_ref[...], nsh_ref[...], nsc_ref[...],
                        ho_ref, hmo_ref)


def _half_ffn(lay, rows, hm, h, w1, w3, w2, layer, which, gate, lng, lnb, nsh, nsc):
    vec = pl.BlockSpec((None, 1, D_MODEL), lambda i, j: (lay.sample_of_tile(i, TM), 0, 0))
    par = pl.BlockSpec((1, D_MODEL), lambda i, j: (0, 0))
    row = pl.BlockSpec((TM, D_MODEL), lambda i, j: (i, 0))
    return pl.pallas_call(
        _ffn_kernel,
        grid=(rows // TM, D_FF // TN_FF),
        in_specs=[row, row,
                  pl.BlockSpec((None, None, D_MODEL, TN_FF), lambda i, j: (layer, which, 0, j)),
                  pl.BlockSpec((None, None, D_MODEL, TN_FF), lambda i, j: (layer, which, 0, j)),
                  pl.BlockSpec((None, None, TN_FF, D_MODEL), lambda i, j: (layer, which, j, 0)),
                  vec, par, par, vec, vec],
        out_specs=[row, row],
        out_shape=[jax.ShapeDtypeStruct((rows, D_MODEL), F32),
                   jax.ShapeDtypeStruct((rows, D_MODEL), BF16)],
        scratch_shapes=[pltpu.VMEM((TM, D_MODEL), F32)],
        compiler_params=_params("parallel", "arbitrary"), name="half_ffn",
    )(hm, h, w1, w3, w2, gate, lng, lnb, nsh, nsc)


def _proj_kernel(x_ref, w_ref, o_ref):
    o_ref[...] = jnp.dot(x_ref[...], w_ref[...], preferred_element_type=F32).astype(o_ref.dtype)


def _proj(lay, hm, w, out_dtype, name):
    n = w.shape[1]
    tn = min(TN_PROJ, n)
    tm = TM_PROJ if lay.n_tok % TM_PROJ == 0 else TM
    return pl.pallas_call(
        _proj_kernel, grid=(lay.n_tok // tm, n // tn),
        in_specs=[pl.BlockSpec((tm, D_MODEL), lambda i, j: (i, 0)),
                  pl.BlockSpec((D_MODEL, tn), lambda i, j: (0, j))],
        out_specs=pl.BlockSpec((tm, tn), lambda i, j: (i, j)),
        out_shape=jax.ShapeDtypeStruct((lay.n_tok, n), out_dtype),
        compiler_params=_params("parallel", "parallel"), name=name,
    )(hm, w)


def _proj_qk_kernel(x_ref, w_ref, cos_ref, sin_ref, o_ref, *, n_lat_tiles, is_q):
    i = pl.program_id(0)
    acc = jnp.dot(x_ref[...], w_ref[...], preferred_element_type=F32)
    lane = lax.broadcasted_iota(jnp.int32, (acc.shape[0], LANES), 1)
    low_rows = lax.broadcasted_iota(jnp.int32, (LANES, acc.shape[0]), 0) < ATT_DH

    def emit(rotate):
        if rotate:
            cos, sin = cos_ref[...], sin_ref[...]
            first = (lane % (2 * ROPE_FREQS)) < ROPE_FREQS
        for h in range(acc.shape[1] // LANES):
            t = acc[:, h * LANES:(h + 1) * LANES]
            if rotate:
                partner = jnp.where(first, pltpu.roll(t, LANES - ROPE_FREQS, 1), pltpu.roll(t, ROPE_FREQS, 1))
                t = t * cos + partner * sin
            if is_q:
                tt = (t * (ATT_DH ** -0.5 * LOG2_E)).T
                o_ref[h, 0] = jnp.where(low_rows, tt, 0.0).astype(o_ref.dtype)
                o_ref[h, 1] = jnp.where(low_rows, 0.0, tt).astype(o_ref.dtype)
            else:
                o_ref[:, h * LANES:(h + 1) * LANES] = t.astype(o_ref.dtype)

    @pl.when(i < n_lat_tiles)
    def _():
        emit(True)

    @pl.when(i >= n_lat_tiles)
    def _():
        emit(False)


def _rope_tables(seq):
    rows = seq // GRID_W
    row = jnp.repeat(jnp.arange(rows), GRID_W)
    col = jnp.tile(jnp.arange(GRID_W), rows)
    inv = ROPE_BASE ** (-jnp.arange(ROPE_FREQS, dtype=F32) / ROPE_FREQS)
    ar, ac = row[:, None] * inv, col[:, None] * inv
    cos = jnp.concatenate([jnp.cos(ar), jnp.cos(ar), jnp.cos(ac), jnp.cos(ac)], axis=1)
    sin = jnp.concatenate([-jnp.sin(ar), jnp.sin(ar), -jnp.sin(ac), jnp.sin(ac)], axis=1)
    return jnp.tile(cos, (1, 2)), jnp.tile(sin, (1, 2))


def _proj_qk(lay, hm, w, cos, sin, is_q):
    tps = lay.seq // TM
    kern = functools.partial(_proj_qk_kernel, n_lat_tiles=lay.n_lat // TM, is_q=is_q)
    tab = pl.BlockSpec((TM, LANES), lambda i: (i % tps, 0))
    if is_q:
        out_spec = pl.BlockSpec((ATT_HEADS, 2, LANES, TM), lambda i: (0, 0, 0, i))
        out_shape = jax.ShapeDtypeStruct((ATT_HEADS, 2, LANES, lay.n_tok), BF16)
    else:
        out_spec = pl.BlockSpec((TM, BRANCH_W), lambda i: (i, 0))
        out_shape = jax.ShapeDtypeStruct((lay.n_tok, BRANCH_W), BF16)
    return pl.pallas_call(
        kern, grid=(lay.n_tok // TM,),
        in_specs=[pl.BlockSpec((TM, D_MODEL), lambda i: (i, 0)),
                  pl.BlockSpec((D_MODEL, BRANCH_W), lambda i: (0, 0)), tab, tab],
        out_specs=out_spec, out_shape=out_shape,
        compiler_params=_params("parallel"), name="proj_q" if is_q else "proj_k",
    )(hm, w, cos, sin)


def _proj_v_kernel(x_ref, w_ref, o_ref):
    acc = jnp.dot(x_ref[...], w_ref[...], preferred_element_type=F32)
    tail = ATT_V_ROWS - ATT_DV
    ones_row = jnp.where(lax.broadcasted_iota(jnp.int32, (tail, acc.shape[0]), 0) == 0, 1.0, 0.0)
    for h in range(acc.shape[1] // LANES):
        o_ref[h, :ATT_DV, :] = acc[:, h * LANES:(h + 1) * LANES].T.astype(o_ref.dtype)
        o_ref[h, ATT_DV:, :] = ones_row.astype(o_ref.dtype)


def _proj_v(lay, hm, w):
    return pl.pallas_call(
        _proj_v_kernel, grid=(lay.n_tok // TM,),
        in_specs=[pl.BlockSpec((TM, D_MODEL), lambda i: (i, 0)),
                  pl.BlockSpec((D_MODEL, BRANCH_W), lambda i: (0, 0))],
        out_specs=pl.BlockSpec((ATT_HEADS, ATT_V_ROWS, TM), lambda i: (0, 0, i)),
        out_shape=jax.ShapeDtypeStruct((ATT_HEADS, ATT_V_ROWS, lay.n_tok), BF16),
        compiler_params=_params("parallel"), name="proj_v",
    )(hm, w)


def _attn_kernel(lamv_ref, subln_ref, q0_ref, q1_ref, kc_ref, vc_ref, *rest, n_lat_chunks, tk, lam_init):
    if n_lat_chunks:
        kl_ref, vl_ref, o_ref = rest
    else:
        _, o_ref = rest
    lv = lamv_ref[...]
    lam = (jnp.exp(jnp.sum(lv[0:1] * lv[1:2], axis=-1, keepdims=True))
           - jnp.exp(jnp.sum(lv[2:3] * lv[3:4], axis=-1, keepdims=True)) + lam_init)

    def first(qt, k, vt):
        s = jnp.dot(k, qt, preferred_element_type=F32)
        m = jnp.max(s, axis=0, keepdims=True)
        return m, jnp.dot(vt, jnp.exp2(s - m).astype(BF16), preferred_element_type=F32)

    def update(state, qt, k, vt):
        m, acc = state
        s = jnp.dot(k, qt, preferred_element_type=F32)
        m_new = jnp.maximum(m, jnp.max(s, axis=0, keepdims=True))
        e = jnp.exp2(s - m_new).astype(BF16)
        return m_new, jnp.exp2(m - m_new) * acc + jnp.dot(vt, e, preferred_element_type=F32)

    n_q = q0_ref.shape[1]
    qt = jnp.concatenate([q0_ref[...], q1_ref[...]], axis=1)
    state = first(qt, kc_ref[...], vc_ref[...])
    if n_lat_chunks:
        def body(c, st):
            off = pl.multiple_of(c * tk, tk)
            return update(st, qt, kl_ref[pl.ds(off, tk), :], vl_ref[:, pl.ds(off, tk)])

        state = lax.fori_loop(0, n_lat_chunks, body, state)
    acc = state[1]
    out = acc[:ATT_DV] / acc[ATT_DV:ATT_DV + 1]
    o = out[:, :n_q] - lam * out[:, n_q:]
    o = o * lax.rsqrt(jnp.mean(o * o, axis=0, keepdims=True) + RMS_EPS) * subln_ref[...] * (1.0 - lam_init)
    o_ref[...] = o.T.astype(o_ref.dtype)


def _diff_attention(lay, qt, k, vt, att_lam, subln, lam_init):
    b_, s_, c_ = lay.batch, lay.seq, lay.ctx
    lamv = pl.BlockSpec((4, ATT_DH), lambda *_: (0, 0))
    sub = pl.BlockSpec((ATT_DV, 1), lambda *_: (0, 0))
    out_shape = jax.ShapeDtypeStruct((lay.n_tok, BRANCH_W), BF16)
    ctx_blk = lay.n_lat // c_

    def specs(tq, q_blk):
        return [lamv, sub,
                pl.BlockSpec((None, None, LANES, tq), lambda b, h, i: (h, 0, 0, q_blk(b, i))),
                pl.BlockSpec((None, None, LANES, tq), lambda b, h, i: (h, 1, 0, q_blk(b, i))),
                pl.BlockSpec((c_, LANES), lambda b, h, i: (ctx_blk + b, h)),
                pl.BlockSpec((None, ATT_V_ROWS, c_), lambda b, h, i: (h, 0, ctx_blk + b))]

    tq, tk = min(ATT_TQ, s_), min(ATT_TK, s_)
    n_q = s_ // tq
    lat_blk = lambda b, i: b * n_q + i
    lat = pl.pallas_call(
        functools.partial(_attn_kernel, n_lat_chunks=s_ // tk, tk=tk, lam_init=lam_init),
        grid=(b_, ATT_HEADS, n_q),
        in_specs=specs(tq, lat_blk) + [
            pl.BlockSpec((s_, LANES), lambda b, h, i: (b, h)),
            pl.BlockSpec((None, ATT_V_ROWS, s_), lambda b, h, i: (h, 0, b))],
        out_specs=pl.BlockSpec((tq, LANES), lambda b, h, i: (lat_blk(b, i), h)),
        out_shape=out_shape,
        compiler_params=_params("parallel", "parallel", "arbitrary"), name="attn_latent",
    )(att_lam, subln, qt, qt, k, vt, k, vt)
    ctx_q = lambda b, i: ctx_blk + b
    return pl.pallas_call(
        functools.partial(_attn_kernel, n_lat_chunks=0, tk=0, lam_init=lam_init),
        grid=(b_, ATT_HEADS, 1),
        in_specs=specs(c_, ctx_q) + [pl.BlockSpec(memory_space=pl.ANY)],
        out_specs=pl.BlockSpec((c_, LANES), lambda b, h, i: (ctx_q(b, i), h)),
        out_shape=out_shape, input_output_aliases={6: 0},
        compiler_params=_params("parallel", "parallel", "arbitrary"), name="attn_context",
    )(att_lam, subln, qt, qt, k, vt, lat)


def _ssd_prep_kernel(prev_ref, cur_ref, next_ref, dt_ref, cw_ref, cb_ref, dtb_ref,
                     xs_ref, bm_ref, cm_ref, dts_ref, dtst_ref, ext_ref, *, tiles_per_seq, n_lat_tiles):
    i = pl.program_id(0)
    is_ctx = i >= n_lat_tiles
    first = jnp.logical_or(is_ctx, i % tiles_per_seq == 0)
    last = jnp.logical_or(is_ctx, i % tiles_per_seq == tiles_per_seq - 1)
    tm = cur_ref.shape[0]
    ext_ref[0:SUBLANES, :] = jnp.where(first, 0.0, prev_ref[...])
    ext_ref[SUBLANES:SUBLANES + tm, :] = cur_ref[...]
    ext_ref[SUBLANES + tm:, :] = jnp.where(last, 0.0, next_ref[...])
    acc = jnp.zeros(cur_ref.shape, F32) + cb_ref[...]
    for k in range(SSD_CONV):
        start = SUBLANES + k - SSD_CONV // 2
        acc = acc + ext_ref[start:start + tm, :] * cw_ref[k:k + 1, :]
    act = acc * jax.nn.sigmoid(acc)
    xs_ref[...] = act[:, :SSD_INNER]
    bm_ref[...] = act[:, SSD_INNER:SSD_INNER + SSD_BC_W].astype(bm_ref.dtype)
    cm_ref[...] = act[:, SSD_INNER + SSD_BC_W:].astype(cm_ref.dtype)
    x = dt_ref[...] + dtb_ref[...]
    sp = jnp.maximum(x, 0.0) + jnp.log1p(jnp.exp(-jnp.abs(x)))
    dts_ref[0] = sp
    dts_ref[1] = pltpu.roll(sp, DT_PAD - SSD_HEADS, 1)
    spt = sp.T
    dtst_ref[0] = spt[0:SSD_HEADS]
    dtst_ref[1] = spt[SSD_HEADS:2 * SSD_HEADS]


def _ssd_prep(lay, xbc, dt, conv_w, conv_b, dt_bias):
    n, tm = lay.n_tok, SSD_TM
    sub_per_tile = tm // SUBLANES
    n_sub = n // SUBLANES
    kern = functools.partial(_ssd_prep_kernel, tiles_per_seq=lay.seq // tm, n_lat_tiles=lay.n_lat // tm)
    row = lambda w: pl.BlockSpec((tm, w), lambda i: (i, 0))
    return pl.pallas_call(
        kern, grid=(n // tm,),
        in_specs=[pl.BlockSpec((SUBLANES, SSD_XBC_W), lambda i: (jnp.maximum(i * sub_per_tile - 1, 0), 0)),
                  row(SSD_XBC_W),
                  pl.BlockSpec((SUBLANES, SSD_XBC_W),
                               lambda i: (jnp.minimum((i + 1) * sub_per_tile, n_sub - 1), 0)),
                  row(DT_PAD),
                  pl.BlockSpec((SUBLANES, SSD_XBC_W), lambda i: (0, 0)),
                  pl.BlockSpec((1, SSD_XBC_W), lambda i: (0, 0)),
                  pl.BlockSpec((1, DT_PAD), lambda i: (0, 0))],
        out_specs=[row(SSD_INNER), row(SSD_BC_W), row(SSD_BC_W),
                   pl.BlockSpec((2, tm, DT_PAD), lambda i: (0, i, 0)),
                   pl.BlockSpec((2, SSD_HEADS, tm), lambda i: (0, 0, i))],
        out_shape=[jax.ShapeDtypeStruct((n, SSD_INNER), F32),
                   jax.ShapeDtypeStruct((n, SSD_BC_W), BF16),
                   jax.ShapeDtypeStruct((n, SSD_BC_W), BF16),
                   jax.ShapeDtypeStruct((2, n, DT_PAD), F32),
                   jax.ShapeDtypeStruct((2, SSD_HEADS, n), F32)],
        scratch_shapes=[pltpu.VMEM((tm + 2 * SUBLANES, SSD_XBC_W), F32)],
        compiler_params=_params("parallel"), name="ssd_prep",
    )(xbc, xbc, xbc, dt, conv_w, conv_b, dt_bias)


def _split_dot(a, x, x_is_lhs=False):
    out = None
    r = x
    for _ in range(3):
        t = r.astype(BF16)
        r = r - t.astype(F32)
        d = (jnp.dot(t, a, preferred_element_type=F32) if x_is_lhs
             else jnp.dot(a, t, preferred_element_type=F32))
        out = d if out is None else out + d
    return out


def _ssd_kernel(xs_ref, bm_ref, cm_ref, dts_ref, dtst_ref, arow_ref, acol_ref, y_ref, h_ref):
    sign = 1 - 2 * pl.program_id(1)

    @pl.when(pl.program_id(2) == 0)
    def _():
        h_ref[...] = jnp.zeros(h_ref.shape, F32)

    ll = xs_ref.shape[0]
    r = lax.broadcasted_iota(jnp.int32, (ll, ll), 0)
    c = lax.broadcasted_iota(jnp.int32, (ll, ll), 1)
    mask = (r - c) * sign >= 0
    mask_t = (r - c) * sign <= 0
    one_hot = lambda m: jnp.where(m, 1.0, 0.0).astype(BF16)
    dts_t = dtst_ref[...]
    da = dts_ref[...] * arow_ref[...]
    da_t = dts_t * acol_ref[...]
    cs = _split_dot(one_hot(mask), da)
    cs_t = _split_dot(one_hot(mask_t), da_t, x_is_lhs=True)
    tot = jnp.sum(da, axis=0, keepdims=True)
    from_start = jnp.exp(cs)
    chunk_decay = jnp.exp(tot)
    src_w_t = dts_t * jnp.exp(jnp.sum(da_t, axis=1, keepdims=True) - cs_t)
    lane = lax.broadcasted_iota(jnp.int32, (ll, LANES), 1)
    lo_half = lane < SSD_P

    def pair_expand(t, hd):
        rows = t.shape[0]
        return jnp.where(lo_half[:rows], jnp.broadcast_to(t[:, hd:hd + 1], (rows, LANES)),
                         jnp.broadcast_to(t[:, hd + 1:hd + 2], (rows, LANES)))

    gw = SSD_HPG * SSD_P
    for g in range(SSD_GROUPS):
        bg = bm_ref[:, g * SSD_N:(g + 1) * SSD_N]
        cg = cm_ref[:, g * SSD_N:(g + 1) * SSD_N]
        cb = lax.dot_general(cg, bg, (((1,), (1,)), ((), ())), preferred_element_type=F32)
        bg_t = bg.astype(F32).T
        y_parts, st_parts, fs_parts, cd_parts = [], [], [], []
        for pr in range(SSD_HPG // 2):
            hd = g * SSD_HPG + 2 * pr
            xp = xs_ref[:, hd * SSD_P:(hd + 2) * SSD_P]
            fs_parts.append(pair_expand(from_start, hd))
            cd_parts.append(pair_expand(chunk_decay, hd))
            yp = sp = None
            for k in range(2):
                e = hd + k
                seg = cs[:, e:e + 1] - cs_t[e:e + 1, :]
                dec = jnp.where(mask, jnp.exp(jnp.where(mask, seg, 0.0)), 0.0)
                w = (cb * dec * dts_t[e:e + 1, :]).astype(BF16)
                xk = jnp.where(lo_half if k == 0 else jnp.logical_not(lo_half), xp, 0.0).astype(BF16)
                d = jnp.dot(w, xk, preferred_element_type=F32)
                yp = d if yp is None else yp + d
                s = jnp.dot((bg_t * src_w_t[e:e + 1, :]).astype(BF16), xk, preferred_element_type=F32)
                sp = s if sp is None else sp + s
            y_parts.append(yp)
            st_parts.append(sp)
        h_prev = h_ref[g]
        y_off = jnp.dot(cg, h_prev.astype(BF16), preferred_element_type=F32) * jnp.concatenate(fs_parts, axis=1)
        y_ref[:, g * gw:(g + 1) * gw] = jnp.concatenate(y_parts, axis=1) + y_off
        h_ref[g] = h_prev * jnp.concatenate(cd_parts, axis=1) + jnp.concatenate(st_parts, axis=1)


def _ssd_scan(lay, xs, bm, cm, dts, dtst, a_row, a_col):
    ll = SSD_L
    n_ctx, n_lat = lay.ctx // ll, lay.seq // ll
    ctx_base = lay.n_lat // ll

    def rb(b, d, s):
        cstep = jnp.where(d == 0, s, n_ctx - 1 - s)
        lstep = jnp.where(d == 0, s - n_ctx, n_lat - 1 - (s - n_ctx))
        return jnp.where(s < n_ctx, ctx_base + b * n_ctx + cstep, b * n_lat + lstep)

    row = lambda w: pl.BlockSpec((ll, w), lambda b, d, s: (rb(b, d, s), 0))
    return pl.pallas_call(
        _ssd_kernel, grid=(lay.batch, 2, n_ctx + n_lat),
        in_specs=[row(SSD_INNER), row(SSD_BC_W), row(SSD_BC_W),
                  pl.BlockSpec((None, ll, DT_PAD), lambda b, d, s: (d, rb(b, d, s), 0)),
                  pl.BlockSpec((None, SSD_HEADS, ll), lambda b, d, s: (d, 0, rb(b, d, s))),
                  pl.BlockSpec((None, 1, DT_PAD), lambda b, d, s: (d, 0, 0)),
                  pl.BlockSpec((None, SSD_HEADS, 1), lambda b, d, s: (d, 0, 0))],
        out_specs=pl.BlockSpec((None, ll, SSD_INNER), lambda b, d, s: (d, rb(b, d, s), 0)),
        out_shape=jax.ShapeDtypeStruct((2, lay.n_tok, SSD_INNER), F32),
        scratch_shapes=[pltpu.VMEM((SSD_GROUPS, SSD_N, SSD_HPG * SSD_P), F32)],
        compiler_params=_params("parallel", "parallel", "arbitrary"), name="ssd_scan",
    )(xs, bm, cm, dts, dtst, a_row, a_col)


def _ssd_gate_norm_kernel(y0_ref, y1_ref, xs_ref, z_ref, dsk_ref, nw_ref, o_ref):
    z = z_ref[...]
    gated = (y0_ref[...] + y1_ref[...] + xs_ref[...] * dsk_ref[...]) * (z * jax.nn.sigmoid(z))
    nw = nw_ref[...]
    for g in range(SSD_GROUPS):
        sl = slice(g * SSD_NORM_GROUP, (g + 1) * SSD_NORM_GROUP)
        t = gated[:, sl]
        o_ref[:, sl] = (t * lax.rsqrt(jnp.mean(t * t, axis=-1, keepdims=True) + RMS_EPS)
                        * nw[:, sl]).astype(o_ref.dtype)


def _ssd_gate_norm(lay, ydir, xs, z, dsk, nw):
    tm = SSD_TM
    row = pl.BlockSpec((tm, SSD_INNER), lambda i: (i, 0))
    par = pl.BlockSpec((1, SSD_INNER), lambda i: (0, 0))
    return pl.pallas_call(
        _ssd_gate_norm_kernel, grid=(lay.n_tok // tm,),
        in_specs=[pl.BlockSpec((None, tm, SSD_INNER), lambda i: (0, i, 0)),
                  pl.BlockSpec((None, tm, SSD_INNER), lambda i: (1, i, 0)), row, row, par, par],
        out_specs=row, out_shape=jax.ShapeDtypeStruct((lay.n_tok, SSD_INNER), BF16),
        compiler_params=_params("parallel"), name="ssd_gate_norm",
    )(ydir, ydir, xs, z, dsk, nw)


def _s5_operators(lam_re, lam_im, log_step, b_re, b_im, c_re, c_im):
    hp = lax.Precision.HIGHEST
    ll, hh = S5_L, S5_GROUP_CH
    step = jnp.exp(log_step)[..., None, None]
    d = jnp.arange(ll + 1, dtype=F32)
    p_mag = jnp.exp(lam_re[..., None] * step * d)
    p_ang = lam_im[..., None] * step * d
    p_re, p_im = p_mag * jnp.cos(p_ang), p_mag * jnp.sin(p_ang)
    ab_re, ab_im = p_re[..., 1], p_im[..., 1]
    den = lam_re * lam_re + lam_im * lam_im
    k_re = ((ab_re - 1.0) * lam_re + ab_im * lam_im) / den
    k_im = (ab_im * lam_re - (ab_re - 1.0) * lam_im) / den
    bb_re = k_re[..., None] * b_re - k_im[..., None] * b_im
    bb_im = k_re[..., None] * b_im + k_im[..., None] * b_re
    cp_re = c_re[..., None] * p_re[:, :, None] - c_im[..., None] * p_im[:, :, None]
    cp_im = c_re[..., None] * p_im[:, :, None] + c_im[..., None] * p_re[:, :, None]
    kern = (jnp.einsum('zghnd,zgnk->zgdhk', cp_re, bb_re, precision=hp)
            - jnp.einsum('zghnd,zgnk->zgdhk', cp_im, bb_im, precision=hp))
    s_idx = jnp.arange(ll)[:, None]
    l_idx = jnp.arange(ll)[None, :]
    t_f = jnp.where((l_idx >= s_idx)[None, :, :, None, None], kern[0][:, jnp.clip(l_idx - s_idx, 0, ll)], 0.0)
    t_b = jnp.where((s_idx >= l_idx)[None, :, :, None, None], kern[1][:, jnp.clip(s_idx - l_idx, 0, ll)], 0.0)
    toep = (t_f + t_b).transpose(0, 1, 4, 2, 3).reshape(S5_GROUPS, S5_CW, S5_CW)

    def state_in(z, powers):
        pr, pi = p_re[z][..., powers], p_im[z][..., powers]
        re = pr[..., None] * bb_re[z][:, :, None] - pi[..., None] * bb_im[z][:, :, None]
        im = pr[..., None] * bb_im[z][:, :, None] + pi[..., None] * bb_re[z][:, :, None]
        re = re.transpose(0, 2, 3, 1).reshape(S5_GROUPS, S5_CW, S5_N)
        im = im.transpose(0, 2, 3, 1).reshape(S5_GROUPS, S5_CW, S5_N)
        return jnp.concatenate([re, im], axis=-1)

    ws_f = state_in(0, ll - 1 - jnp.arange(ll))
    ws_b = state_in(1, jnp.arange(ll))
    w1 = jnp.concatenate([toep, ws_f, ws_b], axis=-1)

    def state_out(z, powers):
        re = cp_re[z][..., powers].transpose(0, 2, 3, 1).reshape(S5_GROUPS, S5_N, S5_CW)
        im = cp_im[z][..., powers].transpose(0, 2, 3, 1).reshape(S5_GROUPS, S5_N, S5_CW)
        return jnp.concatenate([re, -im], axis=1)

    wo = jnp.concatenate([state_out(0, jnp.arange(ll) + 1), state_out(1, ll - jnp.arange(ll))], axis=1)
    ar, ai = p_re[..., ll], p_im[..., ll]
    a1 = jnp.concatenate([ar, ar], axis=-1)
    a2 = jnp.concatenate([-ai, ai], axis=-1)
    zeros = jnp.zeros_like(a1[0])
    av = jnp.stack([a1[0], a2[0], a1[1], a2[1], zeros, zeros, zeros, zeros], axis=1)
    return w1.astype(BF16), wo.astype(BF16), av


def _s5_kernel(x_ref, w1_ref, wo_ref, av_ref, y_ref, u_scr, y_scr, sf_ref, sb_ref, *, n_ctx_tiles, n_tiles):
    rows = x_ref.shape[0]
    row_tile = rows // S5_ROW_TILES
    nst = 2 * S5_N
    lane_blk = lax.broadcasted_iota(jnp.int32, (row_tile, LANES), 1) // S5_GROUP_CH

    def block_transpose(arrs):
        a = list(arrs)
        k = S5_GB // 2
        while k:
            bit = (lane_blk & k) != 0
            for i in range(S5_GB):
                if not i & k:
                    lo, hi = a[i], a[i + k]
                    a[i] = jnp.where(bit, pltpu.roll(hi, k * S5_GROUP_CH, 1), lo)
                    a[i + k] = jnp.where(bit, hi, pltpu.roll(lo, LANES - k * S5_GROUP_CH, 1))
            k //= 2
        return a

    def gather(r, _):
        r0 = pl.multiple_of(r * row_tile, BF16_ROWS)
        for hv in range(S5_L // S5_GB):
            per_group = block_transpose(
                [x_ref[pl.ds(r0, row_tile), hv * S5_GB + sl, :] for sl in range(S5_GB)])
            for g in range(S5_GB):
                u_scr[g, pl.ds(r0, row_tile), hv * LANES:(hv + 1) * LANES] = per_group[g].astype(u_scr.dtype)
        return 0

    lax.fori_loop(0, S5_ROW_TILES, gather, 0)

    for g in range(S5_GB):
        p = jnp.dot(u_scr[g], w1_ref[g], preferred_element_type=F32)
        y_scr[g] = p[:, :S5_CW]
        sf_ref[:, g * nst:(g + 1) * nst] = p[:, S5_CW:S5_CW + nst]
        sb_ref[:, g * nst:(g + 1) * nst] = p[:, S5_CW + nst:]

    wide = S5_GB * nst
    half = SUBLANES // 2
    coef = lambda k: jnp.broadcast_to(
        jnp.concatenate([av_ref[g][k:k + 1] for g in range(S5_GB)], axis=1), (half, wide))
    a1f, a2f, a1b, a2b = coef(0), coef(1), coef(2), coef(3)
    low = lax.broadcasted_iota(jnp.int32, (SUBLANES, wide), 1) % nst < S5_N
    swap = lambda t: jnp.where(low, pltpu.roll(t, wide - S5_N, 1), pltpu.roll(t, S5_N, 1))

    def body(j, carry):
        hf, hfs, hb, hbs = carry
        of = pl.multiple_of(j * SUBLANES, SUBLANES)
        s = sf_ref[pl.ds(of, SUBLANES), :]
        ss = swap(s)
        h1 = a1f * hf + a2f * hfs + s[:half]
        h1s = a1f * hfs - a2f * hf + ss[:half]
        sf_ref[pl.ds(of, SUBLANES), :] = jnp.concatenate([hf, h1], axis=0)
        h2 = a1f * h1 + a2f * h1s + s[half:]
        h2s = a1f * h1s - a2f * h1 + ss[half:]
        jb = jnp.where(j < n_ctx_tiles, n_ctx_tiles - 1 - j, n_tiles - 1 - (j - n_ctx_tiles))
        ob = pl.multiple_of(jb * SUBLANES, SUBLANES)
        s = sb_ref[pl.ds(ob, SUBLANES), :]
        ss = swap(s)
        g1 = a1b * hb + a2b * hbs + s[half:]
        g1s = a1b * hbs - a2b * hb + ss[half:]
        sb_ref[pl.ds(ob, SUBLANES), :] = jnp.concatenate([g1, hb], axis=0)
        g2 = a1b * g1 + a2b * g1s + s[:half]
        g2s = a1b * g1s - a2b * g1 + ss[:half]
        return h2, h2s, g2, g2s

    z = jnp.zeros((half, wide), F32)
    lax.fori_loop(0, n_tiles, body, (z, z, z, z))

    for g in range(S5_GB):
        wo = wo_ref[g]
        y_scr[g] += (
            jnp.dot(sf_ref[:, g * nst:(g + 1) * nst].astype(BF16), wo[:nst], preferred_element_type=F32)
            + jnp.dot(sb_ref[:, g * nst:(g + 1) * nst].astype(BF16), wo[nst:], preferred_element_type=F32))

    def scatter(r, _):
        r0 = pl.multiple_of(r * row_tile, BF16_ROWS)
        for hv in range(S5_L // S5_GB):
            per_lag = block_transpose(
                [y_scr[g, pl.ds(r0, row_tile), hv * LANES:(hv + 1) * LANES] for g in range(S5_GB)])
            for sl in range(S5_GB):
                y_ref[pl.ds(r0, row_tile), hv * S5_GB + sl, :] = per_lag[sl]
        return 0

    lax.fori_loop(0, S5_ROW_TILES, scatter, 0)


def _s5_chunked(lay, x, w1, wo, av):
    rows = lay.n_tok // S5_L
    n_ctx_rows = lay.batch * lay.ctx // S5_L
    assert rows % (S5_ROW_TILES * BF16_ROWS) == 0
    kern = functools.partial(_s5_kernel, n_ctx_tiles=n_ctx_rows // SUBLANES, n_tiles=rows // SUBLANES)
    blk = pl.BlockSpec((rows, None, S5_L, LANES), lambda g: (0, g, 0, 0), pipeline_mode=pl.Buffered(1))
    return pl.pallas_call(
        kern, grid=(S5_NB,),
        in_specs=[blk,
                  pl.BlockSpec((S5_GB, S5_CW, S5_CW + 4 * S5_N), lambda g: (g, 0, 0)),
                  pl.BlockSpec((S5_GB, 4 * S5_N, S5_CW), lambda g: (g, 0, 0)),
                  pl.BlockSpec((S5_GB, SUBLANES, 2 * S5_N), lambda g: (g, 0, 0))],
        out_specs=blk,
        out_shape=jax.ShapeDtypeStruct((rows, S5_NB, S5_L, LANES), F32),
        scratch_shapes=[pltpu.VMEM((S5_GB, rows, S5_CW), BF16), pltpu.VMEM((S5_GB, rows, S5_CW), F32),
                        pltpu.VMEM((rows, S5_GB * 2 * S5_N), F32), pltpu.VMEM((rows, S5_GB * 2 * S5_N), F32)],
        compiler_params=_params("parallel"), name="s5_chunked",
    )(x, w1, wo, av)


def _s5_tile_index(lay, i):
    n_lat_tiles, per_seq = lay.n_lat // S5_TM, lay.seq // S5_TM
    return jnp.where(i < n_lat_tiles, 1 + i % per_seq, 0), jnp.where(i < n_lat_tiles, i // per_seq, i - n_lat_tiles)


def _s5_chunk_spec(lay):
    return pl.BlockSpec((S5_TM // S5_L, None, S5_NB, S5_L, LANES),
                        lambda i: (*_s5_tile_index(lay, i), 0, 0, 0))


def _proj_u_kernel(x_ref, w_ref, o_ref):
    acc = jnp.dot(x_ref[...], w_ref[...], preferred_element_type=F32)
    for nb in range(S5_NB):
        o_ref[:, nb] = acc[:, nb * LANES:(nb + 1) * LANES].reshape(S5_TM // S5_L, S5_L, LANES)


def _proj_u(lay, hm, w):
    chunks = (lay.seq + lay.ctx) // S5_L
    return pl.pallas_call(
        _proj_u_kernel, grid=(lay.n_tok // S5_TM,),
        in_specs=[pl.BlockSpec((S5_TM, D_MODEL), lambda i: (i, 0)),
                  pl.BlockSpec((D_MODEL, S5_CH), lambda i: (0, 0))],
        out_specs=_s5_chunk_spec(lay),
        out_shape=jax.ShapeDtypeStruct((chunks, lay.batch, S5_NB, S5_L, LANES), F32),
        compiler_params=_params("parallel"), name="proj_u",
    )(hm, w)


def _s5_glu_kernel(ys_ref, u_ref, dsk_ref, w_ref, b_ref, o_ref):
    natural = lambda ref: jnp.concatenate([ref[:, nb].reshape(S5_TM, LANES) for nb in range(S5_NB)], axis=1)
    t = natural(ys_ref) + natural(u_ref) * dsk_ref[...]
    t = 0.5 * t * (1.0 + jnp.tanh(math.sqrt(2.0 / math.pi) * (t + 0.044715 * (t * t * t))))
    gate = jnp.dot(t.astype(BF16), w_ref[...], preferred_element_type=F32) + b_ref[...]
    o_ref[...] = (t * jax.nn.sigmoid(gate)).astype(o_ref.dtype)


def _s5_glu(lay, ys, u, dsk, w, b):
    par = pl.BlockSpec((1, S5_CH), lambda i: (0, 0))
    return pl.pallas_call(
        _s5_glu_kernel, grid=(lay.n_tok // S5_TM,),
        in_specs=[_s5_chunk_spec(lay), _s5_chunk_spec(lay), par,
                  pl.BlockSpec((S5_CH, S5_CH), lambda i: (0, 0)), par],
        out_specs=pl.BlockSpec((S5_TM, S5_CH), lambda i: (i, 0)),
        out_shape=jax.ShapeDtypeStruct((lay.n_tok, S5_CH), BF16),
        compiler_params=_params("parallel"), name="s5_glu",
    )(ys, u, dsk, w, b)


def _merge_kernel(hm_ref, oa_ref, os_ref, o5_ref, ga_ref, gs_ref, g5_ref, wa_ref, ws_ref, w5_ref, o_ref):
    hm = hm_ref[...]
    acc = None
    for o, wg, w in ((oa_ref, ga_ref, wa_ref), (os_ref, gs_ref, ws_ref), (o5_ref, g5_ref, w5_ref)):
        gate = jax.nn.sigmoid(jnp.dot(hm, wg[...], preferred_element_type=F32))
        t = gate * jnp.dot(o[...], w[...], preferred_element_type=F32)
        acc = t if acc is None else acc + t
    o_ref[...] = acc.astype(o_ref.dtype)


def _merge(rows, hm, o_att, o_ssd, o_s5, w_gate, w_branch):
    nt = D_MODEL // TN_MERGE
    row = pl.BlockSpec((TM, BRANCH_W), lambda i, j: (i, 0))
    gate = lambda k: pl.BlockSpec((D_MODEL, TN_MERGE), lambda i, j: (0, k * nt + j))
    wb = lambda k: pl.BlockSpec((None, BRANCH_W, TN_MERGE), lambda i, j: (k, 0, j))
    return pl.pallas_call(
        _merge_kernel, grid=(rows // TM, nt),
        in_specs=[pl.BlockSpec((TM, D_MODEL), lambda i, j: (i, 0)), row, row, row,
                  gate(0), gate(1), gate(2), wb(0), wb(1), wb(2)],
        out_specs=pl.BlockSpec((TM, TN_MERGE), lambda i, j: (i, j)),
        out_shape=jax.ShapeDtypeStruct((rows, D_MODEL), BF16),
        compiler_params=_params("parallel", "parallel"), name="branch_merge",
    )(hm, o_att, o_ssd, o_s5, w_gate, w_gate, w_gate, w_branch, w_branch, w_branch)


def _out_norm_kernel(mx_ref, h_ref, w_ref, gate_ref, lng_ref, lnb_ref, nsh_ref, nsc_ref, ho_ref, hmo_ref):
    y = jnp.dot(mx_ref[...], w_ref[...], preferred_element_type=F32)
    _post_norm_emit(h_ref[...], gate_ref[...] * y, lng_ref[...], lnb_ref[...], nsh_ref[...], nsc_ref[...],
                    ho_ref, hmo_ref)


def _out_norm(lay, rows, mixed, h, w_out, gate, lng, lnb, nsh, nsc):
    vec = pl.BlockSpec((None, 1, D_MODEL), lambda i: (lay.sample_of_tile(i, TM), 0, 0))
    par = pl.BlockSpec((1, D_MODEL), lambda i: (0, 0))
    row = pl.BlockSpec((TM, D_MODEL), lambda i: (i, 0))
    return pl.pallas_call(
        _out_norm_kernel, grid=(rows // TM,),
        in_specs=[row, row, pl.BlockSpec((D_MODEL, D_MODEL), lambda i: (0, 0)), vec, par, par, vec, vec],
        out_specs=[row, row],
        out_shape=[jax.ShapeDtypeStruct((rows, D_MODEL), F32),
                   jax.ShapeDtypeStruct((rows, D_MODEL), BF16)],
        compiler_params=_params("parallel"), name="out_norm",
    )(mixed, h, w_out, gate, lng, lnb, nsh, nsc)


def _pad_cols(t, width):
    return jnp.pad(t, [(0, 0)] * (t.ndim - 1) + [(0, width - t.shape[-1])])


def _token_mixer(lay, rows_out, hm, rope, lam_init, w_in, att_lam, att_subln, conv_w, conv_b, a_log, dt_bias,
                 ssd_d, ssd_norm, s5_ops, s5_d, glu_w, glu_b, w_branch):
    cuts = [0]
    for w in (BRANCH_W, BRANCH_W, BRANCH_W, SSD_INNER, SSD_XBC_W, 2 * SSD_HEADS, S5_CH, N_BRANCH * D_MODEL):
        cuts.append(cuts[-1] + w)
    w_q, w_k, w_v, w_z, w_xbc, w_dt, w_u, w_g = (
        w_in[:, a:b].astype(BF16) for a, b in zip(cuts[:-1], cuts[1:]))
    qt = _proj_qk(lay, hm, w_q, *rope, is_q=True)
    k = _proj_qk(lay, hm, w_k, *rope, is_q=False)
    vt = _proj_v(lay, hm, w_v)
    z = _proj(lay, hm, w_z, F32, "proj_z")
    xbc = _proj(lay, hm, w_xbc, F32, "proj_xbc")
    dt = _proj(lay, hm, _pad_cols(w_dt, DT_PAD), F32, "proj_dt")
    u = _proj_u(lay, hm, w_u)

    o_att = _diff_attention(lay, qt, k, vt, att_lam, att_subln.reshape(ATT_DV, 1), lam_init)

    conv_w8 = jnp.pad(conv_w, ((0, SUBLANES - SSD_CONV), (0, 0)))
    xs, bm, cm, dts, dtst = _ssd_prep(lay, xbc, dt, conv_w8, conv_b.reshape(1, -1),
                                      _pad_cols(dt_bias.reshape(1, -1), DT_PAD))
    a = -jnp.exp(a_log.astype(F32))
    ydir = _ssd_scan(lay, xs, bm, cm, dts, dtst, _pad_cols(a, DT_PAD)[:, None, :], a[:, :, None])
    o_ssd = _ssd_gate_norm(lay, ydir, xs, z, jnp.repeat(ssd_d, SSD_P).reshape(1, -1),
                           ssd_norm.reshape(1, -1))

    ys = _s5_chunked(lay, u.reshape(lay.n_tok // S5_L, S5_NB, S5_L, LANES), *s5_ops)
    o_s5 = _s5_glu(lay, ys.reshape(u.shape), u, s5_d.reshape(1, -1), glu_w.astype(BF16),
                   glu_b.reshape(1, -1))

    return _merge(rows_out, hm, o_att, o_ssd, o_s5, w_g, w_branch.astype(BF16))


def _trunk(lay, x, c, ctx, c_ctx, w_mod, b_mod, ln_g, ln_b, ffn_w1, ffn_w3, ffn_w2, w_in,
           att_lam, att_subln, ssd_conv_w, ssd_conv_b, ssd_a_log, ssd_dt_bias, ssd_d, ssd_norm,
           s5_lam_re, s5_lam_im, s5_log_step, s5_b_re, s5_b_im, s5_c_re, s5_c_im,
           s5_d, s5_glu_w, s5_glu_b, w_branch, w_out):
    depth = w_mod.shape[0]
    h = jnp.concatenate([x.reshape(lay.n_lat, D_MODEL), ctx.reshape(-1, D_MODEL)], axis=0)
    cc = jnp.concatenate([c, c_ctx[None], jnp.zeros((MOD_ROWS - lay.batch - 1, D_MODEL), F32)], axis=0)
    mod = _mod_all(cc, w_mod, b_mod).reshape(depth, MOD_ROWS, N_MOD, 1, D_MODEL)
    mvec = lambda l, k: mod[l, :, k]
    zero_vec = jnp.zeros((MOD_ROWS, 1, D_MODEL), F32)
    rope = _rope_tables(lay.seq)
    lnp = lambda l, k: (ln_g[l, k].reshape(1, -1), ln_b[l, k].reshape(1, -1))

    ffn_w = (ffn_w1.astype(BF16), ffn_w3.astype(BF16), ffn_w2.astype(BF16))
    hm = _modulate(lay, h, mvec(0, 0), mvec(0, 1))
    for l in range(depth):
        lam_init = LAMBDA_INIT_BASE - LAMBDA_INIT_SPAN * math.exp(-LAMBDA_INIT_RATE * l)
        last = l + 1 == depth
        rows = lay.n_lat if last else lay.n_tok
        h, hm = _half_ffn(lay, lay.n_tok, hm, h, *ffn_w, l, 0, mvec(l, 2), *lnp(l, 0), mvec(l, 3), mvec(l, 4))
        s5_ops = _s5_operators(s5_lam_re[l], s5_lam_im[l], s5_log_step[l], s5_b_re[l], s5_b_im[l],
                               s5_c_re[l], s5_c_im[l])
        mixed = _token_mixer(lay, rows, hm, rope, lam_init, w_in[l], att_lam[l], att_subln[l],
                             ssd_conv_w[l], ssd_conv_b[l], ssd_a_log[l], ssd_dt_bias[l], ssd_d[l],
                             ssd_norm[l], s5_ops, s5_d[l], s5_glu_w[l], s5_glu_b[l], w_branch[l])
        h, hm = _out_norm(lay, rows, mixed, h, w_out[l].astype(BF16), mvec(l, 5), *lnp(l, 1),
                          mvec(l, 6), mvec(l, 7))
        nxt = (zero_vec, zero_vec) if last else (mvec(l + 1, 0), mvec(l + 1, 1))
        h, hm = _half_ffn(lay, rows, hm, h, *ffn_w, l, 1, mvec(l, 8), *lnp(l, 2), *nxt)
    return h.reshape(x.shape)


def kernel(x, c, ctx, c_ctx, w_mod, b_mod, ln_g, ln_b, ffn_w1, ffn_w3, ffn_w2, w_in, att_lam, att_subln, ssd_conv_w, ssd_conv_b, ssd_a_log, ssd_dt_bias, ssd_d, ssd_norm, s5_lam_re, s5_lam_im, s5_log_step, s5_b_re, s5_b_im, s5_c_re, s5_c_im, s5_d, s5_glu_w, s5_glu_b, w_branch, w_out):
    lay = Layout(x.shape[0], x.shape[1], ctx.shape[1])
    return _trunk(lay, x, c, ctx, c_ctx, w_mod, b_mod, ln_g, ln_b, ffn_w1, ffn_w3, ffn_w2, w_in,
                  att_lam, att_subln, ssd_conv_w, ssd_conv_b, ssd_a_log, ssd_dt_bias, ssd_d, ssd_norm,
                  s5_lam_re, s5_lam_im, s5_log_step, s5_b_re, s5_b_im, s5_c_re, s5_c_im,
                  s5_d, s5_glu_w, s5_glu_b, w_branch, w_out)
```
